```python
import math
import jax
import jax.numpy as jnp
from jax import lax
import numpy as np

D_MODEL = 1024
BATCH = 4
SEQ = 4096
DEPTH = 4
DEC_BATCH = 128
DEC_SEQ = 1
PAST_LEN = 2048
PAGE_SIZE = 128

N_MIXERS = 3
KIND_DIFF, KIND_DSA, KIND_GLA = 0, 1, 2
N_DIFF = (DEPTH + 2) // 3
N_DSA = (DEPTH + 1) // 3
N_GLA = DEPTH // 3

DIFF_HEADS = 8
DIFF_HEAD_DIM = 64
DIFF_QK_WIDTH = 2 * DIFF_HEADS * DIFF_HEAD_DIM
DIFF_V_WIDTH = DIFF_HEADS * 2 * DIFF_HEAD_DIM
DIFF_IN = 2 * DIFF_QK_WIDTH + DIFF_V_WIDTH

DSA_HEADS = 16
DSA_HEAD_DIM = 64
DSA_WIDTH = DSA_HEADS * DSA_HEAD_DIM
IDX_HEADS = 8
IDX_DIM = 64
IDX_TOPK_MAX = 256
DSA_IN = 3 * DSA_WIDTH + IDX_HEADS * IDX_DIM + IDX_DIM + IDX_HEADS

GLA_HEADS = 4
GLA_DK = D_MODEL // 2
GLA_DV = D_MODEL
GLA_HEAD_K = GLA_DK // GLA_HEADS
GLA_HEAD_V = GLA_DV // GLA_HEADS
GLA_GATE_RANK = 16
GLA_TAU = 16.0
GLA_CHUNK = 64
GLA_IN = 2 * GLA_DK + 2 * GLA_DV + GLA_GATE_RANK

D_FF = (((8 * D_MODEL + 2) // 3 + 255) // 256) * 256

Q_BLOCK = 128
ROPE_THETA = 10000.0
LN_EPS = 1e-5
DEEPNORM_ALPHA = (2.0 * DEPTH) ** 0.25
DEEPNORM_BETA = (8.0 * DEPTH) ** -0.25
F32 = jnp.float32

kernel_name = 'hybrid_diff_dsa_gla_deepnorm_step'


def _split(a, sizes):
    offs, acc = [], 0
    for s in sizes[:-1]:
        acc += s
        offs.append(acc)
    return jnp.split(a, offs, axis=-1)


def layer_norm(x, g, b):
    xf = x.astype(F32)
    mu = xf.mean(-1, keepdims=True)
    var = jnp.square(xf - mu).mean(-1, keepdims=True)
    return ((xf - mu) * lax.rsqrt(var + LN_EPS) * g + b).astype(x.dtype)


def rms_norm(x, g):
    xf = x.astype(F32)
    return (xf * lax.rsqrt(jnp.square(xf).mean(-1, keepdims=True) + LN_EPS) * g).astype(x.dtype)


def rope(x, pos):
    d = x.shape[-1]
    inv = ROPE_THETA ** (-jnp.arange(0, d, 2, dtype=F32) / d)
    ang = pos.astype(F32)[:, None] * inv[None, :]
    cos = jnp.cos(ang)[None, :, None, :]
    sin = jnp.sin(ang)[None, :, None, :]
    x1 = x[..., : d // 2].astype(F32)
    x2 = x[..., d // 2:].astype(F32)
    return jnp.concatenate([x1 * cos - x2 * sin, x2 * cos + x1 * sin], axis=-1).astype(x.dtype)


def swiglu(x, w_gate_up, w_down):
    g, u = _split(x @ w_gate_up, [D_FF, D_FF])
    return (jax.nn.silu(g) * u) @ w_down


def query_blocks(a):
    b, s = a.shape[:2]
    return jnp.moveaxis(a.reshape(b, s // Q_BLOCK, Q_BLOCK, *a.shape[2:]), 1, 0)


def merge_blocks(a):
    a = jnp.moveaxis(a, 0, 1)
    return a.reshape(a.shape[0], -1, *a.shape[3:])


def gather_pages(pool, page_table):
    g = pool[page_table]
    return g.reshape(g.shape[0], -1, *g.shape[3:])


def gather_rows(pool, page_table, new_rows, pos):
    page = pool.shape[1]
    past_len = page_table.shape[1] * page
    bi = jnp.arange(pos.shape[0])[:, None, None]
    past = jnp.minimum(pos, past_len - 1)
    rows_past = pool[page_table[bi, past // page], past % page]
    rows_new = new_rows[bi, jnp.clip(pos - past_len, 0, new_rows.shape[1] - 1)]
    return jnp.where((pos >= past_len)[..., None, None], rows_new, rows_past)


def diff_project(x, pos, w_in):
    b, s, _ = x.shape
    q, k, v = _split(x @ w_in, [DIFF_QK_WIDTH, DIFF_QK_WIDTH, DIFF_V_WIDTH])
    q = rope(q.reshape(b, s, 2 * DIFF_HEADS, DIFF_HEAD_DIM), pos)
    k = rope(k.reshape(b, s, 2 * DIFF_HEADS, DIFF_HEAD_DIM), pos)
    v = v.reshape(b, s, DIFF_HEADS, 2 * DIFF_HEAD_DIM)
    return q, k, v


def diff_attend(q, k, v, q_pos, k_pos, lam):
    b, nq = q.shape[:2]
    s = jnp.einsum('bqhd,bkhd->bhqk', q, k).astype(F32) * DIFF_HEAD_DIM ** -0.5
    s = jnp.where(k_pos[None, :] <= q_pos[:, None], s, -jnp.inf)
    p = jax.nn.softmax(s, axis=-1).reshape(b, DIFF_HEADS, 2, nq, -1)
    a = p[:, :, 0] - lam * p[:, :, 1]
    return jnp.einsum('bhqk,bkhe->bqhe', a.astype(v.dtype), v)


def diff_finish(o, subln_g, lam_init, w_out):
    b, s = o.shape[:2]
    o = rms_norm(o, subln_g) * (1.0 - lam_init)
    return o.reshape(b, s, DIFF_V_WIDTH) @ w_out


def diff_layer(x_p, x_s, cache_k, cache_v, page_table, w_in, lam_params, subln_g, w_out, lam_init):
    s_p, s_d = x_p.shape[1], x_s.shape[1]
    past_len = page_table.shape[1] * cache_k.shape[1]
    pos_p = jnp.arange(s_p, dtype=jnp.int32)
    pos_s = past_len + jnp.arange(s_d, dtype=jnp.int32)
    pos_all = jnp.arange(past_len + s_d, dtype=jnp.int32)
    lam = (jnp.exp(jnp.sum(lam_params[0] * lam_params[1]).astype(F32))
           - jnp.exp(jnp.sum(lam_params[2] * lam_params[3]).astype(F32)) + lam_init)
    q_p, k_p, v_p = diff_project(x_p, pos_p, w_in)
    o_p = merge_blocks(lax.map(lambda a: diff_attend(a[0], k_p, v_p, a[1], pos_p, lam),
                               (query_blocks(q_p), pos_p.reshape(-1, Q_BLOCK))))
    q_s, k_s, v_s = diff_project(x_s, pos_s, w_in)
    k_all = jnp.concatenate([gather_pages(cache_k, page_table), k_s], axis=1)
    v_all = jnp.concatenate([gather_pages(cache_v, page_table), v_s], axis=1)
    o_s = diff_attend(q_s, k_all, v_all, pos_s, pos_all, lam)
    y_p = diff_finish(o_p, subln_g, lam_init, w_out)
    y_s = diff_finish(o_s, subln_g, lam_init, w_out)
    return y_p, y_s, k_p, v_p, k_s, v_s


def dsa_project(x, pos, w_in):
    b, s, _ = x.shape
    q, k, v, qi, ki, wi = _split(x @ w_in, [DSA_WIDTH, DSA_WIDTH, DSA_WIDTH,
                                            IDX_HEADS * IDX_DIM, IDX_DIM, IDX_HEADS])
    q = rope(q.reshape(b, s, DSA_HEADS, DSA_HEAD_DIM), pos)
    k = rope(k.reshape(b, s, DSA_HEADS, DSA_HEAD_DIM), pos)
    v = v.reshape(b, s, DSA_HEADS, DSA_HEAD_DIM)
    qi = rope(qi.reshape(b, s, IDX_HEADS, IDX_DIM), pos)
    ki = rope(ki.reshape(b, s, 1, IDX_DIM), pos)[:, :, 0]
    wi = wi * IDX_HEADS ** -0.5
    return q, k, v, qi, ki, wi


def index_topk(qi, wi, ki, q_pos, k_pos, n_sel):
    r = jax.nn.relu(jnp.einsum('bqhd,bld->bqhl', qi, ki).astype(F32) * IDX_DIM ** -0.5)
    score = jnp.einsum('bqh,bqhl->bql', wi.astype(F32), r)
    score = jnp.where(k_pos[None, :] <= q_pos[:, None], score, -jnp.inf)
    _, sel = lax.top_k(score, n_sel)
    return sel


def sparse_attend(q, k_sel, v_sel, valid):
    s = jnp.einsum('bqhd,bqkhd->bhqk', q, k_sel).astype(F32) * DSA_HEAD_DIM ** -0.5
    s = jnp.where(valid[:, None], s, -jnp.inf)
    p = jax.nn.softmax(s, axis=-1)
    return jnp.einsum('bhqk,bqkhd->bqhd', p.astype(v_sel.dtype), v_sel)


def dsa_layer(x_p, x_s, cache_k, cache_v, cache_kidx, page_table, w_in, w_out):
    b, s_p, _ = x_p.shape
    bd, s_d, _ = x_s.shape
    past_len = page_table.shape[1] * cache_k.shape[1]
    pos_p = jnp.arange(s_p, dtype=jnp.int32)
    pos_s = past_len + jnp.arange(s_d, dtype=jnp.int32)
    q_p, k_p, v_p, qi_p, ki_p, wi_p = dsa_project(x_p, pos_p, w_in)
    n_sel_p = min(IDX_TOPK_MAX, s_p // 4)
    bi = jnp.arange(b)[:, None, None]

    def blk(a):
        qb, qib, wib, pb = a
        sel = index_topk(qib, wib, ki_p, pb, pos_p, n_sel_p)
        return sparse_attend(qb, k_p[bi, sel], v_p[bi, sel], sel <= pb[None, :, None])

    o_p = merge_blocks(lax.map(blk, (query_blocks(q_p), query_blocks(qi_p), query_blocks(wi_p),
                                     pos_p.reshape(-1, Q_BLOCK))))
    q_s, k_s, v_s, qi_s, ki_s, wi_s = dsa_project(x_s, pos_s, w_in)
    n_keys = past_len + s_d
    n_sel_s = min(IDX_TOPK_MAX, n_keys // 4)
    ki_all = jnp.concatenate([gather_pages(cache_kidx, page_table), ki_s], axis=1)
    sel_s = index_topk(qi_s, wi_s, ki_all, pos_s, jnp.arange(n_keys, dtype=jnp.int32), n_sel_s)
    k_sel = gather_rows(cache_k, page_table, k_s, sel_s)
    v_sel = gather_rows(cache_v, page_table, v_s, sel_s)
    o_s = sparse_attend(q_s, k_sel, v_sel, sel_s <= pos_s[None, :, None])
    y_p = o_p.reshape(b, s_p, DSA_WIDTH) @ w_out
    y_s = o_s.reshape(bd, s_d, DSA_WIDTH) @ w_out
    return y_p, y_s, k_p, v_p, ki_p, k_s, v_s, ki_s


def gla_project(x, w_in, w_gate_up, gate_b):
    b, s, _ = x.shape
    q, k, v, r, g_low = _split(x @ w_in, [GLA_DK, GLA_DK, GLA_DV, GLA_DV, GLA_GATE_RANK])

    def heads(a):
        return a.reshape(b, s, GLA_HEADS, -1).transpose(0, 2, 1, 3)

    log_a = jax.nn.log_sigmoid((g_low @ w_gate_up + gate_b).astype(F32)) / GLA_TAU
    return heads(q) * GLA_HEAD_K ** -0.5, heads(k), heads(v), heads(log_a), r


def gla_chunk(s0, q, k, v, g):
    c = q.shape[2]
    bcum = jnp.cumsum(g, axis=2)
    o_inter = jnp.einsum('bhtk,bhkv->bhtv', q * jnp.exp(bcum), s0)
    causal = jnp.arange(c)[:, None] >= jnp.arange(c)[None, :]
    decay = jnp.exp(jnp.where(causal[None, None, :, :, None],
                              bcum[:, :, :, None, :] - bcum[:, :, None, :, :], -jnp.inf))
    att = jnp.einsum('bhtk,bhsk,bhtsk->bhts', q, k, decay)
    o = o_inter + jnp.einsum('bhts,bhsv->bhtv', att, v)
    b_last = bcum[:, :, -1:]
    s_new = (jnp.exp(b_last[:, :, 0])[..., None] * s0
             + jnp.einsum('bhsk,bhsv->bhkv', k * jnp.exp(b_last - bcum), v))
    return s_new, o


def gla_finish(o, r, norm_g, w_out):
    b, _, s, _ = o.shape
    o = rms_norm(o.transpose(0, 2, 1, 3), norm_g).reshape(b, s, GLA_DV)
    return (o * jax.nn.silu(r)) @ w_out


def gla_layer(x_p, x_s, state, w_in, w_gate_up, gate_b, norm_g, w_out):
    b, s_p, _ = x_p.shape
    q_p, k_p, v_p, g_p, r_p = gla_project(x_p, w_in, w_gate_up, gate_b)
    n_chunks = s_p // GLA_CHUNK

    def to_chunks(a):
        return a.reshape(b, GLA_HEADS, n_chunks, GLA_CHUNK, a.shape[-1]).transpose(2, 0, 1, 3, 4)

    s0 = jnp.zeros((b, GLA_HEADS, GLA_HEAD_K, GLA_HEAD_V), F32)
    s_p_fin, o_c = lax.scan(lambda st, c: gla_chunk(st, *c), s0,
                            (to_chunks(q_p), to_chunks(k_p), to_chunks(v_p), to_chunks(g_p)))
    o_p = o_c.transpose(1, 2, 0, 3, 4).reshape(b, GLA_HEADS, s_p, GLA_HEAD_V)
    q_s, k_s, v_s, g_s, r_s = gla_project(x_s, w_in, w_gate_up, gate_b)
    s_s_fin, o_s = gla_chunk(state.astype(F32), q_s, k_s, v_s, g_s)
    y_p = gla_finish(o_p, r_p, norm_g, w_out)
    y_s = gla_finish(o_s, r_s, norm_g, w_out)
    return y_p, y_s, s_p_fin, s_s_fin


def setup_inputs(seed: int = 0) -> dict:
    key = jax.random.key(seed)
    ks = jax.random.split(key, 32)
    n_pages = PAST_LEN // PAGE_SIZE
    n_used = DEC_BATCH * n_pages
    n_pool = n_used + max(1, n_used // 4)
    page_table = jax.random.permutation(ks[0], n_pool)[:n_used].reshape(DEC_BATCH, n_pages).astype(jnp.int32)

    def nrm(k, shape, scale=1.0):
        return jax.random.normal(k, shape, F32) * scale

    def col_scale(parts):
        return jnp.concatenate([jnp.full((w,), s, F32) for w, s in parts])

    d, beta = D_MODEL, DEEPNORM_BETA
    return {
        'x_prompt': nrm(ks[1], (BATCH, SEQ, d)),
        'x_sample': nrm(ks[2], (DEC_BATCH, DEC_SEQ, d)),
        'cache_diff_k': nrm(ks[3], (N_DIFF, n_pool, PAGE_SIZE, 2 * DIFF_HEADS, DIFF_HEAD_DIM)),
        'cache_diff_v': nrm(ks[4], (N_DIFF, n_pool, PAGE_SIZE, DIFF_HEADS, 2 * DIFF_HEAD_DIM)),
        'cache_dsa_k': nrm(ks[5], (N_DSA, n_pool, PAGE_SIZE, DSA_HEADS, DSA_HEAD_DIM)),
        'cache_dsa_v': nrm(ks[6], (N_DSA, n_pool, PAGE_SIZE, DSA_HEADS, DSA_HEAD_DIM)),
        'cache_dsa_kidx': nrm(ks[7], (N_DSA, n_pool, PAGE_SIZE, IDX_DIM)),
        'state_gla': nrm(ks[8], (N_GLA, DEC_BATCH, GLA_HEADS, GLA_HEAD_K, GLA_HEAD_V), 0.1),
        'page_table': page_table,
        'ln_mix_g': 1.0 + nrm(ks[9], (DEPTH, d), 0.02),
        'ln_mix_b': nrm(ks[10], (DEPTH, d), 0.02),
        'ln_ffn_g': 1.0 + nrm(ks[11], (DEPTH, d), 0.02),
        'ln_ffn_b': nrm(ks[12], (DEPTH, d), 0.02),
        'ffn_w_gate_up': nrm(ks[13], (DEPTH, d, 2 * D_FF), d ** -0.5 * beta),
        'ffn_w_down': nrm(ks[14], (DEPTH, D_FF, d), D_FF ** -0.5 * beta),
        'diff_w_in': nrm(ks[15], (N_DIFF, d, DIFF_IN), d ** -0.5)
                     * col_scale([(2 * DIFF_QK_WIDTH, 1.0), (DIFF_V_WIDTH, beta)]),
        'diff_lambda': nrm(ks[16], (N_DIFF, 4, DIFF_HEAD_DIM), 0.1),
        'diff_subln_g': 1.0 + nrm(ks[17], (N_DIFF, 2 * DIFF_HEAD_DIM), 0.02),
        'diff_w_out': nrm(ks[18], (N_DIFF, DIFF_V_WIDTH, d), DIFF_V_WIDTH ** -0.5 * beta),
        'dsa_w_in': nrm(ks[19], (N_DSA, d, DSA_IN), d ** -0.5)
                    * col_scale([(2 * DSA_WIDTH, 1.0), (DSA_WIDTH, beta),
                                 (IDX_HEADS * IDX_DIM + IDX_DIM + IDX_HEADS, 1.0)]),
        'dsa_w_out': nrm(ks[20], (N_DSA, DSA_WIDTH, d), DSA_WIDTH ** -0.5 * beta),
        'gla_w_in': nrm(ks[21], (N_GLA, d, GLA_IN), d ** -0.5)
                    * col_scale([(2 * GLA_DK, 1.0), (GLA_DV, beta), (GLA_DV + GLA_GATE_RANK, 1.0)]),
        'gla_w_gate_up': nrm(ks[22], (N_GLA, GLA_GATE_RANK, GLA_DK), GLA_GATE_RANK ** -0.5),
        'gla_gate_b': nrm(ks[23], (N_GLA, GLA_DK), 0.1),
        'gla_norm_g': 1.0 + nrm(ks[24], (N_GLA, GLA_HEAD_V), 0.02),
        'gla_w_out': nrm(ks[25], (N_GLA, GLA_DV, d), GLA_DV ** -0.5 * beta),
    }


def reference(x_prompt, x_sample, cache_diff_k, cache_diff_v, cache_dsa_k, cache_dsa_v,
              cache_dsa_kidx, state_gla, page_table, ln_mix_g, ln_mix_b, ln_ffn_g, ln_ffn_b,
              ffn_w_gate_up, ffn_w_down, diff_w_in, diff_lambda, diff_subln_g, diff_w_out,
              dsa_w_in, dsa_w_out, gla_w_in, gla_w_gate_up, gla_gate_b, gla_norm_g, gla_w_out):
    xp, xs = x_prompt, x_sample
    dk_p, dv_p, dk_s, dv_s = [], [], [], []
    sk_p, sv_p, si_p, sk_s, sv_s, si_s = [], [], [], [], [], []
    gs_p, gs_s = [], []
    for i in range(DEPTH):
        kind, j = i % N_MIXERS, i // N_MIXERS
        if kind == KIND_DIFF:
            lam_init = 0.8 - 0.6 * math.exp(-0.3 * i)
            mp, ms, a, bb, c, dd = diff_layer(xp, xs, cache_diff_k[j], cache_diff_v[j], page_table,
                                              diff_w_in[j], diff_lambda[j], diff_subln_g[j],
                                              diff_w_out[j], lam_init)
            dk_p.append(a); dv_p.append(bb); dk_s.append(c); dv_s.append(dd)
        elif kind == KIND_DSA:
            mp, ms, a, bb, c, dd, e, f = dsa_layer(xp, xs, cache_dsa_k[j], cache_dsa_v[j],
                                                   cache_dsa_kidx[j], page_table, dsa_w_in[j], dsa_w_out[j])
            sk_p.append(a); sv_p.append(bb); si_p.append(c)
            sk_s.append(dd); sv_s.append(e); si_s.append(f)
        else:
            mp, ms, a, bb = gla_layer(xp, xs, state_gla[j], gla_w_in[j], gla_w_gate_up[j],
                                      gla_gate_b[j], gla_norm_g[j], gla_w_out[j])
            gs_p.append(a); gs_s.append(bb)
        xp = layer_norm(DEEPNORM_ALPHA * xp + mp.astype(xp.dtype), ln_mix_g[i], ln_mix_b[i])
        xs = layer_norm(DEEPNORM_ALPHA * xs + ms.astype(xs.dtype), ln_mix_g[i], ln_mix_b[i])
        xp = layer_norm(DEEPNORM_ALPHA * xp + swiglu(xp, ffn_w_gate_up[i], ffn_w_down[i]).astype(xp.dtype),
                        ln_ffn_g[i], ln_ffn_b[i])
        xs = layer_norm(DEEPNORM_ALPHA * xs + swiglu(xs, ffn_w_gate_up[i], ffn_w_down[i]).astype(xs.dtype),
                        ln_ffn_g[i], ln_ffn_b[i])
    return (xp, xs,
            jnp.stack(dk_p), jnp.stack(dv_p), jnp.stack(dk_s), jnp.stack(dv_s),
            jnp.stack(sk_p), jnp.stack(sv_p), jnp.stack(si_p),
            jnp.stack(sk_s), jnp.stack(sv_s), jnp.stack(si_s),
            jnp.stack(gs_p), jnp.stack(gs_s))
```

```python
import functools
import math

import jax
import jax.numpy as jnp
from jax import lax
from jax.experimental import pallas as pl
from jax.experimental.pallas import tpu as pltpu

F32 = jnp.float32
BF16 = jnp.bfloat16

D_MODEL = 1024
DEPTH = 4
PAGE_SIZE = 128
DIFF_HEADS = 8
DIFF_HEAD_DIM = 64
DSA_HEADS = 16
DSA_HEAD_DIM = 64
IDX_HEADS = 8
IDX_DIM = 64
IDX_TOPK_MAX = 256
GLA_HEADS = 4
GLA_DK = D_MODEL // 2
GLA_DV = D_MODEL
GLA_HEAD_K = GLA_DK // GLA_HEADS
GLA_HEAD_V = GLA_DV // GLA_HEADS
GLA_GATE_RANK = 16
GLA_TAU = 16.0
D_FF = (((8 * D_MODEL + 2) // 3 + 255) // 256) * 256
ROPE_THETA = 10000.0
LN_EPS = 1e-5
DEEPNORM_ALPHA = (2.0 * DEPTH) ** 0.25

LANES = 128
SUBLANES = 8
VMEM_LIMIT = 56 * 1024 * 1024
NEG = -1e30
INT_MIN = -2 ** 31

N_GROUPS = D_MODEL // LANES


def _cparams(sem):
    return pltpu.CompilerParams(dimension_semantics=sem, vmem_limit_bytes=VMEM_LIMIT)


def _bdot(a, b):
    return jnp.dot(a.astype(BF16), b.astype(BF16), preferred_element_type=F32)


def _bdot_nt(a, b):
    return lax.dot_general(a.astype(BF16), b.astype(BF16), (((1,), (1,)), ((), ())),
                           preferred_element_type=F32)


def _bdot_tn(a, b):
    return lax.dot_general(a.astype(BF16), b.astype(BF16), (((0,), (0,)), ((), ())),
                           preferred_element_type=F32)


def _layer_norm_rows(z, g, b):
    mu = jnp.mean(z, axis=-1, keepdims=True)
    zc = z - mu
    var = jnp.mean(zc * zc, axis=-1, keepdims=True)
    return zc * lax.rsqrt(var + LN_EPS) * g + b


def _sigmoid(x):
    return 1.0 / (1.0 + jnp.exp(-x))


def _rope_tables(pos):
    d = DIFF_HEAD_DIM
    inv = ROPE_THETA ** (-jnp.arange(0, d, 2, dtype=F32) / d)
    ang = pos.astype(F32)[:, None] * inv[None, :]
    cos = jnp.cos(ang)
    sin = jnp.sin(ang)
    return (jnp.concatenate([cos, cos, cos, cos], axis=-1),
            jnp.concatenate([-sin, sin, -sin, sin], axis=-1))


def _rope_apply(y, cos, sin, first_half):
    partner = jnp.where(first_half, pltpu.roll(y, 96, 1), pltpu.roll(y, 32, 1))
    return y * cos + partner * sin


def _linear_kernel(x_ref, w_ref, *rest, n_rope):
    if n_rope:
        cos_ref, sin_ref, o_ref = rest
    else:
        (o_ref,) = rest
    acc = _bdot(x_ref[...], w_ref[...])
    tm, tn = acc.shape
    if not n_rope:
        o_ref[...] = acc
        return
    j = pl.program_id(1)

    @pl.when(j < n_rope)
    def _():
        cos = cos_ref[...]
        sin = sin_ref[...]
        lane = lax.broadcasted_iota(jnp.int32, (tm, LANES), 1)
        first_half = (lane % DIFF_HEAD_DIM) < (DIFF_HEAD_DIM // 2)
        for c in range(tn // LANES):
            sl = slice(c * LANES, (c + 1) * LANES)
            o_ref[:, sl] = _rope_apply(acc[:, sl], cos, sin, first_half)

    @pl.when(j >= n_rope)
    def _():
        o_ref[...] = acc


def _linear(x, w, layer, col0, width, *, tm, tn, rope=None, n_rope=None):
    n, k = x.shape
    assert n % tm == 0 and width % tn == 0 and col0 % tn == 0
    nj = width // tn
    if rope is None:
        n_rope = 0
    elif n_rope is None:
        n_rope = nj
    in_specs = [pl.BlockSpec((tm, k), lambda i, j: (i, 0)),
                pl.BlockSpec((None, k, tn), lambda i, j: (layer, 0, col0 // tn + j))]
    args = [x, w]
    if n_rope:
        p_blocks = rope[0].shape[0] // tm
        assert rope[0].shape[0] % tm == 0
        spec = pl.BlockSpec((tm, LANES), lambda i, j: (i % p_blocks, 0))
        in_specs += [spec, spec]
        args += [rope[0], rope[1]]
    return pl.pallas_call(
        functools.partial(_linear_kernel, n_rope=n_rope),
        out_shape=jax.ShapeDtypeStruct((n, width), F32),
        grid=(n // tm, nj),
        in_specs=in_specs,
        out_specs=pl.BlockSpec((tm, tn), lambda i, j: (i, j)),
        compiler_params=_cparams(("parallel", "arbitrary")),
        name="linear",
    )(*args)


def _rope_tables_t(pos):
    cos, sin = _rope_tables(pos)
    return cos.T, sin.T


def _linear_t_kernel(w_ref, x_ref, *rest, rope):
    if rope:
        cos_ref, sin_ref, o_ref = rest
    else:
        (o_ref,) = rest
    acc = _bdot_nt(w_ref[...], x_ref[...])
    tc, tt = acc.shape
    if not rope:
        o_ref[...] = acc
        return
    cos = cos_ref[...]
    sin = sin_ref[...]
    row = lax.broadcasted_iota(jnp.int32, (LANES, tt), 0)
    first_half = (row % DIFF_HEAD_DIM) < (DIFF_HEAD_DIM // 2)
    for c in range(tc // LANES):
        sl = slice(c * LANES, (c + 1) * LANES)
        y = acc[sl, :]
        partner = jnp.where(first_half, pltpu.roll(y, LANES - DIFF_HEAD_DIM // 2, 0),
                            pltpu.roll(y, DIFF_HEAD_DIM // 2, 0))
        o_ref[sl, :] = y * cos + partner * sin


def _linear_t(x, w_t, nb, *, tc, tt, rope=None):
    n, k = x.shape
    c = w_t.shape[0]
    s = n // nb
    assert n == nb * s and s % tt == 0 and c % tc == 0 and tc % LANES == 0
    nt = s // tt
    in_specs = [pl.BlockSpec((tc, k), lambda b, ti, j: (j, 0)),
                pl.BlockSpec((tt, k), lambda b, ti, j: (b * nt + ti, 0))]
    args = [w_t, x]
    if rope is not None:
        p_blocks = rope[0].shape[1] // tt
        spec = pl.BlockSpec((LANES, tt), lambda b, ti, j: (0, ti % p_blocks))
        in_specs += [spec, spec]
        args += [rope[0], rope[1]]
    return pl.pallas_call(
        functools.partial(_linear_t_kernel, rope=rope is not None),
        out_shape=jax.ShapeDtypeStruct((nb, c, s), F32),
        grid=(nb, nt, c // tc),
        in_specs=in_specs,
        out_specs=pl.BlockSpec((None, tc, tt), lambda b, ti, j: (b, j, ti)),
        compiler_params=_cparams(("parallel", "parallel", "arbitrary")),
        name="linear_t",
    )(*args)


def _linear_heads_kernel(x_ref, w_ref, o_ref):
    xb = x_ref[...].astype(BF16)
    for h in range(o_ref.shape[1]):
        o_ref[:, h, :] = jnp.dot(xb, w_ref[:, h * LANES:(h + 1) * LANES].astype(BF16), preferred_element_type=F32)


def _linear_heads(x, w, layer, col0, n_heads, *, tm):
    n, k = x.shape
    width = n_heads * LANES
    assert n % tm == 0 and col0 % width == 0
    return pl.pallas_call(
        _linear_heads_kernel,
        out_shape=jax.ShapeDtypeStruct((n, n_heads, LANES), F32),
        grid=(n // tm,),
        in_specs=[pl.BlockSpec((tm, k), lambda i: (i, 0)),
                  pl.BlockSpec((None, k, width), lambda i: (layer, 0, col0 // width))],
        out_specs=pl.BlockSpec((tm, n_heads, LANES), lambda i: (i, 0, 0)),
        compiler_params=_cparams(("parallel",)),
        name="linear_heads",
    )(x, w)


def _res_ln_kernel(x_ref, w_ref, r_ref, g_ref, b_ref, o_ref):
    y = _bdot(x_ref[...], w_ref[...])
    z = DEEPNORM_ALPHA * r_ref[...] + y
    o_ref[...] = _layer_norm_rows(z, g_ref[...], b_ref[...])


def _linear_res_ln(x, w, layer, resid, g, b, ln_layer, *, tm):
    n, k = x.shape
    d = w.shape[2]
    assert n % tm == 0
    row = lambda i: (i, 0)
    ln_row = lambda i: (ln_layer, 0, 0)
    return pl.pallas_call(
        _res_ln_kernel,
        out_shape=jax.ShapeDtypeStruct((n, d), F32),
        grid=(n // tm,),
        in_specs=[pl.BlockSpec((tm, k), row), pl.BlockSpec((None, k, d), lambda i: (layer, 0, 0)),
                  pl.BlockSpec((tm, d), row),
                  pl.BlockSpec((None, 1, d), ln_row), pl.BlockSpec((None, 1, d), ln_row)],
        out_specs=pl.BlockSpec((tm, d), row),
        compiler_params=_cparams(("parallel",)),
        name="out_proj_ln",
    )(x, w, resid, g.reshape(-1, 1, d), b.reshape(-1, 1, d))


def _ffn_kernel(x_ref, wg_ref, wu_ref, wd_ref, g_ref, b_ref, o_ref, acc_ref):
    f = pl.program_id(1)

    @pl.when(f == 0)
    def _():
        acc_ref[...] = jnp.zeros_like(acc_ref)

    xb = x_ref[...].astype(BF16)
    gate = jnp.dot(xb, wg_ref[...].astype(BF16), preferred_element_type=F32)
    up = jnp.dot(xb, wu_ref[...].astype(BF16), preferred_element_type=F32)
    h = gate * _sigmoid(gate) * up
    acc_ref[...] += _bdot(h, wd_ref[...])

    @pl.when(f == pl.num_programs(1) - 1)
    def _():
        z = DEEPNORM_ALPHA * x_ref[...] + acc_ref[...]
        o_ref[...] = _layer_norm_rows(z, g_ref[...], b_ref[...])


def _ffn_ln(x, w_gate_up, w_down, g, b, layer, *, tm, tf):
    n, d = x.shape
    dff = w_down.shape[1]
    assert n % tm == 0 and dff % tf == 0
    nf = dff // tf
    ln_row = lambda i, f: (layer, 0, 0)
    return pl.pallas_call(
        _ffn_kernel,
        out_shape=jax.ShapeDtypeStruct((n, d), F32),
        grid=(n // tm, nf),
        in_specs=[pl.BlockSpec((tm, d), lambda i, f: (i, 0)),
                  pl.BlockSpec((None, d, tf), lambda i, f: (layer, 0, f)),
                  pl.BlockSpec((None, d, tf), lambda i, f: (layer, 0, nf + f)),
                  pl.BlockSpec((None, tf, d), lambda i, f: (layer, f, 0)),
                  pl.BlockSpec((None, 1, d), ln_row), pl.BlockSpec((None, 1, d), ln_row)],
        out_specs=pl.BlockSpec((tm, d), lambda i, f: (i, 0)),
        scratch_shapes=[pltpu.VMEM((tm, d), F32)],
        compiler_params=_cparams(("parallel", "arbitrary")),
        name="ffn_ln",
    )(x, w_gate_up, w_gate_up, w_down, g.reshape(-1, 1, d), b.reshape(-1, 1, d))


def _diff_lambda(lam_ref, lam_init):
    l = lam_ref[...]
    a = jnp.sum(l[0:1] * l[1:2], axis=-1, keepdims=True)
    c = jnp.sum(l[2:3] * l[3:4], axis=-1, keepdims=True)
    return jnp.exp(a) - jnp.exp(c) + lam_init


def _rms_rows(o, g):
    return o * lax.rsqrt(jnp.mean(o * o, axis=-1, keepdims=True) + LN_EPS) * g


def _flash_kernel(*refs, mode, t, scale, lam_init):
    if mode == "diff":
        q_ref, k_ref, v_ref, lam_ref, g_ref, o_ref, m_sc, l_sc, acc_sc = refs
    else:
        q_ref, k_ref, v_ref, bias_ref, o_ref, m_sc, l_sc, acc_sc = refs
    qi = pl.program_id(1)
    ki = pl.program_id(2)

    @pl.when(ki == 0)
    def _():
        m_sc[...] = jnp.full_like(m_sc, NEG)
        l_sc[...] = jnp.zeros_like(l_sc)
        acc_sc[...] = jnp.zeros_like(acc_sc)

    @pl.when(ki <= qi)
    def _():
        lane = lax.broadcasted_iota(jnp.int32, (1, LANES), 1)
        lo = lane < DIFF_HEAD_DIM
        if mode == "diff":
            row = lax.broadcasted_iota(jnp.int32, (t, t), 0)
            col = lax.broadcasted_iota(jnp.int32, (t, t), 1)
            allowed = jnp.logical_or(ki < qi, col <= row)
        else:
            bias = bias_ref[...]
        for g in range(N_GROUPS):
            sl = slice(g * LANES, (g + 1) * LANES)
            q = q_ref[:, sl] * scale
            kb = k_ref[sl, :].astype(BF16)
            if mode == "diff":
                vb = v_ref[:, g, :].astype(BF16)
            else:
                vb = v_ref[sl, :].astype(BF16)
            for j in range(2):
                qj = jnp.where(lo if j == 0 else jnp.logical_not(lo), q, 0.0)
                s = jnp.dot(qj.astype(BF16), kb, preferred_element_type=F32)
                if mode == "diff":
                    s = jnp.where(allowed, s, NEG)
                else:
                    s = s + bias
                idx = 2 * g + j
                m_prev = m_sc[idx]
                m_new = jnp.maximum(m_prev, jnp.max(s, axis=-1, keepdims=True))
                alpha = jnp.exp(m_prev - m_new)
                p = jnp.exp(s - m_new)
                l_sc[idx] = alpha * l_sc[idx] + jnp.sum(p, axis=-1, keepdims=True)
                if mode == "diff":
                    pv = jnp.dot(p.astype(BF16), vb, preferred_element_type=F32)
                else:
                    pv = _bdot_nt(p, vb)
                acc_sc[j, :, sl] = alpha * acc_sc[j, :, sl] + pv
                m_sc[idx] = m_new

    @pl.when(ki == qi)
    def _():
        lane = lax.broadcasted_iota(jnp.int32, (1, LANES), 1)
        lo = lane < DIFF_HEAD_DIM
        if mode == "diff":
            lam = _diff_lambda(lam_ref, lam_init)
            gain = g_ref[...] * (1.0 - lam_init)
        for g in range(N_GROUPS):
            sl = slice(g * LANES, (g + 1) * LANES)
            o_lo = acc_sc[0, :, sl] / l_sc[2 * g]
            o_hi = acc_sc[1, :, sl] / l_sc[2 * g + 1]
            if mode == "diff":
                o_ref[:, sl] = _rms_rows(o_lo - lam * o_hi, gain)
            else:
                o_ref[:, sl] = jnp.where(lo, o_lo, o_hi)


def _flash_attention(q, k_t, v, *, mode, t, lam_params=None, subln_g=None, lam_init=0.0, bias=None):
    b, s, d = q.shape
    assert s % t == 0 and d == D_MODEL
    nblk = s // t
    qspec = pl.BlockSpec((None, t, d), lambda bi, qi, ki: (bi, qi, 0))
    tspec = pl.BlockSpec((None, d, t), lambda bi, qi, ki: (bi, 0, jnp.minimum(ki, qi)))
    if mode == "diff":
        vspec = pl.BlockSpec((None, t, DIFF_HEADS, LANES), lambda bi, qi, ki: (bi, jnp.minimum(ki, qi), 0, 0))
    else:
        vspec = tspec
    in_specs = [qspec, tspec, vspec]
    args = [q, k_t, v]
    if mode == "diff":
        in_specs += [pl.BlockSpec((4, DIFF_HEAD_DIM), lambda bi, qi, ki: (0, 0)),
                     pl.BlockSpec((1, LANES), lambda bi, qi, ki: (0, 0))]
        args += [lam_params, subln_g.reshape(1, LANES)]
    else:
        assert bias.shape == (b, nblk, s, t)
        in_specs += [pl.BlockSpec((None, None, t, t), lambda bi, qi, ki: (bi, jnp.minimum(ki, qi), qi, 0))]
        args += [bias]
    return pl.pallas_call(
        functools.partial(_flash_kernel, mode=mode, t=t, scale=DIFF_HEAD_DIM ** -0.5, lam_init=lam_init),
        out_shape=jax.ShapeDtypeStruct((b, s, d), F32),
        grid=(b, nblk, nblk),
        in_specs=in_specs,
        out_specs=qspec,
        scratch_shapes=[pltpu.VMEM((2 * N_GROUPS, t, 1), F32), pltpu.VMEM((2 * N_GROUPS, t, 1), F32),
                        pltpu.VMEM((2, t, d), F32)],
        compiler_params=_cparams(("parallel", "parallel", "arbitrary")),
        name="flash_" + mode,
    )(*args)


def _decode_kernel(pt_ref, q_ref, *refs, mode, n_pp, n_pages, scale, lam_init):
    del pt_ref
    k_refs = refs[:n_pp]
    v_refs = refs[n_pp:2 * n_pp]
    knew_ref, vnew_ref = refs[2 * n_pp:2 * n_pp + 2]
    rest = refs[2 * n_pp + 2:]
    if mode == "diff":
        lam_ref, g_ref, o_ref, qm_sc, m_sc, l_sc, acc_sc, e_sc = rest
    else:
        bias_ref, o_ref, qm_sc, m_sc, l_sc, acc_sc = rest
    step = pl.program_id(1)
    n_sub = 2 * N_GROUPS
    row = lax.broadcasted_iota(jnp.int32, (n_sub, D_MODEL), 0)
    lane = lax.broadcasted_iota(jnp.int32, (n_sub, D_MODEL), 1)

    @pl.when(step == 0)
    def _():
        qm_sc[...] = jnp.where(lane // DIFF_HEAD_DIM == row, q_ref[...] * scale, 0.0)
        m_sc[...] = jnp.full_like(m_sc, NEG)
        l_sc[...] = jnp.zeros_like(l_sc)
        acc_sc[...] = jnp.zeros_like(acc_sc)
        if mode == "diff":
            tok = lax.broadcasted_iota(jnp.int32, (PAGE_SIZE, D_MODEL), 0)
            col = lax.broadcasted_iota(jnp.int32, (PAGE_SIZE, D_MODEL), 1)
            e_sc[...] = jnp.where(col // DIFF_HEADS == tok, 1.0, 0.0).astype(BF16)

    qm = qm_sc[...]
    s_parts = []
    for i in range(n_pp):
        s = _bdot(qm, k_refs[i][...])
        if mode == "dsa":
            s = s + bias_ref[pl.ds(step * n_pp + i, 1), :]
        s_parts.append(s)
    s_all = jnp.concatenate(s_parts, axis=-1)
    m_prev = m_sc[...]
    m_new = jnp.maximum(m_prev, jnp.max(s_all, axis=-1, keepdims=True))
    alpha = jnp.exp(m_prev - m_new)
    p_all = jnp.exp(s_all - m_new)
    l_sc[...] = alpha * l_sc[...] + jnp.sum(p_all, axis=-1, keepdims=True)
    pv = None
    for i in range(n_pp):
        p_i = p_all[:, i * PAGE_SIZE:(i + 1) * PAGE_SIZE]
        if mode == "diff":
            p_rows = jnp.where(lane % DIFF_HEADS == row // 2,
                               jnp.dot(p_i.astype(BF16), e_sc[...], preferred_element_type=F32), 0.0)
            part = _bdot(p_rows, v_refs[i][...])
        else:
            part = _bdot_nt(p_i, v_refs[i][...])
        pv = part if pv is None else pv + part
    acc_sc[...] = alpha * acc_sc[...] + pv
    m_sc[...] = m_new

    @pl.when(step == pl.num_programs(1) - 1)
    def _():
        s_new = jnp.sum(qm * knew_ref[...], axis=-1, keepdims=True)
        if mode == "dsa":
            s_new = s_new + bias_ref[n_pages:n_pages + 1, 0:1]
        m_prev = m_sc[...]
        m_fin = jnp.maximum(m_prev, s_new)
        alpha = jnp.exp(m_prev - m_fin)
        p_new = jnp.exp(s_new - m_fin)
        l_fin = alpha * l_sc[...] + p_new
        if mode == "diff":
            row_h = lax.broadcasted_iota(jnp.int32, (n_sub, LANES), 0) // 2
            v_rows = jnp.zeros((n_sub, LANES), F32)
            for h in range(DIFF_HEADS):
                v_rows = jnp.where(row_h == h, vnew_ref[:, h * LANES:(h + 1) * LANES], v_rows)
            o16 = (alpha * acc_sc[...] + p_new * v_rows) / l_fin
            lam = _diff_lambda(lam_ref, lam_init)
            gain = g_ref[...] * (1.0 - lam_init)
            for h in range(DIFF_HEADS):
                o_h = o16[2 * h:2 * h + 1, :] - lam * o16[2 * h + 1:2 * h + 2, :]
                o_ref[:, h * LANES:(h + 1) * LANES] = _rms_rows(o_h, gain)
        else:
            o16 = (alpha * acc_sc[...] + p_new * vnew_ref[...]) / l_fin
            o_ref[...] = jnp.sum(jnp.where(lane // DSA_HEAD_DIM == row, o16, 0.0), axis=0, keepdims=True)


def _decode_attention(q, cache_k, cache_v, page_ids, k_new, v_new, *, mode, n_pp,
                      lam_params=None, subln_g=None, lam_init=0.0, bias=None):
    bd, d = q.shape
    n_pages = page_ids.shape[1]
    assert n_pages % n_pp == 0 and cache_k.shape[1:] == (d, PAGE_SIZE) and cache_v.shape[1:] == (d, PAGE_SIZE)
    n_steps = n_pages // n_pp
    row_spec = pl.BlockSpec((None, 1, d), lambda b, s, pt: (b, 0, 0))

    def page_spec(i):
        return pl.BlockSpec((None, d, PAGE_SIZE), lambda b, s, pt: (pt[b * n_pages + s * n_pp + i], 0, 0))

    in_specs = [row_spec] + [page_spec(i) for i in range(n_pp)] * 2 + [row_spec, row_spec]
    args = [q.reshape(bd, 1, d)] + [cache_k] * n_pp + [cache_v] * n_pp + [k_new.reshape(bd, 1, d), v_new.reshape(bd, 1, d)]
    n_sub = 2 * N_GROUPS
    scratch = [pltpu.VMEM((n_sub, d), F32), pltpu.VMEM((n_sub, 1), F32), pltpu.VMEM((n_sub, 1), F32)]
    if mode == "diff":
        in_specs += [pl.BlockSpec((4, DIFF_HEAD_DIM), lambda b, s, pt: (0, 0)),
                     pl.BlockSpec((1, LANES), lambda b, s, pt: (0, 0))]
        args += [lam_params, subln_g.reshape(1, LANES)]
        scratch += [pltpu.VMEM((n_sub, LANES), F32), pltpu.VMEM((PAGE_SIZE, d), BF16)]
    else:
        in_specs += [pl.BlockSpec((None, n_pages + 1, PAGE_SIZE), lambda b, s, pt: (b, 0, 0))]
        args += [bias]
        scratch += [pltpu.VMEM((n_sub, d), F32)]
    out = pl.pallas_call(
        functools.partial(_decode_kernel, mode=mode, n_pp=n_pp, n_pages=n_pages,
                          scale=DIFF_HEAD_DIM ** -0.5, lam_init=lam_init),
        out_shape=jax.ShapeDtypeStruct((bd, 1, d), F32),
        grid_spec=pltpu.PrefetchScalarGridSpec(
            num_scalar_prefetch=1,
            grid=(bd, n_steps),
            in_specs=in_specs,
            out_specs=row_spec,
            scratch_shapes=scratch),
        compiler_params=_cparams(("parallel", "arbitrary")),
        name="decode_" + mode,
    )(page_ids.reshape(-1), *args)
    return out.reshape(bd, d)


def _sortable_keys(score):
    bits = lax.bitcast_convert_type(score, jnp.int32)
    key = jnp.where(bits < 0, bits ^ jnp.int32(0x7FFFFFFF), bits)
    return jnp.where(score == 0.0, 0, key)


def _kth_threshold(key_ref, nch, ch, ksel):
    def count_ge(cand):
        def body(c, acc):
            blk = key_ref[pl.ds(pl.multiple_of(c * ch, ch), ch), :]
            hit = jnp.where(blk >= cand, 1, 0).astype(jnp.int32)
            return acc + jnp.sum(hit.reshape(ch // SUBLANES, SUBLANES, LANES), axis=0)

        acc = lax.fori_loop(0, nch, body, jnp.zeros((SUBLANES, LANES), jnp.int32))
        return jnp.sum(acc, axis=0, keepdims=True)

    def bit_body(i, t):
        cand = t ^ lax.shift_left(jnp.int32(1), (31 - i).astype(jnp.int32))
        return jnp.where(count_ge(cand) >= ksel, cand, t)

    t = lax.fori_loop(0, 32, bit_body, jnp.full((1, LANES), INT_MIN, jnp.int32))
    need = ksel - count_ge(t + 1)
    return t, need.astype(F32)


def _tri_ones(n):
    row = lax.broadcasted_iota(jnp.int32, (n, n), 0)
    col = lax.broadcasted_iota(jnp.int32, (n, n), 1)
    return jnp.where(col <= row, 1.0, 0.0).astype(BF16)


def _select_chunk(key, t, need, carry, tri):
    eq = key == t
    pref = jnp.dot(tri, jnp.where(eq, 1.0, 0.0).astype(BF16), preferred_element_type=F32) + carry
    sel = jnp.logical_or(key > t, jnp.logical_and(eq, pref <= need))
    sel = jnp.logical_and(sel, key > INT_MIN)
    return sel, pref[key.shape[0] - 1:, :]


def _dsa_index_prompt_kernel(idx_ref, kd_ref, o_ref, key_sc, *, tq, ch, ksel):
    qi = pl.program_id(1)
    n_ch_total = o_ref.shape[0]
    nch = (qi * tq + tq + ch - 1) // ch
    qblk = idx_ref[...]
    lane = lax.broadcasted_iota(jnp.int32, (1, LANES), 1)
    lo = lane < IDX_DIM
    parts = []
    for h in range(IDX_HEADS):
        grp = qblk[:, (h // 2) * LANES:(h // 2 + 1) * LANES]
        parts.append(jnp.where(lo if h % 2 == 0 else jnp.logical_not(lo), grp, 0.0).astype(BF16))
    qstack = jnp.concatenate(parts, axis=0)
    w = qblk[:, 5 * LANES:6 * LANES] * (IDX_HEADS ** -0.5 * IDX_DIM ** -0.5)
    w_cols = [w[:, h:h + 1] for h in range(IDX_HEADS)]
    qpos = qi * tq + lane

    def score_body(c, _):
        base = pl.multiple_of(c * ch, ch)
        r = _bdot(qstack, kd_ref[:, pl.ds(base, ch)])
        score = jnp.zeros((tq, ch), F32)
        for h in range(IDX_HEADS):
            score = score + w_cols[h] * jnp.maximum(r[h * tq:(h + 1) * tq, :], 0.0)
        kpos = base + lax.broadcasted_iota(jnp.int32, (ch, tq), 0)
        key_sc[pl.ds(base, ch), :] = jnp.where(kpos <= qpos, _sortable_keys(jnp.transpose(score)), INT_MIN)
        return 0

    lax.fori_loop(0, nch, score_body, 0)
    t, need = _kth_threshold(key_sc, nch, ch, ksel)
    tri = _tri_ones(ch)

    def out_body(c, carry):
        base = pl.multiple_of(c * ch, ch)
        sel, carry = _select_chunk(key_sc[pl.ds(base, ch), :], t, need, carry, tri)
        o_ref[c] = jnp.transpose(jnp.where(sel, 0.0, NEG))
        return carry

    lax.fori_loop(0, nch, out_body, jnp.zeros((1, LANES), F32))

    def fill_body(c, _):
        o_ref[c] = jnp.full((tq, ch), NEG, F32)
        return 0

    lax.fori_loop(nch, n_ch_total, fill_body, 0)


def _dsa_index_prompt(idx, kd_t, *, ch):
    b, s, w = idx.shape
    tq = LANES
    assert s % ch == 0 and ch % tq == 0
    ksel = min(IDX_TOPK_MAX, s // 4)
    return pl.pallas_call(
        functools.partial(_dsa_index_prompt_kernel, tq=tq, ch=ch, ksel=ksel),
        out_shape=jax.ShapeDtypeStruct((b, s // ch, s, ch), F32),
        grid=(b, s // tq),
        in_specs=[pl.BlockSpec((None, tq, w), lambda bi, qi: (bi, qi, 0)),
                  pl.BlockSpec((None, LANES, s), lambda bi, qi: (bi, 0, 0))],
        out_specs=pl.BlockSpec((None, s // ch, tq, ch), lambda bi, qi: (bi, 0, qi, 0)),
        scratch_shapes=[pltpu.VMEM((s, LANES), jnp.int32)],
        compiler_params=_cparams(("parallel", "arbitrary")),
        name="dsa_index_prompt",
    )(idx, kd_t)


def _dsa_score_sample_kernel(pt_ref, q_ref, w_ref, knew_ref, *refs, n_pages):
    del pt_ref
    page_refs = refs[:n_pages]
    o_ref = refs[n_pages]
    q = q_ref[...]
    w = w_ref[...] * (IDX_HEADS ** -0.5 * IDX_DIM ** -0.5)
    for p in range(n_pages):
        s = _bdot(q, page_refs[p][...])
        o_ref[p:p + 1, :] = jnp.sum(w * jnp.maximum(s, 0.0), axis=0, keepdims=True)
    s_new = jnp.sum(q * knew_ref[...], axis=-1, keepdims=True)
    sc_new = jnp.sum(w * jnp.maximum(s_new, 0.0), axis=0, keepdims=True)
    lane = lax.broadcasted_iota(jnp.int32, (1, PAGE_SIZE), 1)
    o_ref[n_pages:n_pages + 1, :] = jnp.where(lane == 0, sc_new, NEG)


def _dsa_score_sample(qi, wi, ki_new, cache_kidx, page_ids):
    bd = qi.shape[0]
    n_pages = page_ids.shape[1]
    assert cache_kidx.shape[1:] == (IDX_DIM, PAGE_SIZE)

    def page_spec(p):
        return pl.BlockSpec((None, IDX_DIM, PAGE_SIZE), lambda b, pt: (pt[b * n_pages + p], 0, 0))

    return pl.pallas_call(
        functools.partial(_dsa_score_sample_kernel, n_pages=n_pages),
        out_shape=jax.ShapeDtypeStruct((bd, n_pages + 1, PAGE_SIZE), F32),
        grid_spec=pltpu.PrefetchScalarGridSpec(
            num_scalar_prefetch=1,
            grid=(bd,),
            in_specs=[pl.BlockSpec((None, IDX_HEADS, IDX_DIM), lambda b, pt: (b, 0, 0)),
                      pl.BlockSpec((None, IDX_HEADS, 1), lambda b, pt: (b, 0, 0)),
                      pl.BlockSpec((None, 1, IDX_DIM), lambda b, pt: (b, 0, 0))]
                     + [page_spec(p) for p in range(n_pages)],
            out_specs=pl.BlockSpec((None, n_pages + 1, PAGE_SIZE), lambda b, pt: (b, 0, 0))),
        compiler_params=_cparams(("parallel",)),
        name="dsa_score_sample",
    )(page_ids.reshape(-1), qi, wi, ki_new, *([cache_kidx] * n_pages))


def _select_bias_kernel(s_ref, o_ref, key_sc, *, n_keys, ch, ksel):
    n_rows = s_ref.shape[0]
    nch = n_rows // ch
    kpos = lax.broadcasted_iota(jnp.int32, (n_rows, LANES), 0)
    key_sc[...] = jnp.where(kpos < n_keys, _sortable_keys(s_ref[...]), INT_MIN)
    t, need = _kth_threshold(key_sc, nch, ch, ksel)
    tri = _tri_ones(ch)
    carry = jnp.zeros((1, LANES), F32)
    for c in range(nch):
        sel, carry = _select_chunk(key_sc[c * ch:(c + 1) * ch, :], t, need, carry, tri)
        o_ref[c * ch:(c + 1) * ch, :] = jnp.where(sel, 0.0, NEG)


def _select_bias(score_t, n_keys, ksel):
    n_rows, n_q = score_t.shape
    assert n_q == LANES and n_rows % LANES == 0
    return pl.pallas_call(
        functools.partial(_select_bias_kernel, n_keys=n_keys, ch=LANES, ksel=ksel),
        out_shape=jax.ShapeDtypeStruct((n_rows, LANES), F32),
        scratch_shapes=[pltpu.VMEM((n_rows, LANES), jnp.int32)],
        compiler_params=pltpu.CompilerParams(vmem_limit_bytes=VMEM_LIMIT),
        name="dsa_select_sample",
    )(score_t)


def _log_sigmoid(x):
    return -(jnp.maximum(-x, 0.0) + jnp.log1p(jnp.exp(-jnp.abs(x))))


def _gla_gate(glow, wgu, gate_b):
    return _log_sigmoid(_bdot(glow, wgu) + gate_b) / GLA_TAU


def _column(row_vec, eye):
    n = eye.shape[0]
    return jnp.sum(jnp.where(eye, jnp.broadcast_to(row_vec, (n, n)), 0.0), axis=-1, keepdims=True)


def _level_reference(b_ref, half, c):
    sub = lax.broadcasted_iota(jnp.int32, (SUBLANES, LANES), 0)
    slabs = []
    for t0 in range(0, c, SUBLANES):
        cur = None
        for t in range(t0, t0 + SUBLANES, min(SUBLANES, 2 * half)):
            r = (t // (2 * half)) * (2 * half) + half - 1
            bc = jnp.broadcast_to(b_ref[r:r + 1, :], (SUBLANES, LANES))
            cur = bc if cur is None else jnp.where(sub >= (t - t0), bc, cur)
        slabs.append(cur)
    return jnp.concatenate(slabs, axis=0)


def _gla_prompt_kernel(q_ref, k_ref, v_ref, r_ref, gl_ref, wgu_ref, gb_ref, ng_ref, o_ref, st_ref,
                       s_sc, b_sc, *, c):
    ci = pl.program_id(2)

    @pl.when(ci == 0)
    def _():
        s_sc[...] = jnp.zeros_like(s_sc)

    g = _gla_gate(gl_ref[...], wgu_ref[...], gb_ref[...])
    row = lax.broadcasted_iota(jnp.int32, (c, c), 0)
    col = lax.broadcasted_iota(jnp.int32, (c, c), 1)
    tri = jnp.where(col <= row, 1.0, 0.0).astype(BF16)
    g1 = g.astype(BF16)
    rem = g - g1.astype(F32)
    g2 = rem.astype(BF16)
    g3 = (rem - g2.astype(F32)).astype(BF16)
    bcum = (jnp.dot(tri, g1, preferred_element_type=F32) + jnp.dot(tri, g2, preferred_element_type=F32)
            + jnp.dot(tri, g3, preferred_element_type=F32))
    b_sc[...] = bcum
    q = q_ref[...] * (GLA_HEAD_K ** -0.5)
    k = k_ref[...]
    v = v_ref[...]
    att = jnp.where(row == col, jnp.sum(q * k, axis=-1, keepdims=True), 0.0)
    half = c // 2
    while half >= 1:
        ref = _level_reference(b_sc, half, c)
        qt = q * jnp.exp(jnp.minimum(bcum - ref, 0.0))
        kt = k * jnp.exp(jnp.minimum(ref - bcum, 0.0))
        valid = jnp.logical_and(row // (2 * half) == col // (2 * half),
                                jnp.logical_and(row % (2 * half) >= half, col % (2 * half) < half))
        att = jnp.where(valid, _bdot_nt(qt, kt), att)
        half //= 2
    s0 = s_sc[...]
    o = _bdot(q * jnp.exp(bcum), s0) + _bdot(att, v)
    o = _rms_rows(o, ng_ref[...])
    rr = r_ref[...]
    o_ref[...] = o * (rr * _sigmoid(rr))
    b_last = b_sc[c - 1:c, :]
    khat = k * jnp.exp(b_last - bcum)
    eye = row == col
    s_new = _column(jnp.exp(b_last), eye) * s0 + _bdot_tn(khat, v)
    s_sc[...] = s_new

    @pl.when(ci == pl.num_programs(2) - 1)
    def _():
        st_ref[...] = s_new


def _gla_prompt(qk, v, r, glow, wgu, gate_b, norm_g, *, c):
    b, s, _ = qk.shape
    assert s % c == 0 and c == GLA_HEAD_K
    hk, hv = GLA_HEAD_K, GLA_HEAD_V
    return pl.pallas_call(
        functools.partial(_gla_prompt_kernel, c=c),
        out_shape=(jax.ShapeDtypeStruct((b, s, GLA_DV), F32),
                   jax.ShapeDtypeStruct((b, GLA_HEADS, hk, hv), F32)),
        grid=(b, GLA_HEADS, s // c),
        in_specs=[pl.BlockSpec((None, c, hk), lambda bi, h, ci: (bi, ci, h)),
                  pl.BlockSpec((None, c, hk), lambda bi, h, ci: (bi, ci, GLA_HEADS + h)),
                  pl.BlockSpec((None, c, hv), lambda bi, h, ci: (bi, ci, h)),
                  pl.BlockSpec((None, c, hv), lambda bi, h, ci: (bi, ci, h)),
                  pl.BlockSpec((None, c, LANES), lambda bi, h, ci: (bi, ci, 0)),
                  pl.BlockSpec((LANES, hk), lambda bi, h, ci: (0, h)),
                  pl.BlockSpec((1, hk), lambda bi, h, ci: (0, h)),
                  pl.BlockSpec((1, hv), lambda bi, h, ci: (0, 0))],
        out_specs=(pl.BlockSpec((None, c, hv), lambda bi, h, ci: (bi, ci, h)),
                   pl.BlockSpec((None, None, hk, hv), lambda bi, h, ci: (bi, h, 0, 0))),
        scratch_shapes=[pltpu.VMEM((hk, hv), F32), pltpu.VMEM((c, hk), F32)],
        compiler_params=_cparams(("parallel", "parallel", "arbitrary")),
        name="gla_prompt",
    )(qk, qk, v, r, glow, wgu, gate_b.reshape(1, GLA_DK), norm_g.reshape(1, hv))


def _gla_sample_kernel(qk_ref, v_ref, r_ref, gl_ref, wgu_ref, gb_ref, ng_ref, st_ref, o_ref, nst_ref):
    hk, hv = GLA_HEAD_K, GLA_HEAD_V
    glow = jnp.broadcast_to(gl_ref[...], (SUBLANES, LANES))
    g = _gla_gate(glow, wgu_ref[...], gb_ref[...])[0:1, :]
    eye = (lax.broadcasted_iota(jnp.int32, (hk, hk), 0) == lax.broadcasted_iota(jnp.int32, (hk, hk), 1))
    for h in range(GLA_HEADS):
        ksl = slice(h * hk, (h + 1) * hk)
        vsl = slice(h * hv, (h + 1) * hv)
        q_col = _column(qk_ref[:, ksl] * (GLA_HEAD_K ** -0.5), eye)
        k_col = _column(qk_ref[:, GLA_DK + h * hk:GLA_DK + (h + 1) * hk], eye)
        a_col = _column(jnp.exp(g[:, ksl]), eye)
        s_new = a_col * st_ref[h] + k_col * v_ref[:, vsl]
        nst_ref[h] = s_new
        o = jnp.sum(q_col * s_new, axis=0, keepdims=True)
        rr = r_ref[:, vsl]
        o_ref[:, vsl] = _rms_rows(o, ng_ref[...]) * (rr * _sigmoid(rr))


def _gla_sample(qk, v, r, glow, wgu, gate_b, norm_g, state, state_row0):
    bd = qk.shape[0]
    hk, hv = GLA_HEAD_K, GLA_HEAD_V
    rowspec = lambda w: pl.BlockSpec((None, 1, w), lambda b: (b, 0, 0))
    fixed2 = lambda b: (0, 0)
    o, nst = pl.pallas_call(
        _gla_sample_kernel,
        out_shape=(jax.ShapeDtypeStruct((bd, 1, GLA_DV), F32),
                   jax.ShapeDtypeStruct((bd, GLA_HEADS, hk, hv), F32)),
        grid=(bd,),
        in_specs=[rowspec(2 * GLA_DK), rowspec(GLA_DV), rowspec(GLA_DV), rowspec(LANES),
                  pl.BlockSpec((LANES, GLA_DK), fixed2), pl.BlockSpec((1, GLA_DK), fixed2),
                  pl.BlockSpec((1, hv), fixed2),
                  pl.BlockSpec((None, GLA_HEADS, hk, hv), lambda b: (state_row0 + b, 0, 0, 0))],
        out_specs=(rowspec(GLA_DV), pl.BlockSpec((None, GLA_HEADS, hk, hv), lambda b: (b, 0, 0, 0))),
        compiler_params=_cparams(("parallel",)),
        name="gla_sample",
    )(qk.reshape(bd, 1, -1), v.reshape(bd, 1, -1), r.reshape(bd, 1, -1), glow.reshape(bd, 1, -1),
      wgu, gate_b.reshape(1, GLA_DK), norm_g.reshape(1, hv), state)
    return o.reshape(bd, GLA_DV), nst


def _row_tile(n, cap):
    t = cap
    while n % t:
        t //= 2
    return t


def _project(x, w, layer, col0, width, rope=None, n_rope=None, tn=None):
    tm = _row_tile(x.shape[0], 1024)
    if tn is None:
        tn = 512 if width % 512 == 0 else LANES
    return _linear(x, w, layer, col0, width, tm=tm, tn=tn, rope=rope, n_rope=n_rope)


def _pages_per_step(n_pages):
    return 4 if n_pages % 4 == 0 else 1


def _project_t(x, w_t, nb, rope=None):
    s = x.shape[0] // nb
    return _linear_t(x, w_t, nb, tc=min(512, w_t.shape[0]), tt=_row_tile(s, 512), rope=rope)


def _heads_last(x_t, n_heads):
    b, c, s = x_t.shape
    return x_t.reshape(b, n_heads, c // n_heads, s).transpose(0, 3, 1, 2)


def _diff_mixer(xp, xs, dims, cache_k, cache_v, page_ids, w_in, j, lam_params, subln_g, lam_init, ropes):
    b, s, bd = dims
    d = D_MODEL
    rope_p, rope_s, rope_pt, rope_st = ropes
    w_k_t = w_in[j][:, d:2 * d].T
    q_p = _project(xp, w_in, j, 0, d, rope_p)
    k_pt = _project_t(xp, w_k_t, b, rope_pt)
    v_p = _linear_heads(xp, w_in, j, 2 * d, DIFF_HEADS, tm=_row_tile(b * s, 512))
    o_p = _flash_attention(q_p.reshape(b, s, d), k_pt, v_p.reshape(b, s, DIFF_HEADS, LANES), mode="diff",
                           t=_row_tile(s, 512), lam_params=lam_params, subln_g=subln_g, lam_init=lam_init)
    q_s = _project(xs, w_in, j, 0, d, rope_s)
    k_s = _project(xs, w_in, j, d, d, rope_s)
    v_s = _project(xs, w_in, j, 2 * d, d)
    k_st = _project_t(xs, w_k_t, 1, rope_st)
    v_sh = _linear_heads(xs, w_in, j, 2 * d, DIFF_HEADS, tm=_row_tile(bd, 512))
    o_s = _decode_attention(q_s, cache_k, cache_v, page_ids, k_s, v_s, mode="diff",
                            n_pp=_pages_per_step(page_ids.shape[1]),
                            lam_params=lam_params, subln_g=subln_g, lam_init=lam_init)
    k_leaf_p = _heads_last(k_pt, 2 * DIFF_HEADS)
    v_leaf_p = v_p.reshape(b, s, DIFF_HEADS, LANES)
    k_leaf_s = _heads_last(k_st, 2 * DIFF_HEADS).reshape(bd, 1, 2 * DIFF_HEADS, DIFF_HEAD_DIM)
    v_leaf_s = v_sh.reshape(bd, 1, DIFF_HEADS, LANES)
    return o_p.reshape(b * s, d), o_s, k_leaf_p, v_leaf_p, k_leaf_s, v_leaf_s


def _dsa_mixer(xp, xs, dims, cache_k, cache_v, cache_kidx, page_ids, w_in, j, ropes, past_len):
    b, s, bd = dims
    d = D_MODEL
    rope_p, rope_s, rope_pt, rope_st = ropes
    w = w_in[j]
    c_qi, c_ki, c_wi = 3 * d, 3 * d + IDX_HEADS * IDX_DIM, 3 * d + IDX_HEADS * IDX_DIM + IDX_DIM
    w_idx = jnp.concatenate([w[:, c_qi:c_ki], w[:, c_ki:c_wi], w[:, c_ki:c_wi], w[:, c_wi:c_wi + IDX_HEADS],
                             jnp.zeros((d, LANES - IDX_HEADS), F32)], axis=1)[None]
    idx_w = w_idx.shape[2]
    w_k_t = w[:, d:2 * d].T
    w_v_t = w[:, 2 * d:3 * d].T
    w_ki2_t = jnp.concatenate([w[:, c_ki:c_wi], w[:, c_ki:c_wi]], axis=1).T

    q_p = _project(xp, w_in, j, 0, d, rope_p)
    k_pt = _project_t(xp, w_k_t, b, rope_pt)
    v_pt = _project_t(xp, w_v_t, b)
    kd_pt = _project_t(xp, w_ki2_t, b, rope_pt)
    idx_p = _project(xp, w_idx, 0, 0, idx_w, rope_p, n_rope=5, tn=LANES)
    t = _row_tile(s, 512)
    bias_p = _dsa_index_prompt(idx_p.reshape(b, s, idx_w), kd_pt, ch=t)
    o_p = _flash_attention(q_p.reshape(b, s, d), k_pt, v_pt, mode="dsa", t=t, bias=bias_p)

    q_s = _project(xs, w_in, j, 0, d, rope_s)
    k_s = _project(xs, w_in, j, d, d, rope_s)
    v_s = _project(xs, w_in, j, 2 * d, d)
    k_st = _project_t(xs, w_k_t, 1, rope_st)
    v_st = _project_t(xs, w_v_t, 1)
    kd_st = _project_t(xs, w_ki2_t, 1, rope_st)
    idx_s = _project(xs, w_idx, 0, 0, idx_w, rope_s, n_rope=5, tn=LANES)
    qi_s = idx_s[:, :IDX_HEADS * IDX_DIM].reshape(bd, IDX_HEADS, IDX_DIM)
    ki_s = idx_s[:, IDX_HEADS * IDX_DIM:IDX_HEADS * IDX_DIM + IDX_DIM]
    wi_s = idx_s[:, 5 * LANES:5 * LANES + IDX_HEADS].reshape(bd, IDX_HEADS, 1)
    n_pages = page_ids.shape[1]
    scores = _dsa_score_sample(qi_s, wi_s, ki_s.reshape(bd, 1, IDX_DIM), cache_kidx, page_ids)
    n_keys = past_len + 1
    bias_t = _select_bias(scores.reshape(bd, (n_pages + 1) * PAGE_SIZE).T, n_keys, min(IDX_TOPK_MAX, n_keys // 4))
    bias_s = bias_t.T.reshape(bd, n_pages + 1, PAGE_SIZE)
    o_s = _decode_attention(q_s, cache_k, cache_v, page_ids, k_s, v_s, mode="dsa",
                            n_pp=_pages_per_step(n_pages), bias=bias_s)
    leaves_p = (_heads_last(k_pt, DSA_HEADS), _heads_last(v_pt, DSA_HEADS), kd_pt[:, :IDX_DIM, :].transpose(0, 2, 1))
    leaves_s = (_heads_last(k_st, DSA_HEADS).reshape(bd, 1, DSA_HEADS, DSA_HEAD_DIM),
                _heads_last(v_st, DSA_HEADS).reshape(bd, 1, DSA_HEADS, DSA_HEAD_DIM),
                kd_st[0, :IDX_DIM, :].T.reshape(bd, 1, IDX_DIM))
    return (o_p.reshape(b * s, d), o_s) + leaves_p + leaves_s


def _gla_mixer(xp, xs, dims, state, j, w_in, w_gate_up, gate_b, norm_g):
    b, s, bd = dims
    d = D_MODEL
    c_g = 2 * GLA_DK + 2 * GLA_DV
    w_g = jnp.pad(w_in[j][:, c_g:c_g + GLA_GATE_RANK], ((0, 0), (0, LANES - GLA_GATE_RANK)))[None]
    wgu = jnp.pad(w_gate_up[j], ((0, LANES - GLA_GATE_RANK), (0, 0)))
    outs = []
    for x in (xp, xs):
        qk = _project(x, w_in, j, 0, 2 * GLA_DK)
        v = _project(x, w_in, j, 2 * GLA_DK, GLA_DV)
        r = _project(x, w_in, j, 2 * GLA_DK + GLA_DV, GLA_DV)
        glow = _project(x, w_g, 0, 0, LANES, tn=LANES)
        outs.append((qk, v, r, glow))
    qk, v, r, glow = outs[0]
    o_p, st_p = _gla_prompt(qk.reshape(b, s, -1), v.reshape(b, s, -1), r.reshape(b, s, -1), glow.reshape(b, s, -1),
                            wgu, gate_b[j], norm_g[j], c=GLA_HEAD_K)
    qk, v, r, glow = outs[1]
    n_state = state.shape[1]
    o_s, st_s = _gla_sample(qk, v, r, glow, wgu, gate_b[j], norm_g[j],
                            state.reshape((-1,) + state.shape[2:]), j * n_state)
    return o_p.reshape(b * s, d), o_s, st_p, st_s


def kernel(x_prompt, x_sample, cache_diff_k, cache_diff_v, cache_dsa_k, cache_dsa_v, cache_dsa_kidx, state_gla,
           page_table, ln_mix_g, ln_mix_b, ln_ffn_g, ln_ffn_b, ffn_w_gate_up, ffn_w_down, diff_w_in, diff_lambda,
           diff_subln_g, diff_w_out, dsa_w_in, dsa_w_out, gla_w_in, gla_w_gate_up, gla_gate_b, gla_norm_g,
           gla_w_out):
    b, s, d = x_prompt.shape
    bd, s_d, _ = x_sample.shape
    assert s_d == 1 and d == D_MODEL
    dims = (b, s, bd)
    n_pool, page = cache_diff_k.shape[1], cache_diff_k.shape[2]
    past_len = page_table.shape[1] * page
    xp = x_prompt.reshape(b * s, d)
    xs = x_sample.reshape(bd, d)
    pos_p = jnp.arange(s, dtype=jnp.int32)
    pos_s = jnp.full((bd,), past_len, dtype=jnp.int32)
    ropes = (_rope_tables(pos_p), _rope_tables(pos_s), _rope_tables_t(pos_p), _rope_tables_t(pos_s))

    def feature_major(c):
        perm = (0, 1) + tuple(range(3, c.ndim)) + (2,)
        return jnp.transpose(c, perm).reshape(c.shape[0] * c.shape[1], -1, page)

    cdk, csk, csv, csi = (feature_major(c) for c in (cache_diff_k, cache_dsa_k, cache_dsa_v, cache_dsa_kidx))
    cdv = cache_diff_v.reshape(cache_diff_v.shape[0] * n_pool, page * DIFF_HEADS, LANES)
    tm_p = _row_tile(b * s, 512)
    tm_s = _row_tile(bd, 512)
    tf = 256
    dk_p, dv_p, dk_s, dv_s = [], [], [], []
    sk_p, sv_p, si_p, sk_s, sv_s, si_s = [], [], [], [], [], []
    gs_p, gs_s = [], []
    for i in range(DEPTH):
        kind, j = i % 3, i // 3
        page_ids = page_table + j * n_pool
        if kind == 0:
            lam_init = 0.8 - 0.6 * math.exp(-0.3 * i)
            o_p, o_s, k_p, v_p, k_s, v_s = _diff_mixer(xp, xs, dims, cdk, cdv, page_ids, diff_w_in, j,
                                                       diff_lambda[j], diff_subln_g[j], lam_init, ropes)
            dk_p.append(k_p)
            dv_p.append(v_p)
            dk_s.append(k_s)
            dv_s.append(v_s)
            w_out = diff_w_out
        elif kind == 1:
            o_p, o_s, k_p, v_p, ki_p, k_s, v_s, ki_s = _dsa_mixer(xp, xs, dims, csk, csv, csi, page_ids, dsa_w_in, j,
                                                                  ropes, past_len)
            sk_p.append(k_p)
            sv_p.append(v_p)
            si_p.append(ki_p)
            sk_s.append(k_s)
            sv_s.append(v_s)
            si_s.append(ki_s)
            w_out = dsa_w_out
        else:
            o_p, o_s, st_p, st_s = _gla_mixer(xp, xs, dims, state_gla, j, gla_w_in, gla_w_gate_up, gla_gate_b,
                                              gla_norm_g)
            gs_p.append(st_p)
            gs_s.append(st_s)
            w_out = gla_w_out
        xp = _linear_res_ln(o_p, w_out, j, xp, ln_mix_g, ln_mix_b, i, tm=tm_p)
        xs = _linear_res_ln(o_s, w_out, j, xs, ln_mix_g, ln_mix_b, i, tm=tm_s)
        xp = _ffn_ln(xp, ffn_w_gate_up, ffn_w_down, ln_ffn_g, ln_ffn_b, i, tm=_row_tile(b * s, 1024), tf=tf)
        xs = _ffn_ln(xs, ffn_w_gate_up, ffn_w_down, ln_ffn_g, ln_ffn_b, i, tm=tm_s, tf=tf)
    return (xp.reshape(b, s, d), xs.reshape(bd, 1, d),
            jnp.stack(dk_p), jnp.stack(dv_p), jnp.stack(dk_s), jnp.stack(dv_s),
            jnp.stack(sk_p), jnp.stack(sv_p), jnp.stack(si_p),
            jnp.stack(sk_s), jnp.stack(sv_s), jnp.stack(si_s),
            jnp.stack(gs_p), jnp.stack(gs_s))
```

```python
import functools
import math

import jax
import jax.numpy as jnp
from jax import lax
from jax.experimental import pallas as pl
from jax.experimental.pallas import tpu as pltpu

F32 = jnp.float32
BF16 = jnp.bfloat16

D_MODEL = 1024
DEPTH = 4
PAGE_SIZE = 128
DIFF_HEADS = 8
DIFF_HEAD_DIM = 64
DSA_HEADS = 16
DSA_HEAD_DIM = 64
IDX_HEADS = 8
IDX_DIM = 64
IDX_TOPK_MAX = 256
GLA_HEADS = 4
GLA_DK = D_MODEL // 2
GLA_DV = D_MODEL
GLA_HEAD_K = GLA_DK // GLA_HEADS
GLA_HEAD_V = GLA_DV // GLA_HEADS
GLA_GATE_RANK = 16
GLA_TAU = 16.0
D_FF = (((8 * D_MODEL + 2) // 3 + 255) // 256) * 256
ROPE_THETA = 10000.0
LN_EPS = 1e-5
DEEPNORM_ALPHA = (2.0 * DEPTH) ** 0.25

LANES = 128
SUBLANES = 8
VMEM_LIMIT = 56 * 1024 * 1024
NEG = -1e30
INT_MIN = -2 ** 31

N_GROUPS = D_MODEL // LANES


def _cparams(sem):
    return pltpu.CompilerParams(dimension_semantics=sem, vmem_limit_bytes=VMEM_LIMIT)


def _bdot(a, b):
    return jnp.dot(a.astype(BF16), b.astype(BF16), preferred_element_type=F32)


def _bdot_nt(a, b):
    return lax.dot_general(a.astype(BF16), b.astype(BF16), (((1,), (1,)), ((), ())),
                           preferred_element_type=F32)


def _bdot_tn(a, b):
    return lax.dot_general(a.astype(BF16), b.astype(BF16), (((0,), (0,)), ((), ())),
                           preferred_element_type=F32)


def _layer_norm_rows(z, g, b):
    mu = jnp.mean(z, axis=-1, keepdims=True)
    zc = z - mu
    var = jnp.mean(zc * zc, axis=-1, keepdims=True)
    return zc * lax.rsqrt(var + LN_EPS) * g + b


def _sigmoid(x):
    return 1.0 / (1.0 + jnp.exp(-x))


def _rope_tables(pos):
    d = DIFF_HEAD_DIM
    inv = ROPE_THETA ** (-jnp.arange(0, d, 2, dtype=F32) / d)
    ang = pos.astype(F32)[:, None] * inv[None, :]
    cos = jnp.cos(ang)
    sin = jnp.sin(ang)
    return (jnp.concatenate([cos, cos, cos, cos], axis=-1),
            jnp.concatenate([-sin, sin, -sin, sin], axis=-1))


def _rope_apply(y, cos, sin, first_half):
    partner = jnp.where(first_half, pltpu.roll(y, 96, 1), pltpu.roll(y, 32, 1))
    return y * cos + partner * sin


def _linear_kernel(x_ref, w_ref, *rest, n_rope):
    if n_rope:
        cos_ref, sin_ref, o_ref = rest
    else:
        (o_ref,) = rest
    acc = _bdot(x_ref[...], w_ref[...])
    tm, tn = acc.shape
    if not n_rope:
        o_ref[...] = acc
        return
    j = pl.program_id(1)

    @pl.when(j < n_rope)
    def _():
        cos = cos_ref[...]
        sin = sin_ref[...]
        lane = lax.broadcasted_iota(jnp.int32, (tm, LANES), 1)
        first_half = (lane % DIFF_HEAD_DIM) < (DIFF_HEAD_DIM // 2)
        for c in range(tn // LANES):
            sl = slice(c * LANES, (c + 1) * LANES)
            o_ref[:, sl] = _rope_apply(acc[:, sl], cos, sin, first_half)

    @pl.when(j >= n_rope)
    def _():
        o_ref[...] = acc


def _linear(x, w, layer, col0, width, *, tm, tn, rope=None, n_rope=None):
    n, k = x.shape
    assert n % tm == 0 and width % tn == 0 and col0 % tn == 0
    nj = width // tn
    if rope is None:
        n_rope = 0
    elif n_rope is None:
        n_rope = nj
    in_specs = [pl.BlockSpec((tm, k), lambda i, j: (i, 0)),
                pl.BlockSpec((None, k, tn), lambda i, j: (layer, 0, col0 // tn + j))]
    args = [x, w]
    if n_rope:
        p_blocks = rope[0].shape[0] // tm
        assert rope[0].shape[0] % tm == 0
        spec = pl.BlockSpec((tm, LANES), lambda i, j: (i % p_blocks, 0))
        in_specs += [spec, spec]
        args += [rope[0], rope[1]]
    return pl.pallas_call(
        functools.partial(_linear_kernel, n_rope=n_rope),
        out_shape=jax.ShapeDtypeStruct((n, width), F32),
        grid=(n // tm, nj),
        in_specs=in_specs,
        out_specs=pl.BlockSpec((tm, tn), lambda i, j: (i, j)),
        compiler_params=_cparams(("parallel", "arbitrary")),
        name="linear",
    )(*args)


def _rope_tables_t(pos):
    cos, sin = _rope_tables(pos)
    return cos.T, sin.T


def _linear_t_kernel(w_ref, x_ref, *rest, rope, with_bf16):
    n_out = 2 if with_bf16 else 1
    o_refs = rest[-n_out:]
    acc = _bdot_nt(w_ref[...], x_ref[...])
    tc, tt = acc.shape

    def emit(sl, y):
        o_refs[0][sl, :] = y
        if with_bf16:
            o_refs[1][sl, :] = y.astype(BF16)

    if not rope:
        emit(slice(None), acc)
        return
    cos = rest[0][...]
    sin = rest[1][...]
    row = lax.broadcasted_iota(jnp.int32, (LANES, tt), 0)
    first_half = (row % DIFF_HEAD_DIM) < (DIFF_HEAD_DIM // 2)
    for c in range(tc // LANES):
        sl = slice(c * LANES, (c + 1) * LANES)
        y = acc[sl, :]
        partner = jnp.where(first_half, pltpu.roll(y, LANES - DIFF_HEAD_DIM // 2, 0),
                            pltpu.roll(y, DIFF_HEAD_DIM // 2, 0))
        emit(sl, y * cos + partner * sin)


def _linear_t(x, w_t, nb, *, tc, tt, rope=None, with_bf16=False):
    n, k = x.shape
    c = w_t.shape[0]
    s = n // nb
    assert n == nb * s and s % tt == 0 and c % tc == 0 and tc % LANES == 0
    nt = s // tt
    in_specs = [pl.BlockSpec((tc, k), lambda b, ti, j: (j, 0)),
                pl.BlockSpec((tt, k), lambda b, ti, j: (b * nt + ti, 0))]
    args = [w_t, x]
    if rope is not None:
        p_blocks = rope[0].shape[1] // tt
        spec = pl.BlockSpec((LANES, tt), lambda b, ti, j: (0, ti % p_blocks))
        in_specs += [spec, spec]
        args += [rope[0], rope[1]]
    out_spec = pl.BlockSpec((None, tc, tt), lambda b, ti, j: (b, j, ti))
    out_shape = [jax.ShapeDtypeStruct((nb, c, s), F32)]
    if with_bf16:
        out_shape.append(jax.ShapeDtypeStruct((nb, c, s), BF16))
    outs = pl.pallas_call(
        functools.partial(_linear_t_kernel, rope=rope is not None, with_bf16=with_bf16),
        out_shape=out_shape,
        grid=(nb, nt, c // tc),
        in_specs=in_specs,
        out_specs=[out_spec] * len(out_shape),
        compiler_params=_cparams(("parallel", "parallel", "arbitrary")),
        name="linear_t",
    )(*args)
    return tuple(outs) if with_bf16 else outs[0]


def _linear_heads_kernel(x_ref, w_ref, *o_refs, leaf, groups):
    xb = x_ref[...].astype(BF16)
    o_refs = list(o_refs)
    leaf_ref = o_refs.pop(0) if leaf else None
    grp_ref = o_refs.pop(0) if groups else None
    for h in range(w_ref.shape[1] // LANES):
        y = jnp.dot(xb, w_ref[:, h * LANES:(h + 1) * LANES].astype(BF16), preferred_element_type=F32)
        if leaf:
            leaf_ref[:, h, :] = y
        if groups:
            grp_ref[h] = y.astype(BF16)


def _linear_heads(x, w, layer, col0, n_heads, *, tm, leaf=True, groups=False):
    n, k = x.shape
    width = n_heads * LANES
    assert n % tm == 0 and col0 % width == 0 and (leaf or groups)
    out_shape, out_specs = [], []
    if leaf:
        out_shape.append(jax.ShapeDtypeStruct((n, n_heads, LANES), F32))
        out_specs.append(pl.BlockSpec((tm, n_heads, LANES), lambda i: (i, 0, 0)))
    if groups:
        out_shape.append(jax.ShapeDtypeStruct((n_heads, n, LANES), BF16))
        out_specs.append(pl.BlockSpec((n_heads, tm, LANES), lambda i: (0, i, 0)))
    outs = pl.pallas_call(
        functools.partial(_linear_heads_kernel, leaf=leaf, groups=groups),
        out_shape=out_shape,
        grid=(n // tm,),
        in_specs=[pl.BlockSpec((tm, k), lambda i: (i, 0)),
                  pl.BlockSpec((None, k, width), lambda i: (layer, 0, col0 // width))],
        out_specs=out_specs,
        compiler_params=_cparams(("parallel",)),
        name="linear_heads",
    )(x, w)
    return outs[0] if len(outs) == 1 else tuple(outs)


LOG2E = 1.4426950408889634


def _q_pairs_kernel(x_ref, w_ref, cos_ref, sin_ref, o_ref, *, scale):
    acc = _bdot(x_ref[...], w_ref[...])
    tm = acc.shape[0]
    cos = cos_ref[...]
    sin = sin_ref[...]
    lane = lax.broadcasted_iota(jnp.int32, (tm, LANES), 1)
    first_half = (lane % DIFF_HEAD_DIM) < (DIFF_HEAD_DIM // 2)
    lo = lane < DIFF_HEAD_DIM
    for g in range(N_GROUPS):
        y = _rope_apply(acc[:, g * LANES:(g + 1) * LANES], cos, sin, first_half) * scale
        o_ref[g] = jnp.where(lo, y, 0.0).astype(BF16)
        o_ref[N_GROUPS + g] = jnp.where(lo, 0.0, y).astype(BF16)


def _q_pairs(x, w, layer, rope, *, tm):
    n, k = x.shape
    d = D_MODEL
    assert n % tm == 0 and rope[0].shape[0] % tm == 0
    p_blocks = rope[0].shape[0] // tm
    rspec = pl.BlockSpec((tm, LANES), lambda i: (i % p_blocks, 0))
    return pl.pallas_call(
        functools.partial(_q_pairs_kernel, scale=DIFF_HEAD_DIM ** -0.5 * LOG2E),
        out_shape=jax.ShapeDtypeStruct((2 * N_GROUPS, n, LANES), BF16),
        grid=(n // tm,),
        in_specs=[pl.BlockSpec((tm, k), lambda i: (i, 0)),
                  pl.BlockSpec((None, k, d), lambda i: (layer, 0, 0)), rspec, rspec],
        out_specs=pl.BlockSpec((2 * N_GROUPS, tm, LANES), lambda i: (0, i, 0)),
        compiler_params=_cparams(("parallel",)),
        name="q_pairs",
    )(x, w, rope[0], rope[1])


def _res_ln_kernel(x_ref, w_ref, r_ref, g_ref, b_ref, o_ref):
    y = _bdot(x_ref[...], w_ref[...])
    z = DEEPNORM_ALPHA * r_ref[...] + y
    o_ref[...] = _layer_norm_rows(z, g_ref[...], b_ref[...])


def _linear_res_ln(x, w, layer, resid, g, b, ln_layer, *, tm):
    n, k = x.shape
    d = w.shape[2]
    assert n % tm == 0
    row = lambda i: (i, 0)
    ln_row = lambda i: (ln_layer, 0, 0)
    return pl.pallas_call(
        _res_ln_kernel,
        out_shape=jax.ShapeDtypeStruct((n, d), F32),
        grid=(n // tm,),
        in_specs=[pl.BlockSpec((tm, k), row), pl.BlockSpec((None, k, d), lambda i: (layer, 0, 0)),
                  pl.BlockSpec((tm, d), row),
                  pl.BlockSpec((None, 1, d), ln_row), pl.BlockSpec((None, 1, d), ln_row)],
        out_specs=pl.BlockSpec((tm, d), row),
        compiler_params=_cparams(("parallel",)),
        name="out_proj_ln",
    )(x, w, resid, g.reshape(-1, 1, d), b.reshape(-1, 1, d))


def _ffn_kernel(x_ref, wg_ref, wu_ref, wd_ref, g_ref, b_ref, o_ref, acc_ref):
    f = pl.program_id(1)

    @pl.when(f == 0)
    def _():
        acc_ref[...] = jnp.zeros_like(acc_ref)

    xb = x_ref[...].astype(BF16)
    gate = jnp.dot(xb, wg_ref[...].astype(BF16), preferred_element_type=F32)
    up = jnp.dot(xb, wu_ref[...].astype(BF16), preferred_element_type=F32)
    h = gate * _sigmoid(gate) * up
    acc_ref[...] += _bdot(h, wd_ref[...])

    @pl.when(f == pl.num_programs(1) - 1)
    def _():
        z = DEEPNORM_ALPHA * x_ref[...] + acc_ref[...]
        o_ref[...] = _layer_norm_rows(z, g_ref[...], b_ref[...])


def _ffn_ln(x, w_gate_up, w_down, g, b, layer, *, tm, tf):
    n, d = x.shape
    dff = w_down.shape[1]
    assert n % tm == 0 and dff % tf == 0
    nf = dff // tf
    ln_row = lambda i, f: (layer, 0, 0)
    return pl.pallas_call(
        _ffn_kernel,
        out_shape=jax.ShapeDtypeStruct((n, d), F32),
        grid=(n // tm, nf),
        in_specs=[pl.BlockSpec((tm, d), lambda i, f: (i, 0)),
                  pl.BlockSpec((None, d, tf), lambda i, f: (layer, 0, f)),
                  pl.BlockSpec((None, d, tf), lambda i, f: (layer, 0, nf + f)),
                  pl.BlockSpec((None, tf, d), lambda i, f: (layer, f, 0)),
                  pl.BlockSpec((None, 1, d), ln_row), pl.BlockSpec((None, 1, d), ln_row)],
        out_specs=pl.BlockSpec((tm, d), lambda i, f: (i, 0)),
        scratch_shapes=[pltpu.VMEM((tm, d), F32)],
        compiler_params=_cparams(("parallel", "arbitrary")),
        name="ffn_ln",
    )(x, w_gate_up, w_gate_up, w_down, g.reshape(-1, 1, d), b.reshape(-1, 1, d))


def _diff_lambda(lam_ref, lam_init):
    l = lam_ref[...]
    a = jnp.sum(l[0:1] * l[1:2], axis=-1, keepdims=True)
    c = jnp.sum(l[2:3] * l[3:4], axis=-1, keepdims=True)
    return jnp.exp(a) - jnp.exp(c) + lam_init


def _rms_rows(o, g):
    return o * lax.rsqrt(jnp.mean(o * o, axis=-1, keepdims=True) + LN_EPS) * g


def _flash_kernel(*refs, mode, t, lam_init):
    if mode == "diff":
        q_ref, k_ref, v_ref, lam_ref, g_ref, o_ref, m_sc, l_sc, acc_sc = refs
    else:
        q_ref, k_ref, v_ref, bias_ref, o_ref, m_sc, l_sc, acc_sc = refs
    qi = pl.program_id(1)
    ki = pl.program_id(2)

    @pl.when(ki == 0)
    def _():
        m_sc[...] = jnp.full_like(m_sc, NEG)
        l_sc[...] = jnp.zeros_like(l_sc)
        acc_sc[...] = jnp.zeros_like(acc_sc)

    def block_update(causal):
        if causal:
            row = lax.broadcasted_iota(jnp.int32, (t, t), 0)
            col = lax.broadcasted_iota(jnp.int32, (t, t), 1)
            allowed = col <= row

        def group(g, carry):
            kb = k_ref[g]
            vb = v_ref[g]
            for j in range(2):
                idx = j * N_GROUPS + g
                s = jnp.dot(q_ref[idx], kb, preferred_element_type=F32)
                if mode == "dsa":
                    s = s + bias_ref[...]
                if causal:
                    s = jnp.where(allowed, s, NEG)
                m_prev = m_sc[idx]
                m_next = jnp.maximum(m_prev, jnp.max(s, axis=-1, keepdims=True))
                p = jnp.exp2(s - jnp.concatenate([m_next] * (t // LANES), axis=-1))
                alpha = jnp.exp2(m_prev - m_next)
                l_sc[idx] = alpha * l_sc[idx] + jnp.sum(p, axis=-1, keepdims=True)
                acc_sc[idx] = alpha * acc_sc[idx] + jnp.dot(p.astype(BF16), vb, preferred_element_type=F32)
                m_sc[idx] = m_next
            return carry

        lax.fori_loop(0, N_GROUPS, group, 0)

    if mode == "diff":
        pl.when(ki < qi)(lambda: block_update(False))
        pl.when(ki == qi)(lambda: block_update(True))
    else:
        pl.when(ki <= qi)(lambda: block_update(False))

    @pl.when(ki == qi)
    def _():
        lane = lax.broadcasted_iota(jnp.int32, (1, LANES), 1)
        lo = lane < DIFF_HEAD_DIM
        if mode == "diff":
            lam = _diff_lambda(lam_ref, lam_init)
            gain = g_ref[...] * (1.0 - lam_init)
        for g in range(N_GROUPS):
            o_lo = acc_sc[g] / l_sc[g]
            o_hi = acc_sc[N_GROUPS + g] / l_sc[N_GROUPS + g]
            if mode == "diff":
                o_ref[:, g * LANES:(g + 1) * LANES] = _rms_rows(o_lo - lam * o_hi, gain)
            else:
                o_ref[:, g * LANES:(g + 1) * LANES] = jnp.where(lo, o_lo, o_hi)


def _flash_attention(q2, k_t, v_g, b, *, mode, t, lam_params=None, subln_g=None, lam_init=0.0, bias=None):
    n = q2.shape[1]
    s = n // b
    d = D_MODEL
    assert s % t == 0 and n == b * s
    nblk = s // t
    k4 = k_t.reshape(b, N_GROUPS, LANES, s)
    qspec = pl.BlockSpec((2 * N_GROUPS, t, LANES), lambda bi, qi, ki: (0, bi * nblk + qi, 0))
    kspec = pl.BlockSpec((None, N_GROUPS, LANES, t), lambda bi, qi, ki: (bi, 0, 0, jnp.minimum(ki, qi)))
    vspec = pl.BlockSpec((N_GROUPS, t, LANES), lambda bi, qi, ki: (0, bi * nblk + jnp.minimum(ki, qi), 0))
    in_specs = [qspec, kspec, vspec]
    args = [q2, k4, v_g]
    if mode == "diff":
        in_specs += [pl.BlockSpec((4, DIFF_HEAD_DIM), lambda bi, qi, ki: (0, 0)),
                     pl.BlockSpec((1, LANES), lambda bi, qi, ki: (0, 0))]
        args += [lam_params, subln_g.reshape(1, LANES)]
    else:
        assert bias.shape == (b, nblk, s, t)
        in_specs += [pl.BlockSpec((None, None, t, t), lambda bi, qi, ki: (bi, jnp.minimum(ki, qi), qi, 0))]
        args += [bias]
    stat = pltpu.VMEM((2 * N_GROUPS, t, LANES), F32)
    return pl.pallas_call(
        functools.partial(_flash_kernel, mode=mode, t=t, lam_init=lam_init),
        out_shape=jax.ShapeDtypeStruct((n, d), F32),
        grid=(b, nblk, nblk),
        in_specs=in_specs,
        out_specs=pl.BlockSpec((t, d), lambda bi, qi, ki: (bi * nblk + qi, 0)),
        scratch_shapes=[stat, stat, stat],
        compiler_params=_cparams(("parallel", "parallel", "arbitrary")),
        name="flash_" + mode,
    )(*args)


def _decode_kernel(pt_ref, q_ref, *refs, mode, n_pp, n_pages, scale, lam_init):
    del pt_ref
    k_refs = refs[:n_pp]
    v_refs = refs[n_pp:2 * n_pp]
    knew_ref, vnew_ref = refs[2 * n_pp:2 * n_pp + 2]
    rest = refs[2 * n_pp + 2:]
    if mode == "diff":
        lam_ref, g_ref, o_ref, qm_sc, m_sc, l_sc, acc_sc, e_sc = rest
    else:
        bias_ref, o_ref, qm_sc, m_sc, l_sc, acc_sc = rest
    step = pl.program_id(1)
    n_sub = 2 * N_GROUPS
    row = lax.broadcasted_iota(jnp.int32, (n_sub, D_MODEL), 0)
    lane = lax.broadcasted_iota(jnp.int32, (n_sub, D_MODEL), 1)

    @pl.when(step == 0)
    def _():
        qm_sc[...] = jnp.where(lane // DIFF_HEAD_DIM == row, q_ref[...] * scale, 0.0)
        m_sc[...] = jnp.full_like(m_sc, NEG)
        l_sc[...] = jnp.zeros_like(l_sc)
        acc_sc[...] = jnp.zeros_like(acc_sc)
        if mode == "diff":
            tok = lax.broadcasted_iota(jnp.int32, (PAGE_SIZE, D_MODEL), 0)
            col = lax.broadcasted_iota(jnp.int32, (PAGE_SIZE, D_MODEL), 1)
            e_sc[...] = jnp.where(col // DIFF_HEADS == tok, 1.0, 0.0).astype(BF16)

    qm = qm_sc[...]
    s_parts = []
    for i in range(n_pp):
        s = _bdot(qm, k_refs[i][...])
        if mode == "dsa":
            s = s + bias_ref[pl.ds(step * n_pp + i, 1), :]
        s_parts.append(s)
    s_all = jnp.concatenate(s_parts, axis=-1)
    m_prev = m_sc[...]
    m_new = jnp.maximum(m_prev, jnp.max(s_all, axis=-1, keepdims=True))
    alpha = jnp.exp(m_prev - m_new)
    p_all = jnp.exp(s_all - m_new)
    l_sc[...] = alpha * l_sc[...] + jnp.sum(p_all, axis=-1, keepdims=True)
    pv = None
    for i in range(n_pp):
        p_i = p_all[:, i * PAGE_SIZE:(i + 1) * PAGE_SIZE]
        if mode == "diff":
            p_rows = jnp.where(lane % DIFF_HEADS == row // 2,
                               jnp.dot(p_i.astype(BF16), e_sc[...], preferred_element_type=F32), 0.0)
            part = _bdot(p_rows, v_refs[i][...])
        else:
            part = _bdot_nt(p_i, v_refs[i][...])
        pv = part if pv is None else pv + part
    acc_sc[...] = alpha * acc_sc[...] + pv
    m_sc[...] = m_new

    @pl.when(step == pl.num_programs(1) - 1)
    def _():
        s_new = jnp.sum(qm * knew_ref[...], axis=-1, keepdims=True)
        if mode == "dsa":
            s_new = s_new + bias_ref[n_pages:n_pages + 1, 0:1]
        m_prev = m_sc[...]
        m_fin = jnp.maximum(m_prev, s_new)
        alpha = jnp.exp(m_prev - m_fin)
        p_new = jnp.exp(s_new - m_fin)
        l_fin = alpha * l_sc[...] + p_new
        if mode == "diff":
            row_h = lax.broadcasted_iota(jnp.int32, (n_sub, LANES), 0) // 2
            v_rows = jnp.zeros((n_sub, LANES), F32)
            for h in range(DIFF_HEADS):
                v_rows = jnp.where(row_h == h, vnew_ref[:, h * LANES:(h + 1) * LANES], v_rows)
            o16 = (alpha * acc_sc[...] + p_new * v_rows) / l_fin
            lam = _diff_lambda(lam_ref, lam_init)
            gain = g_ref[...] * (1.0 - lam_init)
            for h in range(DIFF_HEADS):
                o_h = o16[2 * h:2 * h + 1, :] - lam * o16[2 * h + 1:2 * h + 2, :]
                o_ref[:, h * LANES:(h + 1) * LANES] = _rms_rows(o_h, gain)
        else:
            o16 = (alpha * acc_sc[...] + p_new * vnew_ref[...]) / l_fin
            o_ref[...] = jnp.sum(jnp.where(lane // DSA_HEAD_DIM == row, o16, 0.0), axis=0, keepdims=True)


def _decode_attention(q, cache_k, cache_v, page_ids, k_new, v_new, *, mode, n_pp,
                      lam_params=None, subln_g=None, lam_init=0.0, bias=None):
    bd, d = q.shape
    n_pages = page_ids.shape[1]
    assert n_pages % n_pp == 0 and cache_k.shape[1:] == (d, PAGE_SIZE) and cache_v.shape[1:] == (d, PAGE_SIZE)
    n_steps = n_pages // n_pp
    row_spec = pl.BlockSpec((None, 1, d), lambda b, s, pt: (b, 0, 0))

    def page_spec(i):
        return pl.BlockSpec((None, d, PAGE_SIZE), lambda b, s, pt: (pt[b * n_pages + s * n_pp + i], 0, 0))

    in_specs = [row_spec] + [page_spec(i) for i in range(n_pp)] * 2 + [row_spec, row_spec]
    args = [q.reshape(bd, 1, d)] + [cache_k] * n_pp + [cache_v] * n_pp + [k_new.reshape(bd, 1, d), v_new.reshape(bd, 1, d)]
    n_sub = 2 * N_GROUPS
    scratch = [pltpu.VMEM((n_sub, d), F32), pltpu.VMEM((n_sub, 1), F32), pltpu.VMEM((n_sub, 1), F32)]
    if mode == "diff":
        in_specs += [pl.BlockSpec((4, DIFF_HEAD_DIM), lambda b, s, pt: (0, 0)),
                     pl.BlockSpec((1, LANES), lambda b, s, pt: (0, 0))]
        args += [lam_params, subln_g.reshape(1, LANES)]
        scratch += [pltpu.VMEM((n_sub, LANES), F32), pltpu.VMEM((PAGE_SIZE, d), BF16)]
    else:
        in_specs += [pl.BlockSpec((None, n_pages + 1, PAGE_SIZE), lambda b, s, pt: (b, 0, 0))]
        args += [bias]
        scratch += [pltpu.VMEM((n_sub, d), F32)]
    out = pl.pallas_call(
        functools.partial(_decode_kernel, mode=mode, n_pp=n_pp, n_pages=n_pages,
                          scale=DIFF_HEAD_DIM ** -0.5, lam_init=lam_init),
        out_shape=jax.ShapeDtypeStruct((bd, 1, d), F32),
        grid_spec=pltpu.PrefetchScalarGridSpec(
            num_scalar_prefetch=1,
            grid=(bd, n_steps),
            in_specs=in_specs,
            out_specs=row_spec,
            scratch_shapes=scratch),
        compiler_params=_cparams(("parallel", "arbitrary")),
        name="decode_" + mode,
    )(page_ids.reshape(-1), *args)
    return out.reshape(bd, d)


def _sortable_keys(score):
    bits = lax.bitcast_convert_type(score, jnp.int32)
    key = jnp.where(bits < 0, bits ^ jnp.int32(0x7FFFFFFF), bits)
    return jnp.where(score == 0.0, 0, key)


def _kth_threshold(key_ref, nch, ch, ksel):
    def count_ge(cand):
        def body(c, acc):
            blk = key_ref[pl.ds(pl.multiple_of(c * ch, ch), ch), :]
            hit = jnp.where(blk >= cand, 1, 0).astype(jnp.int32)
            return acc + jnp.sum(hit.reshape(ch // SUBLANES, SUBLANES, LANES), axis=0)

        acc = lax.fori_loop(0, nch, body, jnp.zeros((SUBLANES, LANES), jnp.int32))
        return jnp.sum(acc, axis=0, keepdims=True)

    def bit_body(i, t):
        cand = t ^ lax.shift_left(jnp.int32(1), (31 - i).astype(jnp.int32))
        return jnp.where(count_ge(cand) >= ksel, cand, t)

    t = lax.fori_loop(0, 32, bit_body, jnp.full((1, LANES), INT_MIN, jnp.int32))
    need = ksel - count_ge(t + 1)
    return t, need.astype(F32)


def _tri_ones(n):
    row = lax.broadcasted_iota(jnp.int32, (n, n), 0)
    col = lax.broadcasted_iota(jnp.int32, (n, n), 1)
    return jnp.where(col <= row, 1.0, 0.0).astype(BF16)


def _select_chunk(key, t, need, carry, tri):
    eq = key == t
    pref = jnp.dot(tri, jnp.where(eq, 1.0, 0.0).astype(BF16), preferred_element_type=F32) + carry
    sel = jnp.logical_or(key > t, jnp.logical_and(eq, pref <= need))
    sel = jnp.logical_and(sel, key > INT_MIN)
    return sel, pref[key.shape[0] - 1:, :]


def _dsa_index_prompt_kernel(idx_ref, kd_ref, o_ref, key_sc, *, tq, ch, ksel):
    qi = pl.program_id(1)
    n_ch_total = o_ref.shape[0]
    nch = (qi * tq + tq + ch - 1) // ch
    qblk = idx_ref[...]
    lane = lax.broadcasted_iota(jnp.int32, (1, LANES), 1)
    lo = lane < IDX_DIM
    parts = []
    for h in range(IDX_HEADS):
        grp = qblk[:, (h // 2) * LANES:(h // 2 + 1) * LANES]
        parts.append(jnp.where(lo if h % 2 == 0 else jnp.logical_not(lo), grp, 0.0).astype(BF16))
    qstack = jnp.concatenate(parts, axis=0)
    w = qblk[:, 5 * LANES:6 * LANES] * (IDX_HEADS ** -0.5 * IDX_DIM ** -0.5)
    w_cols = [w[:, h:h + 1] for h in range(IDX_HEADS)]
    qpos = qi * tq + lane

    def score_body(c, _):
        base = pl.multiple_of(c * ch, ch)
        r = _bdot(qstack, kd_ref[:, pl.ds(base, ch)])
        score = jnp.zeros((tq, ch), F32)
        for h in range(IDX_HEADS):
            score = score + w_cols[h] * jnp.maximum(r[h * tq:(h + 1) * tq, :], 0.0)
        kpos = base + lax.broadcasted_iota(jnp.int32, (ch, tq), 0)
        key_sc[pl.ds(base, ch), :] = jnp.where(kpos <= qpos, _sortable_keys(jnp.transpose(score)), INT_MIN)
        return 0

    lax.fori_loop(0, nch, score_body, 0)
    t, need = _kth_threshold(key_sc, nch, ch, ksel)
    tri = _tri_ones(ch)

    def out_body(c, carry):
        base = pl.multiple_of(c * ch, ch)
        sel, carry = _select_chunk(key_sc[pl.ds(base, ch), :], t, need, carry, tri)
        o_ref[c] = jnp.transpose(jnp.where(sel, 0.0, NEG))
        return carry

    lax.fori_loop(0, nch, out_body, jnp.zeros((1, LANES), F32))

    def fill_body(c, _):
        o_ref[c] = jnp.full((tq, ch), NEG, F32)
        return 0

    lax.fori_loop(nch, n_ch_total, fill_body, 0)


def _dsa_index_prompt(idx, kd_t, *, ch):
    b, s, w = idx.shape
    tq = LANES
    assert s % ch == 0 and ch % tq == 0
    ksel = min(IDX_TOPK_MAX, s // 4)
    return pl.pallas_call(
        functools.partial(_dsa_index_prompt_kernel, tq=tq, ch=ch, ksel=ksel),
        out_shape=jax.ShapeDtypeStruct((b, s // ch, s, ch), F32),
        grid=(b, s // tq),
        in_specs=[pl.BlockSpec((None, tq, w), lambda bi, qi: (bi, qi, 0)),
                  pl.BlockSpec((None, LANES, s), lambda bi, qi: (bi, 0, 0))],
        out_specs=pl.BlockSpec((None, s // ch, tq, ch), lambda bi, qi: (bi, 0, qi, 0)),
        scratch_shapes=[pltpu.VMEM((s, LANES), jnp.int32)],
        compiler_params=_cparams(("parallel", "arbitrary")),
        name="dsa_index_prompt",
    )(idx, kd_t)


def _dsa_score_sample_kernel(pt_ref, q_ref, w_ref, knew_ref, *refs, n_pages):
    del pt_ref
    page_refs = refs[:n_pages]
    o_ref = refs[n_pages]
    q = q_ref[...]
    w = w_ref[...] * (IDX_HEADS ** -0.5 * IDX_DIM ** -0.5)
    for p in range(n_pages):
        s = _bdot(q, page_refs[p][...])
        o_ref[p:p + 1, :] = jnp.sum(w * jnp.maximum(s, 0.0), axis=0, keepdims=True)
    s_new = jnp.sum(q * knew_ref[...], axis=-1, keepdims=True)
    sc_new = jnp.sum(w * jnp.maximum(s_new, 0.0), axis=0, keepdims=True)
    lane = lax.broadcasted_iota(jnp.int32, (1, PAGE_SIZE), 1)
    o_ref[n_pages:n_pages + 1, :] = jnp.where(lane == 0, sc_new, NEG)


def _dsa_score_sample(qi, wi, ki_new, cache_kidx, page_ids):
    bd = qi.shape[0]
    n_pages = page_ids.shape[1]
    assert cache_kidx.shape[1:] == (IDX_DIM, PAGE_SIZE)

    def page_spec(p):
        return pl.BlockSpec((None, IDX_DIM, PAGE_SIZE), lambda b, pt: (pt[b * n_pages + p], 0, 0))

    return pl.pallas_call(
        functools.partial(_dsa_score_sample_kernel, n_pages=n_pages),
        out_shape=jax.ShapeDtypeStruct((bd, n_pages + 1, PAGE_SIZE), F32),
        grid_spec=pltpu.PrefetchScalarGridSpec(
            num_scalar_prefetch=1,
            grid=(bd,),
            in_specs=[pl.BlockSpec((None, IDX_HEADS, IDX_DIM), lambda b, pt: (b, 0, 0)),
                      pl.BlockSpec((None, IDX_HEADS, 1), lambda b, pt: (b, 0, 0)),
                      pl.BlockSpec((None, 1, IDX_DIM), lambda b, pt: (b, 0, 0))]
                     + [page_spec(p) for p in range(n_pages)],
            out_specs=pl.BlockSpec((None, n_pages + 1, PAGE_SIZE), lambda b, pt: (b, 0, 0))),
        compiler_params=_cparams(("parallel",)),
        name="dsa_score_sample",
    )(page_ids.reshape(-1), qi, wi, ki_new, *([cache_kidx] * n_pages))


def _select_bias_kernel(s_ref, o_ref, key_sc, *, n_keys, ch, ksel):
    n_rows = s_ref.shape[0]
    nch = n_rows // ch
    kpos = lax.broadcasted_iota(jnp.int32, (n_rows, LANES), 0)
    key_sc[...] = jnp.where(kpos < n_keys, _sortable_keys(s_ref[...]), INT_MIN)
    t, need = _kth_threshold(key_sc, nch, ch, ksel)
    tri = _tri_ones(ch)
    carry = jnp.zeros((1, LANES), F32)
    for c in range(nch):
        sel, carry = _select_chunk(key_sc[c * ch:(c + 1) * ch, :], t, need, carry, tri)
        o_ref[c * ch:(c + 1) * ch, :] = jnp.where(sel, 0.0, NEG)


def _select_bias(score_t, n_keys, ksel):
    n_rows, n_q = score_t.shape
    assert n_q == LANES and n_rows % LANES == 0
    return pl.pallas_call(
        functools.partial(_select_bias_kernel, n_keys=n_keys, ch=LANES, ksel=ksel),
        out_shape=jax.ShapeDtypeStruct((n_rows, LANES), F32),
        scratch_shapes=[pltpu.VMEM((n_rows, LANES), jnp.int32)],
        compiler_params=pltpu.CompilerParams(vmem_limit_bytes=VMEM_LIMIT),
        name="dsa_select_sample",
    )(score_t)


def _log_sigmoid(x):
    return -(jnp.maximum(-x, 0.0) + jnp.log1p(jnp.exp(-jnp.abs(x))))


def _gla_gate(glow, wgu, gate_b):
    return _log_sigmoid(_bdot(glow, wgu) + gate_b) / GLA_TAU


def _column(row_vec, eye):
    n = eye.shape[0]
    return jnp.sum(jnp.where(eye, jnp.broadcast_to(row_vec, (n, n)), 0.0), axis=-1, keepdims=True)


def _level_reference(b_ref, half, c):
    sub = lax.broadcasted_iota(jnp.int32, (SUBLANES, LANES), 0)
    slabs = []
    for t0 in range(0, c, SUBLANES):
        cur = None
        for t in range(t0, t0 + SUBLANES, min(SUBLANES, 2 * half)):
            r = (t // (2 * half)) * (2 * half) + half - 1
            bc = jnp.broadcast_to(b_ref[r:r + 1, :], (SUBLANES, LANES))
            cur = bc if cur is None else jnp.where(sub >= (t - t0), bc, cur)
        slabs.append(cur)
    return jnp.concatenate(slabs, axis=0)


def _gla_prompt_kernel(q_ref, k_ref, v_ref, r_ref, gl_ref, wgu_ref, gb_ref, ng_ref, o_ref, st_ref,
                       s_sc, b_sc, *, c):
    ci = pl.program_id(2)

    @pl.when(ci == 0)
    def _():
        s_sc[...] = jnp.zeros_like(s_sc)

    g = _gla_gate(gl_ref[...], wgu_ref[...], gb_ref[...])
    row = lax.broadcasted_iota(jnp.int32, (c, c), 0)
    col = lax.broadcasted_iota(jnp.int32, (c, c), 1)
    tri = jnp.where(col <= row, 1.0, 0.0).astype(BF16)
    g1 = g.astype(BF16)
    rem = g - g1.astype(F32)
    g2 = rem.astype(BF16)
    g3 = (rem - g2.astype(F32)).astype(BF16)
    bcum = (jnp.dot(tri, g1, preferred_element_type=F32) + jnp.dot(tri, g2, preferred_element_type=F32)
            + jnp.dot(tri, g3, preferred_element_type=F32))
    b_sc[...] = bcum
    q = q_ref[...] * (GLA_HEAD_K ** -0.5)
    k = k_ref[...]
    v = v_ref[...]
    att = jnp.where(row == col, jnp.sum(q * k, axis=-1, keepdims=True), 0.0)
    half = c // 2
    while half >= 1:
        ref = _level_reference(b_sc, half, c)
        qt = q * jnp.exp(jnp.minimum(bcum - ref, 0.0))
        kt = k * jnp.exp(jnp.minimum(ref - bcum, 0.0))
        valid = jnp.logical_and(row // (2 * half) == col // (2 * half),
                                jnp.logical_and(row % (2 * half) >= half, col % (2 * half) < half))
        att = jnp.where(valid, _bdot_nt(qt, kt), att)
        half //= 2
    s0 = s_sc[...]
    o = _bdot(q * jnp.exp(bcum), s0) + _bdot(att, v)
    o = _rms_rows(o, ng_ref[...])
    rr = r_ref[...]
    o_ref[...] = o * (rr * _sigmoid(rr))
    b_last = b_sc[c - 1:c, :]
    khat = k * jnp.exp(b_last - bcum)
    eye = row == col
    s_new = _column(jnp.exp(b_last), eye) * s0 + _bdot_tn(khat, v)
    s_sc[...] = s_new

    @pl.when(ci == pl.num_programs(2) - 1)
    def _():
        st_ref[...] = s_new


def _gla_prompt(qk, v, r, glow, wgu, gate_b, norm_g, *, c):
    b, s, _ = qk.shape
    assert s % c == 0 and c == GLA_HEAD_K
    hk, hv = GLA_HEAD_K, GLA_HEAD_V
    return pl.pallas_call(
        functools.partial(_gla_prompt_kernel, c=c),
        out_shape=(jax.ShapeDtypeStruct((b, s, GLA_DV), F32),
                   jax.ShapeDtypeStruct((b, GLA_HEADS, hk, hv), F32)),
        grid=(b, GLA_HEADS, s // c),
        in_specs=[pl.BlockSpec((None, c, hk), lambda bi, h, ci: (bi, ci, h)),
                  pl.BlockSpec((None, c, hk), lambda bi, h, ci: (bi, ci, GLA_HEADS + h)),
                  pl.BlockSpec((None, c, hv), lambda bi, h, ci: (bi, ci, h)),
                  pl.BlockSpec((None, c, hv), lambda bi, h, ci: (bi, ci, h)),
                  pl.BlockSpec((None, c, LANES), lambda bi, h, ci: (bi, ci, 0)),
                  pl.BlockSpec((LANES, hk), lambda bi, h, ci: (0, h)),
                  pl.BlockSpec((1, hk), lambda bi, h, ci: (0, h)),
                  pl.BlockSpec((1, hv), lambda bi, h, ci: (0, 0))],
        out_specs=(pl.BlockSpec((None, c, hv), lambda bi, h, ci: (bi, ci, h)),
                   pl.BlockSpec((None, None, hk, hv), lambda bi, h, ci: (bi, h, 0, 0))),
        scratch_shapes=[pltpu.VMEM((hk, hv), F32), pltpu.VMEM((c, hk), F32)],
        compiler_params=_cparams(("parallel", "parallel", "arbitrary")),
        name="gla_prompt",
    )(qk, qk, v, r, glow, wgu, gate_b.reshape(1, GLA_DK), norm_g.reshape(1, hv))


def _gla_sample_kernel(qk_ref, v_ref, r_ref, gl_ref, wgu_ref, gb_ref, ng_ref, st_ref, o_ref, nst_ref):
    hk, hv = GLA_HEAD_K, GLA_HEAD_V
    glow = jnp.broadcast_to(gl_ref[...], (SUBLANES, LANES))
    g = _gla_gate(glow, wgu_ref[...], gb_ref[...])[0:1, :]
    eye = (lax.broadcasted_iota(jnp.int32, (hk, hk), 0) == lax.broadcasted_iota(jnp.int32, (hk, hk), 1))
    for h in range(GLA_HEADS):
        ksl = slice(h * hk, (h + 1) * hk)
        vsl = slice(h * hv, (h + 1) * hv)
        q_col = _column(qk_ref[:, ksl] * (GLA_HEAD_K ** -0.5), eye)
        k_col = _column(qk_ref[:, GLA_DK + h * hk:GLA_DK + (h + 1) * hk], eye)
        a_col = _column(jnp.exp(g[:, ksl]), eye)
        s_new = a_col * st_ref[h] + k_col * v_ref[:, vsl]
        nst_ref[h] = s_new
        o = jnp.sum(q_col * s_new, axis=0, keepdims=True)
        rr = r_ref[:, vsl]
        o_ref[:, vsl] = _rms_rows(o, ng_ref[...]) * (rr * _sigmoid(rr))


def _gla_sample(qk, v, r, glow, wgu, gate_b, norm_g, state, state_row0):
    bd = qk.shape[0]
    hk, hv = GLA_HEAD_K, GLA_HEAD_V
    rowspec = lambda w: pl.BlockSpec((None, 1, w), lambda b: (b, 0, 0))
    fixed2 = lambda b: (0, 0)
    o, nst = pl.pallas_call(
        _gla_sample_kernel,
        out_shape=(jax.ShapeDtypeStruct((bd, 1, GLA_DV), F32),
                   jax.ShapeDtypeStruct((bd, GLA_HEADS, hk, hv), F32)),
        grid=(bd,),
        in_specs=[rowspec(2 * GLA_DK), rowspec(GLA_DV), rowspec(GLA_DV), rowspec(LANES),
                  pl.BlockSpec((LANES, GLA_DK), fixed2), pl.BlockSpec((1, GLA_DK), fixed2),
                  pl.BlockSpec((1, hv), fixed2),
                  pl.BlockSpec((None, GLA_HEADS, hk, hv), lambda b: (state_row0 + b, 0, 0, 0))],
        out_specs=(rowspec(GLA_DV), pl.BlockSpec((None, GLA_HEADS, hk, hv), lambda b: (b, 0, 0, 0))),
        compiler_params=_cparams(("parallel",)),
        name="gla_sample",
    )(qk.reshape(bd, 1, -1), v.reshape(bd, 1, -1), r.reshape(bd, 1, -1), glow.reshape(bd, 1, -1),
      wgu, gate_b.reshape(1, GLA_DK), norm_g.reshape(1, hv), state)
    return o.reshape(bd, GLA_DV), nst


def _row_tile(n, cap):
    t = cap
    while n % t:
        t //= 2
    return t


def _project(x, w, layer, col0, width, rope=None, n_rope=None, tn=None):
    tm = _row_tile(x.shape[0], 1024)
    if tn is None:
        tn = 512 if width % 512 == 0 else LANES
    return _linear(x, w, layer, col0, width, tm=tm, tn=tn, rope=rope, n_rope=n_rope)


def _pages_per_step(n_pages):
    for n in (16, 4, 2):
        if n_pages % n == 0:
            return n
    return 1


def _project_t(x, w_t, nb, rope=None, with_bf16=False):
    s = x.shape[0] // nb
    return _linear_t(x, w_t, nb, tc=min(512, w_t.shape[0]), tt=_row_tile(s, 512), rope=rope, with_bf16=with_bf16)


def _heads_last(x_t, n_heads):
    b, c, s = x_t.shape
    return x_t.reshape(b, n_heads, c // n_heads, s).transpose(0, 3, 1, 2)


def _diff_mixer(xp, xs, dims, cache_k, cache_v, page_ids, w_in, j, lam_params, subln_g, lam_init, ropes):
    b, s, bd = dims
    d = D_MODEL
    rope_p, rope_s, rope_pt, rope_st = ropes
    w_k_t = w_in[j][:, d:2 * d].T
    tm = _row_tile(b * s, 512)
    q2_p = _q_pairs(xp, w_in, j, rope_p, tm=_row_tile(s, 512))
    k_pt, k_pt16 = _project_t(xp, w_k_t, b, rope_pt, with_bf16=True)
    v_p, v_pg = _linear_heads(xp, w_in, j, 2 * d, DIFF_HEADS, tm=tm, groups=True)
    o_p = _flash_attention(q2_p, k_pt16, v_pg, b, mode="diff", t=_row_tile(s, 512),
                           lam_params=lam_params, subln_g=subln_g, lam_init=lam_init)
    q_s = _project(xs, w_in, j, 0, d, rope_s)
    k_s = _project(xs, w_in, j, d, d, rope_s)
    v_s = _project(xs, w_in, j, 2 * d, d)
    k_st = _project_t(xs, w_k_t, 1, rope_st)
    v_sh = _linear_heads(xs, w_in, j, 2 * d, DIFF_HEADS, tm=_row_tile(bd, 512))
    o_s = _decode_attention(q_s, cache_k, cache_v, page_ids, k_s, v_s, mode="diff",
                            n_pp=_pages_per_step(page_ids.shape[1]),
                            lam_params=lam_params, subln_g=subln_g, lam_init=lam_init)
    k_leaf_p = _heads_last(k_pt, 2 * DIFF_HEADS)
    v_leaf_p = v_p.reshape(b, s, DIFF_HEADS, LANES)
    k_leaf_s = _heads_last(k_st, 2 * DIFF_HEADS).reshape(bd, 1, 2 * DIFF_HEADS, DIFF_HEAD_DIM)
    v_leaf_s = v_sh.reshape(bd, 1, DIFF_HEADS, LANES)
    return o_p.reshape(b * s, d), o_s, k_leaf_p, v_leaf_p, k_leaf_s, v_leaf_s


def _dsa_mixer(xp, xs, dims, cache_k, cache_v, cache_kidx, page_ids, w_in, j, ropes, past_len):
    b, s, bd = dims
    d = D_MODEL
    rope_p, rope_s, rope_pt, rope_st = ropes
    w = w_in[j]
    c_qi, c_ki, c_wi = 3 * d, 3 * d + IDX_HEADS * IDX_DIM, 3 * d + IDX_HEADS * IDX_DIM + IDX_DIM
    w_idx = jnp.concatenate([w[:, c_qi:c_ki], w[:, c_ki:c_wi], w[:, c_ki:c_wi], w[:, c_wi:c_wi + IDX_HEADS],
                             jnp.zeros((d, LANES - IDX_HEADS), F32)], axis=1)[None]
    idx_w = w_idx.shape[2]
    w_k_t = w[:, d:2 * d].T
    w_v_t = w[:, 2 * d:3 * d].T
    w_ki2_t = jnp.concatenate([w[:, c_ki:c_wi], w[:, c_ki:c_wi]], axis=1).T

    q2_p = _q_pairs(xp, w_in, j, rope_p, tm=_row_tile(s, 512))
    k_pt, k_pt16 = _project_t(xp, w_k_t, b, rope_pt, with_bf16=True)
    v_pt = _project_t(xp, w_v_t, b)
    v_pg = _linear_heads(xp, w_in, j, 2 * d, N_GROUPS, tm=_row_tile(b * s, 512), leaf=False, groups=True)
    kd_pt = _project_t(xp, w_ki2_t, b, rope_pt)
    idx_p = _project(xp, w_idx, 0, 0, idx_w, rope_p, n_rope=5, tn=LANES)
    t = _row_tile(s, 512)
    bias_p = _dsa_index_prompt(idx_p.reshape(b, s, idx_w), kd_pt, ch=t)
    o_p = _flash_attention(q2_p, k_pt16, v_pg, b, mode="dsa", t=t, bias=bias_p)

    q_s = _project(xs, w_in, j, 0, d, rope_s)
    k_s = _project(xs, w_in, j, d, d, rope_s)
    v_s = _project(xs, w_in, j, 2 * d, d)
    k_st = _project_t(xs, w_k_t, 1, rope_st)
    v_st = _project_t(xs, w_v_t, 1)
    kd_st = _project_t(xs, w_ki2_t, 1, rope_st)
    idx_s = _project(xs, w_idx, 0, 0, idx_w, rope_s, n_rope=5, tn=LANES)
    qi_s = idx_s[:, :IDX_HEADS * IDX_DIM].reshape(bd, IDX_HEADS, IDX_DIM)
    ki_s = idx_s[:, IDX_HEADS * IDX_DIM:IDX_HEADS * IDX_DIM + IDX_DIM]
    wi_s = idx_s[:, 5 * LANES:5 * LANES + IDX_HEADS].reshape(bd, IDX_HEADS, 1)
    n_pages = page_ids.shape[1]
    scores = _dsa_score_sample(qi_s, wi_s, ki_s.reshape(bd, 1, IDX_DIM), cache_kidx, page_ids)
    n_keys = past_len + 1
    bias_t = _select_bias(scores.reshape(bd, (n_pages + 1) * PAGE_SIZE).T, n_keys, min(IDX_TOPK_MAX, n_keys // 4))
    bias_s = bias_t.T.reshape(bd, n_pages + 1, PAGE_SIZE)
    o_s = _decode_attention(q_s, cache_k, cache_v, page_ids, k_s, v_s, mode="dsa",
                            n_pp=_pages_per_step(n_pages), bias=bias_s)
    leaves_p = (_heads_last(k_pt, DSA_HEADS), _heads_last(v_pt, DSA_HEADS), kd_pt[:, :IDX_DIM, :].transpose(0, 2, 1))
    leaves_s = (_heads_last(k_st, DSA_HEADS).reshape(bd, 1, DSA_HEADS, DSA_HEAD_DIM),
                _heads_last(v_st, DSA_HEADS).reshape(bd, 1, DSA_HEADS, DSA_HEAD_DIM),
                kd_st[0, :IDX_DIM, :].T.reshape(bd, 1, IDX_DIM))
    return (o_p.reshape(b * s, d), o_s) + leaves_p + leaves_s


def _gla_mixer(xp, xs, dims, state, j, w_in, w_gate_up, gate_b, norm_g):
    b, s, bd = dims
    d = D_MODEL
    c_g = 2 * GLA_DK + 2 * GLA_DV
    w_g = jnp.pad(w_in[j][:, c_g:c_g + GLA_GATE_RANK], ((0, 0), (0, LANES - GLA_GATE_RANK)))[None]
    wgu = jnp.pad(w_gate_up[j], ((0, LANES - GLA_GATE_RANK), (0, 0)))
    outs = []
    for x in (xp, xs):
        qk = _project(x, w_in, j, 0, 2 * GLA_DK)
        v = _project(x, w_in, j, 2 * GLA_DK, GLA_DV)
        r = _project(x, w_in, j, 2 * GLA_DK + GLA_DV, GLA_DV)
        glow = _project(x, w_g, 0, 0, LANES, tn=LANES)
        outs.append((qk, v, r, glow))
    qk, v, r, glow = outs[0]
    o_p, st_p = _gla_prompt(qk.reshape(b, s, -1), v.reshape(b, s, -1), r.reshape(b, s, -1), glow.reshape(b, s, -1),
                            wgu, gate_b[j], norm_g[j], c=GLA_HEAD_K)
    qk, v, r, glow = outs[1]
    n_state = state.shape[1]
    o_s, st_s = _gla_sample(qk, v, r, glow, wgu, gate_b[j], norm_g[j],
                            state.reshape((-1,) + state.shape[2:]), j * n_state)
    return o_p.reshape(b * s, d), o_s, st_p, st_s


def kernel(x_prompt, x_sample, cache_diff_k, cache_diff_v, cache_dsa_k, cache_dsa_v, cache_dsa_kidx, state_gla,
           page_table, ln_mix_g, ln_mix_b, ln_ffn_g, ln_ffn_b, ffn_w_gate_up, ffn_w_down, diff_w_in, diff_lambda,
           diff_subln_g, diff_w_out, dsa_w_in, dsa_w_out, gla_w_in, gla_w_gate_up, gla_gate_b, gla_norm_g,
           gla_w_out):
    b, s, d = x_prompt.shape
    bd, s_d, _ = x_sample.shape
    assert s_d == 1 and d == D_MODEL
    dims = (b, s, bd)
    n_pool, page = cache_diff_k.shape[1], cache_diff_k.shape[2]
    past_len = page_table.shape[1] * page
    xp = x_prompt.reshape(b * s, d)
    xs = x_sample.reshape(bd, d)
    pos_p = jnp.arange(s, dtype=jnp.int32)
    pos_s = jnp.full((bd,), past_len, dtype=jnp.int32)
    ropes = (_rope_tables(pos_p), _rope_tables(pos_s), _rope_tables_t(pos_p), _rope_tables_t(pos_s))

    def feature_major(c):
        perm = (0, 1) + tuple(range(3, c.ndim)) + (2,)
        return jnp.transpose(c, perm).reshape(c.shape[0] * c.shape[1], -1, page)

    cdk, csk, csv, csi = (feature_major(c) for c in (cache_diff_k, cache_dsa_k, cache_dsa_v, cache_dsa_kidx))
    cdv = cache_diff_v.reshape(cache_diff_v.shape[0] * n_pool, page * DIFF_HEADS, LANES)
    tm_p = _row_tile(b * s, 512)
    tm_s = _row_tile(bd, 512)
    tf = 256
    dk_p, dv_p, dk_s, dv_s = [], [], [], []
    sk_p, sv_p, si_p, sk_s, sv_s, si_s = [], [], [], [], [], []
    gs_p, gs_s = [], []
    for i in range(DEPTH):
        kind, j = i % 3, i // 3
        page_ids = page_table + j * n_pool
        if kind == 0:
            lam_init = 0.8 - 0.6 * math.exp(-0.3 * i)
            o_p, o_s, k_p, v_p, k_s, v_s = _diff_mixer(xp, xs, dims, cdk, cdv, page_ids, diff_w_in, j,
                                                       diff_lambda[j], diff_subln_g[j], lam_init, ropes)
            dk_p.append(k_p)
            dv_p.append(v_p)
            dk_s.append(k_s)
            dv_s.append(v_s)
            w_out = diff_w_out
        elif kind == 1:
            o_p, o_s, k_p, v_p, ki_p, k_s, v_s, ki_s = _dsa_mixer(xp, xs, dims, csk, csv, csi, page_ids, dsa_w_in, j,
                                                                  ropes, past_len)
            sk_p.append(k_p)
            sv_p.append(v_p)
            si_p.append(ki_p)
            sk_s.append(k_s)
            sv_s.append(v_s)
            si_s.append(ki_s)
            w_out = dsa_w_out
        else:
            o_p, o_s, st_p, st_s = _gla_mixer(xp, xs, dims, state_gla, j, gla_w_in, gla_w_gate_up, gla_gate_b,
                                              gla_norm_g)
            gs_p.append(st_p)
            gs_s.append(st_s)
            w_out = gla_w_out
        xp = _linear_res_ln(o_p, w_out, j, xp, ln_mix_g, ln_mix_b, i, tm=tm_p)
        xs = _linear_res_ln(o_s, w_out, j, xs, ln_mix_g, ln_mix_b, i, tm=tm_s)
        xp = _ffn_ln(xp, ffn_w_gate_up, ffn_w_down, ln_ffn_g, ln_ffn_b, i, tm=_row_tile(b * s, 1024), tf=tf)
        xs = _ffn_ln(xs, ffn_w_gate_up, ffn_w_down, ln_ffn_g, ln_ffn_b, i, tm=tm_s, tf=tf)
    return (xp.reshape(b, s, d), xs.reshape(bd, 1, d),
            jnp.stack(dk_p), jnp.stack(dv_p), jnp.stack(dk_s), jnp.stack(dv_s),
            jnp.stack(sk_p), jnp.stack(sv_p), jnp.stack(si_p),
            jnp.stack(sk_s), jnp.stack(sv_s), jnp.stack(si_s),
            jnp.stack(gs_p), jnp.stack(gs_s))
```

```python
import functools
import math

import jax
import jax.numpy as jnp
from jax import lax
from jax.experimental import pallas as pl
from jax.experimental.pallas import tpu as pltpu

F32 = jnp.float32
BF16 = jnp.bfloat16

D_MODEL = 1024
DEPTH = 4
PAGE_SIZE = 128
DIFF_HEADS = 8
DIFF_HEAD_DIM = 64
DSA_HEADS = 16
DSA_HEAD_DIM = 64
IDX_HEADS = 8
IDX_DIM = 64
IDX_TOPK_MAX = 256
GLA_HEADS = 4
GLA_DK = D_MODEL // 2
GLA_DV = D_MODEL
GLA_HEAD_K = GLA_DK // GLA_HEADS
GLA_HEAD_V = GLA_DV // GLA_HEADS
GLA_GATE_RANK = 16
GLA_TAU = 16.0
D_FF = (((8 * D_MODEL + 2) // 3 + 255) // 256) * 256
ROPE_THETA = 10000.0
LN_EPS = 1e-5
DEEPNORM_ALPHA = (2.0 * DEPTH) ** 0.25

LANES = 128
SUBLANES = 8
VMEM_LIMIT = 56 * 1024 * 1024
NEG = -1e30
INT_MIN = -2 ** 31

N_GROUPS = D_MODEL // LANES


def _cparams(sem):
    return pltpu.CompilerParams(dimension_semantics=sem, vmem_limit_bytes=VMEM_LIMIT)


def _bdot(a, b):
    return jnp.dot(a.astype(BF16), b.astype(BF16), preferred_element_type=F32)


def _bdot_nt(a, b):
    return lax.dot_general(a.astype(BF16), b.astype(BF16), (((1,), (1,)), ((), ())),
                           preferred_element_type=F32)


def _bdot_tn(a, b):
    return lax.dot_general(a.astype(BF16), b.astype(BF16), (((0,), (0,)), ((), ())),
                           preferred_element_type=F32)


def _layer_norm_rows(z, g, b):
    mu = jnp.mean(z, axis=-1, keepdims=True)
    zc = z - mu
    var = jnp.mean(zc * zc, axis=-1, keepdims=True)
    return zc * lax.rsqrt(var + LN_EPS) * g + b


def _sigmoid(x):
    return 1.0 / (1.0 + jnp.exp(-x))


def _rope_tables(pos):
    d = DIFF_HEAD_DIM
    inv = ROPE_THETA ** (-jnp.arange(0, d, 2, dtype=F32) / d)
    ang = pos.astype(F32)[:, None] * inv[None, :]
    cos = jnp.cos(ang)
    sin = jnp.sin(ang)
    return (jnp.concatenate([cos, cos, cos, cos], axis=-1),
            jnp.concatenate([-sin, sin, -sin, sin], axis=-1))


def _rope_apply(y, cos, sin, first_half):
    partner = jnp.where(first_half, pltpu.roll(y, 96, 1), pltpu.roll(y, 32, 1))
    return y * cos + partner * sin


def _linear_kernel(x_ref, w_ref, *rest, n_rope):
    if n_rope:
        cos_ref, sin_ref, o_ref = rest
    else:
        (o_ref,) = rest
    acc = _bdot(x_ref[...], w_ref[...])
    tm, tn = acc.shape
    if not n_rope:
        o_ref[...] = acc
        return
    j = pl.program_id(1)

    @pl.when(j < n_rope)
    def _():
        cos = cos_ref[...]
        sin = sin_ref[...]
        lane = lax.broadcasted_iota(jnp.int32, (tm, LANES), 1)
        first_half = (lane % DIFF_HEAD_DIM) < (DIFF_HEAD_DIM // 2)
        for c in range(tn // LANES):
            sl = slice(c * LANES, (c + 1) * LANES)
            o_ref[:, sl] = _rope_apply(acc[:, sl], cos, sin, first_half)

    @pl.when(j >= n_rope)
    def _():
        o_ref[...] = acc


def _linear(x, w, layer, col0, width, *, tm, tn, rope=None, n_rope=None):
    n, k = x.shape
    assert n % tm == 0 and width % tn == 0 and col0 % tn == 0
    nj = width // tn
    if rope is None:
        n_rope = 0
    elif n_rope is None:
        n_rope = nj
    in_specs = [pl.BlockSpec((tm, k), lambda i, j: (i, 0)),
                pl.BlockSpec((None, k, tn), lambda i, j: (layer, 0, col0 // tn + j))]
    args = [x, w]
    if n_rope:
        p_blocks = rope[0].shape[0] // tm
        assert rope[0].shape[0] % tm == 0
        spec = pl.BlockSpec((tm, LANES), lambda i, j: (i % p_blocks, 0))
        in_specs += [spec, spec]
        args += [rope[0], rope[1]]
    return pl.pallas_call(
        functools.partial(_linear_kernel, n_rope=n_rope),
        out_shape=jax.ShapeDtypeStruct((n, width), F32),
        grid=(n // tm, nj),
        in_specs=in_specs,
        out_specs=pl.BlockSpec((tm, tn), lambda i, j: (i, j)),
        compiler_params=_cparams(("parallel", "arbitrary")),
        name="linear",
    )(*args)


def _rope_tables_t(pos):
    cos, sin = _rope_tables(pos)
    return cos.T, sin.T


def _linear_t_kernel(w_ref, x_ref, *rest, rope, with_bf16):
    n_out = 2 if with_bf16 else 1
    o_refs = rest[-n_out:]
    acc = _bdot_nt(w_ref[...], x_ref[...])
    tc, tt = acc.shape

    def emit(sl, y):
        o_refs[0][sl, :] = y
        if with_bf16:
            o_refs[1][sl, :] = y.astype(BF16)

    if not rope:
        emit(slice(None), acc)
        return
    cos = rest[0][...]
    sin = rest[1][...]
    row = lax.broadcasted_iota(jnp.int32, (LANES, tt), 0)
    first_half = (row % DIFF_HEAD_DIM) < (DIFF_HEAD_DIM // 2)
    for c in range(tc // LANES):
        sl = slice(c * LANES, (c + 1) * LANES)
        y = acc[sl, :]
        partner = jnp.where(first_half, pltpu.roll(y, LANES - DIFF_HEAD_DIM // 2, 0),
                            pltpu.roll(y, DIFF_HEAD_DIM // 2, 0))
        emit(sl, y * cos + partner * sin)


def _linear_t(x, w_t, nb, *, tc, tt, rope=None, with_bf16=False):
    n, k = x.shape
    c = w_t.shape[0]
    s = n // nb
    assert n == nb * s and s % tt == 0 and c % tc == 0 and tc % LANES == 0
    nt = s // tt
    in_specs = [pl.BlockSpec((tc, k), lambda b, ti, j: (j, 0)),
                pl.BlockSpec((tt, k), lambda b, ti, j: (b * nt + ti, 0))]
    args = [w_t, x]
    if rope is not None:
        p_blocks = rope[0].shape[1] // tt
        spec = pl.BlockSpec((LANES, tt), lambda b, ti, j: (0, ti % p_blocks))
        in_specs += [spec, spec]
        args += [rope[0], rope[1]]
    out_spec = pl.BlockSpec((None, tc, tt), lambda b, ti, j: (b, j, ti))
    out_shape = [jax.ShapeDtypeStruct((nb, c, s), F32)]
    if with_bf16:
        out_shape.append(jax.ShapeDtypeStruct((nb, c, s), BF16))
    outs = pl.pallas_call(
        functools.partial(_linear_t_kernel, rope=rope is not None, with_bf16=with_bf16),
        out_shape=out_shape,
        grid=(nb, nt, c // tc),
        in_specs=in_specs,
        out_specs=[out_spec] * len(out_shape),
        compiler_params=_cparams(("parallel", "parallel", "arbitrary")),
        name="linear_t",
    )(*args)
    return tuple(outs) if with_bf16 else outs[0]


def _linear_heads_kernel(x_ref, w_ref, *o_refs, leaf, groups):
    xb = x_ref[...].astype(BF16)
    o_refs = list(o_refs)
    leaf_ref = o_refs.pop(0) if leaf else None
    grp_ref = o_refs.pop(0) if groups else None
    for h in range(w_ref.shape[1] // LANES):
        y = jnp.dot(xb, w_ref[:, h * LANES:(h + 1) * LANES].astype(BF16), preferred_element_type=F32)
        if leaf:
            leaf_ref[:, h, :] = y
        if groups:
            grp_ref[h] = y.astype(BF16)


def _linear_heads(x, w, layer, col0, n_heads, *, tm, leaf=True, groups=False):
    n, k = x.shape
    width = n_heads * LANES
    assert n % tm == 0 and col0 % width == 0 and (leaf or groups)
    out_shape, out_specs = [], []
    if leaf:
        out_shape.append(jax.ShapeDtypeStruct((n, n_heads, LANES), F32))
        out_specs.append(pl.BlockSpec((tm, n_heads, LANES), lambda i: (i, 0, 0)))
    if groups:
        out_shape.append(jax.ShapeDtypeStruct((n_heads, n, LANES), BF16))
        out_specs.append(pl.BlockSpec((n_heads, tm, LANES), lambda i: (0, i, 0)))
    outs = pl.pallas_call(
        functools.partial(_linear_heads_kernel, leaf=leaf, groups=groups),
        out_shape=out_shape,
        grid=(n // tm,),
        in_specs=[pl.BlockSpec((tm, k), lambda i: (i, 0)),
                  pl.BlockSpec((None, k, width), lambda i: (layer, 0, col0 // width))],
        out_specs=out_specs,
        compiler_params=_cparams(("parallel",)),
        name="linear_heads",
    )(x, w)
    return outs[0] if len(outs) == 1 else tuple(outs)


LOG2E = 1.4426950408889634


def _q_pairs_kernel(x_ref, w_ref, cos_ref, sin_ref, o_ref, *, scale):
    acc = _bdot(x_ref[...], w_ref[...])
    tm = acc.shape[0]
    cos = cos_ref[...]
    sin = sin_ref[...]
    lane = lax.broadcasted_iota(jnp.int32, (tm, LANES), 1)
    first_half = (lane % DIFF_HEAD_DIM) < (DIFF_HEAD_DIM // 2)
    lo = lane < DIFF_HEAD_DIM
    for g in range(N_GROUPS):
        y = _rope_apply(acc[:, g * LANES:(g + 1) * LANES], cos, sin, first_half) * scale
        o_ref[g] = jnp.where(lo, y, 0.0).astype(BF16)
        o_ref[N_GROUPS + g] = jnp.where(lo, 0.0, y).astype(BF16)


def _q_pairs(x, w, layer, rope, *, tm):
    n, k = x.shape
    d = D_MODEL
    assert n % tm == 0 and rope[0].shape[0] % tm == 0
    p_blocks = rope[0].shape[0] // tm
    rspec = pl.BlockSpec((tm, LANES), lambda i: (i % p_blocks, 0))
    return pl.pallas_call(
        functools.partial(_q_pairs_kernel, scale=DIFF_HEAD_DIM ** -0.5 * LOG2E),
        out_shape=jax.ShapeDtypeStruct((2 * N_GROUPS, n, LANES), BF16),
        grid=(n // tm,),
        in_specs=[pl.BlockSpec((tm, k), lambda i: (i, 0)),
                  pl.BlockSpec((None, k, d), lambda i: (layer, 0, 0)), rspec, rspec],
        out_specs=pl.BlockSpec((2 * N_GROUPS, tm, LANES), lambda i: (0, i, 0)),
        compiler_params=_cparams(("parallel",)),
        name="q_pairs",
    )(x, w, rope[0], rope[1])


def _res_ln_kernel(x_ref, w_ref, r_ref, g_ref, b_ref, o_ref):
    y = _bdot(x_ref[...], w_ref[...])
    z = DEEPNORM_ALPHA * r_ref[...] + y
    o_ref[...] = _layer_norm_rows(z, g_ref[...], b_ref[...])


def _linear_res_ln(x, w, layer, resid, g, b, ln_layer, *, tm):
    n, k = x.shape
    d = w.shape[2]
    assert n % tm == 0
    row = lambda i: (i, 0)
    ln_row = lambda i: (ln_layer, 0, 0)
    return pl.pallas_call(
        _res_ln_kernel,
        out_shape=jax.ShapeDtypeStruct((n, d), F32),
        grid=(n // tm,),
        in_specs=[pl.BlockSpec((tm, k), row), pl.BlockSpec((None, k, d), lambda i: (layer, 0, 0)),
                  pl.BlockSpec((tm, d), row),
                  pl.BlockSpec((None, 1, d), ln_row), pl.BlockSpec((None, 1, d), ln_row)],
        out_specs=pl.BlockSpec((tm, d), row),
        compiler_params=_cparams(("parallel",)),
        name="out_proj_ln",
    )(x, w, resid, g.reshape(-1, 1, d), b.reshape(-1, 1, d))


def _ffn_kernel(x_ref, wg_ref, wu_ref, wd_ref, g_ref, b_ref, o_ref, acc_ref, xb_ref):
    f = pl.program_id(1)

    @pl.when(f == 0)
    def _():
        acc_ref[...] = jnp.zeros_like(acc_ref)
        xb_ref[...] = x_ref[...].astype(BF16)

    xb = xb_ref[...]
    gate = jnp.dot(xb, wg_ref[...].astype(BF16), preferred_element_type=F32)
    up = jnp.dot(xb, wu_ref[...].astype(BF16), preferred_element_type=F32)
    h = gate * _sigmoid(gate) * up
    acc_ref[...] += _bdot(h, wd_ref[...])

    @pl.when(f == pl.num_programs(1) - 1)
    def _():
        z = DEEPNORM_ALPHA * x_ref[...] + acc_ref[...]
        o_ref[...] = _layer_norm_rows(z, g_ref[...], b_ref[...])


def _ffn_ln(x, w_gate_up, w_down, g, b, layer, *, tm, tf):
    n, d = x.shape
    dff = w_down.shape[1]
    assert n % tm == 0 and dff % tf == 0
    nf = dff // tf
    ln_row = lambda i, f: (layer, 0, 0)
    return pl.pallas_call(
        _ffn_kernel,
        out_shape=jax.ShapeDtypeStruct((n, d), F32),
        grid=(n // tm, nf),
        in_specs=[pl.BlockSpec((tm, d), lambda i, f: (i, 0)),
                  pl.BlockSpec((None, d, tf), lambda i, f: (layer, 0, f)),
                  pl.BlockSpec((None, d, tf), lambda i, f: (layer, 0, nf + f)),
                  pl.BlockSpec((None, tf, d), lambda i, f: (layer, f, 0)),
                  pl.BlockSpec((None, 1, d), ln_row), pl.BlockSpec((None, 1, d), ln_row)],
        out_specs=pl.BlockSpec((tm, d), lambda i, f: (i, 0)),
        scratch_shapes=[pltpu.VMEM((tm, d), F32), pltpu.VMEM((tm, d), BF16)],
        compiler_params=_cparams(("parallel", "arbitrary")),
        name="ffn_ln",
    )(x, w_gate_up, w_gate_up, w_down, g.reshape(-1, 1, d), b.reshape(-1, 1, d))


def _diff_lambda(lam_ref, lam_init):
    l = lam_ref[...]
    a = jnp.sum(l[0:1] * l[1:2], axis=-1, keepdims=True)
    c = jnp.sum(l[2:3] * l[3:4], axis=-1, keepdims=True)
    return jnp.exp(a) - jnp.exp(c) + lam_init


def _rms_rows(o, g):
    return o * lax.rsqrt(jnp.mean(o * o, axis=-1, keepdims=True) + LN_EPS) * g


def _flash_kernel(qi_ref, ki_ref, *refs, mode, t, lam_init):
    if mode == "diff":
        q_ref, k_ref, v_ref, lam_ref, g_ref, o_ref, m_sc, l_sc, acc_sc = refs
    else:
        q_ref, k_ref, v_ref, bias_ref, o_ref, m_sc, l_sc, acc_sc = refs
    step = pl.program_id(1)
    qi = qi_ref[step]
    ki = ki_ref[step]

    @pl.when(ki == 0)
    def _():
        m_sc[...] = jnp.full_like(m_sc, NEG)
        l_sc[...] = jnp.zeros_like(l_sc)
        acc_sc[...] = jnp.zeros_like(acc_sc)

    def block_update(causal):
        if causal:
            row = lax.broadcasted_iota(jnp.int32, (t, t), 0)
            col = lax.broadcasted_iota(jnp.int32, (t, t), 1)
            allowed = col <= row

        def group(g, carry):
            kb = k_ref[g]
            vb = v_ref[g]
            for j in range(2):
                idx = j * N_GROUPS + g
                s = jnp.dot(q_ref[idx], kb, preferred_element_type=F32)
                if mode == "dsa":
                    s = s + bias_ref[...]
                if causal:
                    s = jnp.where(allowed, s, NEG)
                m_prev = m_sc[idx]
                m_next = jnp.maximum(m_prev, jnp.max(s, axis=-1, keepdims=True))
                p = jnp.exp2(s - jnp.concatenate([m_next] * (t // LANES), axis=-1))
                alpha = jnp.exp2(m_prev - m_next)
                l_sc[idx] = alpha * l_sc[idx] + jnp.sum(p, axis=-1, keepdims=True)
                acc_sc[idx] = alpha * acc_sc[idx] + jnp.dot(p.astype(BF16), vb, preferred_element_type=F32)
                m_sc[idx] = m_next
            return carry

        lax.fori_loop(0, N_GROUPS, group, 0, unroll=2)

    if mode == "diff":
        pl.when(ki < qi)(lambda: block_update(False))
        pl.when(ki == qi)(lambda: block_update(True))
    else:
        block_update(False)

    @pl.when(ki == qi)
    def _():
        lane = lax.broadcasted_iota(jnp.int32, (1, LANES), 1)
        lo = lane < DIFF_HEAD_DIM
        if mode == "diff":
            lam = _diff_lambda(lam_ref, lam_init)
            gain = g_ref[...] * (1.0 - lam_init)
        for g in range(N_GROUPS):
            o_lo = acc_sc[g] / l_sc[g]
            o_hi = acc_sc[N_GROUPS + g] / l_sc[N_GROUPS + g]
            if mode == "diff":
                o_ref[:, g * LANES:(g + 1) * LANES] = _rms_rows(o_lo - lam * o_hi, gain)
            else:
                o_ref[:, g * LANES:(g + 1) * LANES] = jnp.where(lo, o_lo, o_hi)


def _flash_attention(q2, k_t, v_g, b, *, mode, t, lam_params=None, subln_g=None, lam_init=0.0, bias=None):
    n = q2.shape[1]
    s = n // b
    d = D_MODEL
    assert s % t == 0 and n == b * s
    nblk = s // t
    k4 = k_t.reshape(b, N_GROUPS, LANES, s)
    pairs = [(qb, kb) for qb in range(nblk) for kb in range(qb + 1)]
    qi_of = jnp.asarray([p[0] for p in pairs], dtype=jnp.int32)
    ki_of = jnp.asarray([p[1] for p in pairs], dtype=jnp.int32)
    qspec = pl.BlockSpec((2 * N_GROUPS, t, LANES), lambda bi, st, qo, ko: (0, bi * nblk + qo[st], 0))
    kspec = pl.BlockSpec((None, N_GROUPS, LANES, t), lambda bi, st, qo, ko: (bi, 0, 0, ko[st]))
    vspec = pl.BlockSpec((N_GROUPS, t, LANES), lambda bi, st, qo, ko: (0, bi * nblk + ko[st], 0))
    in_specs = [qspec, kspec, vspec]
    args = [q2, k4, v_g]
    if mode == "diff":
        in_specs += [pl.BlockSpec((4, DIFF_HEAD_DIM), lambda bi, st, qo, ko: (0, 0)),
                     pl.BlockSpec((1, LANES), lambda bi, st, qo, ko: (0, 0))]
        args += [lam_params, subln_g.reshape(1, LANES)]
    else:
        assert bias.shape == (b, nblk, s, t)
        in_specs += [pl.BlockSpec((None, None, t, t), lambda bi, st, qo, ko: (bi, ko[st], qo[st], 0))]
        args += [bias]
    stat = pltpu.VMEM((2 * N_GROUPS, t, LANES), F32)
    return pl.pallas_call(
        functools.partial(_flash_kernel, mode=mode, t=t, lam_init=lam_init),
        out_shape=jax.ShapeDtypeStruct((n, d), F32),
        grid_spec=pltpu.PrefetchScalarGridSpec(
            num_scalar_prefetch=2,
            grid=(b, len(pairs)),
            in_specs=in_specs,
            out_specs=pl.BlockSpec((t, d), lambda bi, st, qo, ko: (bi * nblk + qo[st], 0)),
            scratch_shapes=[stat, stat, stat]),
        compiler_params=_cparams(("parallel", "arbitrary")),
        name="flash_" + mode,
    )(qi_of, ki_of, *args)


def _decode_kernel(pt_ref, q_ref, *refs, mode, n_pp, n_pages, scale, lam_init):
    del pt_ref
    k_refs = refs[:n_pp]
    v_refs = refs[n_pp:2 * n_pp]
    knew_ref, vnew_ref = refs[2 * n_pp:2 * n_pp + 2]
    rest = refs[2 * n_pp + 2:]
    if mode == "diff":
        lam_ref, g_ref, o_ref, qm_sc, m_sc, l_sc, acc_sc, e_sc = rest
    else:
        bias_ref, o_ref, qm_sc, m_sc, l_sc, acc_sc = rest
    step = pl.program_id(1)
    n_sub = 2 * N_GROUPS
    row = lax.broadcasted_iota(jnp.int32, (n_sub, D_MODEL), 0)
    lane = lax.broadcasted_iota(jnp.int32, (n_sub, D_MODEL), 1)

    @pl.when(step == 0)
    def _():
        qm_sc[...] = jnp.where(lane // DIFF_HEAD_DIM == row, q_ref[...] * scale, 0.0)
        m_sc[...] = jnp.full_like(m_sc, NEG)
        l_sc[...] = jnp.zeros_like(l_sc)
        acc_sc[...] = jnp.zeros_like(acc_sc)
        if mode == "diff":
            tok = lax.broadcasted_iota(jnp.int32, (PAGE_SIZE, D_MODEL), 0)
            col = lax.broadcasted_iota(jnp.int32, (PAGE_SIZE, D_MODEL), 1)
            e_sc[...] = jnp.where(col // DIFF_HEADS == tok, 1.0, 0.0).astype(BF16)

    qm = qm_sc[...]
    s_parts = []
    for i in range(n_pp):
        s = _bdot(qm, k_refs[i][...])
        if mode == "dsa":
            s = s + bias_ref[pl.ds(step * n_pp + i, 1), :]
        s_parts.append(s)
    s_all = jnp.concatenate(s_parts, axis=-1)
    m_prev = m_sc[...]
    m_new = jnp.maximum(m_prev, jnp.max(s_all, axis=-1, keepdims=True))
    alpha = jnp.exp(m_prev - m_new)
    p_all = jnp.exp(s_all - m_new)
    l_sc[...] = alpha * l_sc[...] + jnp.sum(p_all, axis=-1, keepdims=True)
    pv = None
    for i in range(n_pp):
        p_i = p_all[:, i * PAGE_SIZE:(i + 1) * PAGE_SIZE]
        if mode == "diff":
            p_rows = jnp.where(lane % DIFF_HEADS == row // 2,
                               jnp.dot(p_i.astype(BF16), e_sc[...], preferred_element_type=F32), 0.0)
            part = _bdot(p_rows, v_refs[i][...])
        else:
            part = _bdot_nt(p_i, v_refs[i][...])
        pv = part if pv is None else pv + part
    acc_sc[...] = alpha * acc_sc[...] + pv
    m_sc[...] = m_new

    @pl.when(step == pl.num_programs(1) - 1)
    def _():
        s_new = jnp.sum(qm * knew_ref[...], axis=-1, keepdims=True)
        if mode == "dsa":
            s_new = s_new + bias_ref[n_pages:n_pages + 1, 0:1]
        m_prev = m_sc[...]
        m_fin = jnp.maximum(m_prev, s_new)
        alpha = jnp.exp(m_prev - m_fin)
        p_new = jnp.exp(s_new - m_fin)
        l_fin = alpha * l_sc[...] + p_new
        if mode == "diff":
            row_h = lax.broadcasted_iota(jnp.int32, (n_sub, LANES), 0) // 2
            v_rows = jnp.zeros((n_sub, LANES), F32)
            for h in range(DIFF_HEADS):
                v_rows = jnp.where(row_h == h, vnew_ref[:, h * LANES:(h + 1) * LANES], v_rows)
            o16 = (alpha * acc_sc[...] + p_new * v_rows) / l_fin
            lam = _diff_lambda(lam_ref, lam_init)
            gain = g_ref[...] * (1.0 - lam_init)
            for h in range(DIFF_HEADS):
                o_h = o16[2 * h:2 * h + 1, :] - lam * o16[2 * h + 1:2 * h + 2, :]
                o_ref[:, h * LANES:(h + 1) * LANES] = _rms_rows(o_h, gain)
        else:
            o16 = (alpha * acc_sc[...] + p_new * vnew_ref[...]) / l_fin
            o_ref[...] = jnp.sum(jnp.where(lane // DSA_HEAD_DIM == row, o16, 0.0), axis=0, keepdims=True)


def _decode_attention(q, cache_k, cache_v, page_ids, k_new, v_new, *, mode, n_pp,
                      lam_params=None, subln_g=None, lam_init=0.0, bias=None):
    bd, d = q.shape
    n_pages = page_ids.shape[1]
    assert n_pages % n_pp == 0 and cache_k.shape[1:] == (d, PAGE_SIZE) and cache_v.shape[1:] == (d, PAGE_SIZE)
    n_steps = n_pages // n_pp
    row_spec = pl.BlockSpec((None, 1, d), lambda b, s, pt: (b, 0, 0))

    def page_spec(i):
        return pl.BlockSpec((None, d, PAGE_SIZE), lambda b, s, pt: (pt[b * n_pages + s * n_pp + i], 0, 0))

    in_specs = [row_spec] + [page_spec(i) for i in range(n_pp)] * 2 + [row_spec, row_spec]
    args = [q.reshape(bd, 1, d)] + [cache_k] * n_pp + [cache_v] * n_pp + [k_new.reshape(bd, 1, d), v_new.reshape(bd, 1, d)]
    n_sub = 2 * N_GROUPS
    scratch = [pltpu.VMEM((n_sub, d), F32), pltpu.VMEM((n_sub, 1), F32), pltpu.VMEM((n_sub, 1), F32)]
    if mode == "diff":
        in_specs += [pl.BlockSpec((4, DIFF_HEAD_DIM), lambda b, s, pt: (0, 0)),
                     pl.BlockSpec((1, LANES), lambda b, s, pt: (0, 0))]
        args += [lam_params, subln_g.reshape(1, LANES)]
        scratch += [pltpu.VMEM((n_sub, LANES), F32), pltpu.VMEM((PAGE_SIZE, d), BF16)]
    else:
        in_specs += [pl.BlockSpec((None, n_pages + 1, PAGE_SIZE), lambda b, s, pt: (b, 0, 0))]
        args += [bias]
        scratch += [pltpu.VMEM((n_sub, d), F32)]
    out = pl.pallas_call(
        functools.partial(_decode_kernel, mode=mode, n_pp=n_pp, n_pages=n_pages,
                          scale=DIFF_HEAD_DIM ** -0.5, lam_init=lam_init),
        out_shape=jax.ShapeDtypeStruct((bd, 1, d), F32),
        grid_spec=pltpu.PrefetchScalarGridSpec(
            num_scalar_prefetch=1,
            grid=(bd, n_steps),
            in_specs=in_specs,
            out_specs=row_spec,
            scratch_shapes=scratch),
        compiler_params=_cparams(("parallel", "arbitrary")),
        name="decode_" + mode,
    )(page_ids.reshape(-1), *args)
    return out.reshape(bd, d)


def _sortable_keys(score):
    bits = lax.bitcast_convert_type(score, jnp.int32)
    key = jnp.where(bits < 0, bits ^ jnp.int32(0x7FFFFFFF), bits)
    return jnp.where(score == 0.0, 0, key)


def _kth_threshold(key_ref, nch, ch, ksel):
    def count_ge(cand):
        def body(c, acc):
            blk = key_ref[pl.ds(pl.multiple_of(c * ch, ch), ch), :]
            hit = jnp.where(blk >= cand, 1, 0).astype(jnp.int32)
            return acc + jnp.sum(hit.reshape(ch // SUBLANES, SUBLANES, LANES), axis=0)

        acc = lax.fori_loop(0, nch, body, jnp.zeros((SUBLANES, LANES), jnp.int32))
        return jnp.sum(acc, axis=0, keepdims=True)

    def bit_body(i, t):
        cand = t ^ lax.shift_left(jnp.int32(1), jnp.asarray(31 - i, dtype=jnp.int32))
        return jnp.where(count_ge(cand) >= ksel, cand, t)

    t = lax.fori_loop(0, 32, bit_body, jnp.full((1, LANES), INT_MIN, jnp.int32))
    n_above = count_ge(t + 1)
    need = ksel - n_above
    surplus = jnp.logical_and(t > INT_MIN, count_ge(t) - n_above > need)
    return t, need.astype(F32), surplus


INT16_MIN = -2 ** 15
HALF_BITS = 16


def _split_keys(key):
    hi = (key >> HALF_BITS).astype(jnp.int16)
    lo = ((key & (2 ** HALF_BITS - 1)) + INT16_MIN).astype(jnp.int16)
    return hi, lo


def _kth_threshold_halves(key_ref, hi_ref, lo_ref, nch, ch, ksel):
    pack = 2 * SUBLANES

    def chunk(ref, c):
        return ref[pl.ds(pl.multiple_of(c * ch, ch), ch), :]

    def count16(ref, pred):
        def body(c, acc):
            hit = jnp.where(pred(chunk(ref, c)), jnp.int16(1), jnp.int16(0))
            while hit.shape[0] > pack:
                half = hit.shape[0] // 2
                hit = hit[:half] + hit[half:]
            return acc + hit

        acc = lax.fori_loop(0, nch, body, jnp.zeros((pack, LANES), jnp.int16))
        return jnp.sum(acc.astype(jnp.int32), axis=0, keepdims=True)

    def search(ref, base_count):
        def bit_body(i, u):
            cand = u | lax.shift_left(jnp.int32(1), jnp.asarray(HALF_BITS - 1 - i, dtype=jnp.int32))
            cand16 = (cand + INT16_MIN).astype(jnp.int16)
            n = base_count + count16(ref, lambda blk: blk >= cand16)
            return jnp.where(n >= ksel, cand, u)

        return lax.fori_loop(0, HALF_BITS, bit_body, jnp.zeros((1, LANES), jnp.int32))

    zero = jnp.zeros((1, LANES), jnp.int32)
    t_hi = search(hi_ref, zero) + INT16_MIN
    t_hi16 = t_hi.astype(jnp.int16)
    n_hi_above = count16(hi_ref, lambda blk: blk > t_hi16)

    def mask_body(c, _):
        sl = pl.ds(pl.multiple_of(c * ch, ch), ch)
        lo_ref[sl, :] = jnp.where(hi_ref[sl, :] == t_hi16, lo_ref[sl, :], jnp.int16(INT16_MIN))
        return 0

    lax.fori_loop(0, nch, mask_body, 0)
    t_lo = search(lo_ref, n_hi_above)
    t = lax.shift_left(t_hi, jnp.int32(HALF_BITS)) | t_lo

    def count_ge(cand):
        def body(c, acc):
            hit = jnp.where(chunk(key_ref, c) >= cand, 1, 0).astype(jnp.int32)
            return acc + jnp.sum(hit.reshape(ch // SUBLANES, SUBLANES, LANES), axis=0)

        acc = lax.fori_loop(0, nch, body, jnp.zeros((SUBLANES, LANES), jnp.int32))
        return jnp.sum(acc, axis=0, keepdims=True)

    n_above = count_ge(t + 1)
    need = ksel - n_above
    surplus = jnp.logical_and(t > INT_MIN, count_ge(t) - n_above > need)
    return t, need.astype(F32), surplus


def _tri_ones(n):
    row = lax.broadcasted_iota(jnp.int32, (n, n), 0)
    col = lax.broadcasted_iota(jnp.int32, (n, n), 1)
    return jnp.where(col <= row, 1.0, 0.0).astype(BF16)


def _select_chunk(key, t, need, carry, tri):
    eq = key == t
    pref = jnp.dot(tri, jnp.where(eq, 1.0, 0.0).astype(BF16), preferred_element_type=F32) + carry
    sel = jnp.logical_or(key > t, jnp.logical_and(eq, pref <= need))
    sel = jnp.logical_and(sel, key > INT_MIN)
    return sel, pref[key.shape[0] - 1:, :]


def _dsa_index_prompt_kernel(idx_ref, kd_ref, o_ref, key_sc, hi_sc, lo_sc, *, tq, ch, ksel):
    qi = pl.program_id(1)
    n_ch_total = o_ref.shape[0]
    nch = (qi * tq + tq + ch - 1) // ch
    qblk = idx_ref[...]
    lane = lax.broadcasted_iota(jnp.int32, (1, LANES), 1)
    lo = lane < IDX_DIM
    parts = []
    for h in range(IDX_HEADS):
        grp = qblk[:, (h // 2) * LANES:(h // 2 + 1) * LANES]
        parts.append(jnp.where(lo if h % 2 == 0 else jnp.logical_not(lo), grp, 0.0).astype(BF16))
    qstack = jnp.concatenate(parts, axis=0)
    w = qblk[:, 5 * LANES:6 * LANES] * (IDX_HEADS ** -0.5 * IDX_DIM ** -0.5)
    w_cols = [w[:, h:h + 1] for h in range(IDX_HEADS)]
    qpos = qi * tq + lane

    def score_body(c, _):
        base = pl.multiple_of(c * ch, ch)
        r = _bdot(qstack, kd_ref[:, pl.ds(base, ch)])
        score = jnp.zeros((tq, ch), F32)
        for h in range(IDX_HEADS):
            score = score + w_cols[h] * jnp.maximum(r[h * tq:(h + 1) * tq, :], 0.0)
        kpos = base + lax.broadcasted_iota(jnp.int32, (ch, tq), 0)
        key = jnp.where(kpos <= qpos, _sortable_keys(jnp.transpose(score)), INT_MIN)
        key_sc[pl.ds(base, ch), :] = key
        hi_sc[pl.ds(base, ch), :], lo_sc[pl.ds(base, ch), :] = _split_keys(key)
        return 0

    lax.fori_loop(0, nch, score_body, 0)
    t, need, surplus = _kth_threshold_halves(key_sc, hi_sc, lo_sc, nch, ch, ksel)
    any_surplus = jnp.max(jnp.where(surplus, 1, 0)) > 0

    @pl.when(any_surplus)
    def _():
        tri = _tri_ones(ch)

        def out_body(c, carry):
            base = pl.multiple_of(c * ch, ch)
            sel, carry = _select_chunk(key_sc[pl.ds(base, ch), :], t, need, carry, tri)
            o_ref[c] = jnp.transpose(jnp.where(sel, 0.0, NEG))
            return carry

        lax.fori_loop(0, nch, out_body, jnp.zeros((1, LANES), F32))

    @pl.when(jnp.logical_not(any_surplus))
    def _():
        def out_body(c, _):
            key = key_sc[pl.ds(pl.multiple_of(c * ch, ch), ch), :]
            sel = jnp.logical_and(key >= t, key > INT_MIN)
            o_ref[c] = jnp.transpose(jnp.where(sel, 0.0, NEG))
            return 0

        lax.fori_loop(0, nch, out_body, 0)

    def fill_body(c, _):
        o_ref[c] = jnp.full((tq, ch), NEG, F32)
        return 0

    lax.fori_loop(nch, n_ch_total, fill_body, 0)


def _dsa_index_prompt(idx, kd_t, *, ch):
    b, s, w = idx.shape
    tq = LANES
    assert s % ch == 0 and ch % tq == 0
    ksel = min(IDX_TOPK_MAX, s // 4)
    return pl.pallas_call(
        functools.partial(_dsa_index_prompt_kernel, tq=tq, ch=ch, ksel=ksel),
        out_shape=jax.ShapeDtypeStruct((b, s // ch, s, ch), F32),
        grid=(b, s // tq),
        in_specs=[pl.BlockSpec((None, tq, w), lambda bi, qi: (bi, qi, 0)),
                  pl.BlockSpec((None, LANES, s), lambda bi, qi: (bi, 0, 0))],
        out_specs=pl.BlockSpec((None, s // ch, tq, ch), lambda bi, qi: (bi, 0, qi, 0)),
        scratch_shapes=[pltpu.VMEM((s, LANES), jnp.int32), pltpu.VMEM((s, LANES), jnp.int16),
                        pltpu.VMEM((s, LANES), jnp.int16)],
        compiler_params=_cparams(("parallel", "arbitrary")),
        name="dsa_index_prompt",
    )(idx, kd_t)


def _dsa_score_sample_kernel(pt_ref, q_ref, w_ref, knew_ref, *refs, n_pages):
    del pt_ref
    page_refs = refs[:n_pages]
    o_ref = refs[n_pages]
    q = q_ref[...]
    w = w_ref[...] * (IDX_HEADS ** -0.5 * IDX_DIM ** -0.5)
    for p in range(n_pages):
        s = _bdot(q, page_refs[p][...])
        o_ref[p:p + 1, :] = jnp.sum(w * jnp.maximum(s, 0.0), axis=0, keepdims=True)
    s_new = jnp.sum(q * knew_ref[...], axis=-1, keepdims=True)
    sc_new = jnp.sum(w * jnp.maximum(s_new, 0.0), axis=0, keepdims=True)
    lane = lax.broadcasted_iota(jnp.int32, (1, PAGE_SIZE), 1)
    o_ref[n_pages:n_pages + 1, :] = jnp.where(lane == 0, sc_new, NEG)


def _dsa_score_sample(qi, wi, ki_new, cache_kidx, page_ids):
    bd = qi.shape[0]
    n_pages = page_ids.shape[1]
    assert cache_kidx.shape[1:] == (IDX_DIM, PAGE_SIZE)

    def page_spec(p):
        return pl.BlockSpec((None, IDX_DIM, PAGE_SIZE), lambda b, pt: (pt[b * n_pages + p], 0, 0))

    return pl.pallas_call(
        functools.partial(_dsa_score_sample_kernel, n_pages=n_pages),
        out_shape=jax.ShapeDtypeStruct((bd, n_pages + 1, PAGE_SIZE), F32),
        grid_spec=pltpu.PrefetchScalarGridSpec(
            num_scalar_prefetch=1,
            grid=(bd,),
            in_specs=[pl.BlockSpec((None, IDX_HEADS, IDX_DIM), lambda b, pt: (b, 0, 0)),
                      pl.BlockSpec((None, IDX_HEADS, 1), lambda b, pt: (b, 0, 0)),
                      pl.BlockSpec((None, 1, IDX_DIM), lambda b, pt: (b, 0, 0))]
                     + [page_spec(p) for p in range(n_pages)],
            out_specs=pl.BlockSpec((None, n_pages + 1, PAGE_SIZE), lambda b, pt: (b, 0, 0))),
        compiler_params=_cparams(("parallel",)),
        name="dsa_score_sample",
    )(page_ids.reshape(-1), qi, wi, ki_new, *([cache_kidx] * n_pages))


def _select_bias_kernel(s_ref, o_ref, key_sc, *, n_keys, ch, ksel):
    n_rows = s_ref.shape[0]
    nch = n_rows // ch
    kpos = lax.broadcasted_iota(jnp.int32, (n_rows, LANES), 0)
    key_sc[...] = jnp.where(kpos < n_keys, _sortable_keys(s_ref[...]), INT_MIN)
    t, need, _ = _kth_threshold(key_sc, nch, ch, ksel)
    tri = _tri_ones(ch)
    carry = jnp.zeros((1, LANES), F32)
    for c in range(nch):
        sel, carry = _select_chunk(key_sc[c * ch:(c + 1) * ch, :], t, need, carry, tri)
        o_ref[c * ch:(c + 1) * ch, :] = jnp.where(sel, 0.0, NEG)


def _select_bias(score_t, n_keys, ksel):
    n_rows, n_q = score_t.shape
    assert n_q == LANES and n_rows % LANES == 0
    return pl.pallas_call(
        functools.partial(_select_bias_kernel, n_keys=n_keys, ch=LANES, ksel=ksel),
        out_shape=jax.ShapeDtypeStruct((n_rows, LANES), F32),
        scratch_shapes=[pltpu.VMEM((n_rows, LANES), jnp.int32)],
        compiler_params=pltpu.CompilerParams(vmem_limit_bytes=VMEM_LIMIT),
        name="dsa_select_sample",
    )(score_t)


def _log_sigmoid(x):
    return -(jnp.maximum(-x, 0.0) + jnp.log1p(jnp.exp(-jnp.abs(x))))


def _gla_gate(glow, wgu, gate_b):
    return _log_sigmoid(_bdot(glow, wgu) + gate_b) / GLA_TAU


def _column(row_vec, eye):
    n = eye.shape[0]
    return jnp.sum(jnp.where(eye, jnp.broadcast_to(row_vec, (n, n)), 0.0), axis=-1, keepdims=True)


def _level_reference(b_ref, half, c):
    sub = lax.broadcasted_iota(jnp.int32, (SUBLANES, LANES), 0)
    slabs = []
    for t0 in range(0, c, SUBLANES):
        cur = None
        for t in range(t0, t0 + SUBLANES, min(SUBLANES, 2 * half)):
            r = (t // (2 * half)) * (2 * half) + half - 1
            bc = jnp.broadcast_to(b_ref[r:r + 1, :], (SUBLANES, LANES))
            cur = bc if cur is None else jnp.where(sub >= (t - t0), bc, cur)
        slabs.append(cur)
    return jnp.concatenate(slabs, axis=0)


def _gla_prompt_kernel(q_ref, k_ref, v_ref, r_ref, gl_ref, wgu_ref, gb_ref, ng_ref, o_ref, st_ref,
                       s_sc, b_sc, *, c, hps):
    ci = pl.program_id(2)
    hk, hv = GLA_HEAD_K, GLA_HEAD_V

    @pl.when(ci == 0)
    def _():
        s_sc[...] = jnp.zeros_like(s_sc)

    g_all = _gla_gate(gl_ref[...], wgu_ref[...], gb_ref[...])
    row = lax.broadcasted_iota(jnp.int32, (c, c), 0)
    col = lax.broadcasted_iota(jnp.int32, (c, c), 1)
    tri = jnp.where(col <= row, 1.0, 0.0).astype(BF16)
    eye = row == col
    for hh in range(hps):
        ksl = slice(hh * hk, (hh + 1) * hk)
        vsl = slice(hh * hv, (hh + 1) * hv)
        g = g_all[:, ksl]
        g1 = g.astype(BF16)
        rem = g - g1.astype(F32)
        g2 = rem.astype(BF16)
        g3 = (rem - g2.astype(F32)).astype(BF16)
        bcum = (jnp.dot(tri, g1, preferred_element_type=F32) + jnp.dot(tri, g2, preferred_element_type=F32)
                + jnp.dot(tri, g3, preferred_element_type=F32))
        b_ref = b_sc.at[hh]
        b_ref[...] = bcum
        q = q_ref[:, ksl] * (GLA_HEAD_K ** -0.5)
        k = k_ref[:, ksl]
        v = v_ref[:, vsl]
        att = jnp.where(eye, jnp.sum(q * k, axis=-1, keepdims=True), 0.0)
        half = c // 2
        while half >= 1:
            ref = _level_reference(b_ref, half, c)
            qt = q * jnp.exp(jnp.minimum(bcum - ref, 0.0))
            kt = k * jnp.exp(jnp.minimum(ref - bcum, 0.0))
            valid = jnp.logical_and(row // (2 * half) == col // (2 * half),
                                    jnp.logical_and(row % (2 * half) >= half, col % (2 * half) < half))
            att = jnp.where(valid, _bdot_nt(qt, kt), att)
            half //= 2
        s0 = s_sc[hh]
        o = _bdot(q * jnp.exp(bcum), s0) + _bdot(att, v)
        o = _rms_rows(o, ng_ref[...])
        rr = r_ref[:, vsl]
        o_ref[:, vsl] = o * (rr * _sigmoid(rr))
        b_last = b_ref[c - 1:c, :]
        khat = k * jnp.exp(b_last - bcum)
        s_new = _column(jnp.exp(b_last), eye) * s0 + _bdot_tn(khat, v)
        s_sc[hh] = s_new

    @pl.when(ci == pl.num_programs(2) - 1)
    def _():
        st_ref[...] = s_sc[...]


def _gla_prompt(qk, v, r, glow, wgu, gate_b, norm_g, *, c, hps):
    b, s, _ = qk.shape
    assert s % c == 0 and c == GLA_HEAD_K and GLA_HEADS % hps == 0
    hk, hv = GLA_HEAD_K, GLA_HEAD_V
    nhb = GLA_HEADS // hps
    return pl.pallas_call(
        functools.partial(_gla_prompt_kernel, c=c, hps=hps),
        out_shape=(jax.ShapeDtypeStruct((b, s, GLA_DV), F32),
                   jax.ShapeDtypeStruct((b, GLA_HEADS, hk, hv), F32)),
        grid=(b, nhb, s // c),
        in_specs=[pl.BlockSpec((None, c, hps * hk), lambda bi, h, ci: (bi, ci, h)),
                  pl.BlockSpec((None, c, hps * hk), lambda bi, h, ci: (bi, ci, nhb + h)),
                  pl.BlockSpec((None, c, hps * hv), lambda bi, h, ci: (bi, ci, h)),
                  pl.BlockSpec((None, c, hps * hv), lambda bi, h, ci: (bi, ci, h)),
                  pl.BlockSpec((None, c, LANES), lambda bi, h, ci: (bi, ci, 0)),
                  pl.BlockSpec((LANES, hps * hk), lambda bi, h, ci: (0, h)),
                  pl.BlockSpec((1, hps * hk), lambda bi, h, ci: (0, h)),
                  pl.BlockSpec((1, hv), lambda bi, h, ci: (0, 0))],
        out_specs=(pl.BlockSpec((None, c, hps * hv), lambda bi, h, ci: (bi, ci, h)),
                   pl.BlockSpec((None, hps, hk, hv), lambda bi, h, ci: (bi, h, 0, 0))),
        scratch_shapes=[pltpu.VMEM((hps, hk, hv), F32), pltpu.VMEM((hps, c, hk), F32)],
        compiler_params=_cparams(("parallel", "parallel", "arbitrary")),
        name="gla_prompt",
    )(qk, qk, v, r, glow, wgu, gate_b.reshape(1, GLA_DK), norm_g.reshape(1, hv))


def _gla_sample_kernel(qk_ref, v_ref, r_ref, gl_ref, wgu_ref, gb_ref, ng_ref, st_ref, o_ref, nst_ref):
    hk, hv = GLA_HEAD_K, GLA_HEAD_V
    glow = jnp.broadcast_to(gl_ref[...], (SUBLANES, LANES))
    g = _gla_gate(glow, wgu_ref[...], gb_ref[...])[0:1, :]
    eye = (lax.broadcasted_iota(jnp.int32, (hk, hk), 0) == lax.broadcasted_iota(jnp.int32, (hk, hk), 1))
    for h in range(GLA_HEADS):
        ksl = slice(h * hk, (h + 1) * hk)
        vsl = slice(h * hv, (h + 1) * hv)
        q_col = _column(qk_ref[:, ksl] * (GLA_HEAD_K ** -0.5), eye)
        k_col = _column(qk_ref[:, GLA_DK + h * hk:GLA_DK + (h + 1) * hk], eye)
        a_col = _column(jnp.exp(g[:, ksl]), eye)
        s_new = a_col * st_ref[h] + k_col * v_ref[:, vsl]
        nst_ref[h] = s_new
        o = jnp.sum(q_col * s_new, axis=0, keepdims=True)
        rr = r_ref[:, vsl]
        o_ref[:, vsl] = _rms_rows(o, ng_ref[...]) * (rr * _sigmoid(rr))


def _gla_sample(qk, v, r, glow, wgu, gate_b, norm_g, state, state_row0):
    bd = qk.shape[0]
    hk, hv = GLA_HEAD_K, GLA_HEAD_V
    rowspec = lambda w: pl.BlockSpec((None, 1, w), lambda b: (b, 0, 0))
    fixed2 = lambda b: (0, 0)
    o, nst = pl.pallas_call(
        _gla_sample_kernel,
        out_shape=(jax.ShapeDtypeStruct((bd, 1, GLA_DV), F32),
                   jax.ShapeDtypeStruct((bd, GLA_HEADS, hk, hv), F32)),
        grid=(bd,),
        in_specs=[rowspec(2 * GLA_DK), rowspec(GLA_DV), rowspec(GLA_DV), rowspec(LANES),
                  pl.BlockSpec((LANES, GLA_DK), fixed2), pl.BlockSpec((1, GLA_DK), fixed2),
                  pl.BlockSpec((1, hv), fixed2),
                  pl.BlockSpec((None, GLA_HEADS, hk, hv), lambda b: (state_row0 + b, 0, 0, 0))],
        out_specs=(rowspec(GLA_DV), pl.BlockSpec((None, GLA_HEADS, hk, hv), lambda b: (b, 0, 0, 0))),
        compiler_params=_cparams(("parallel",)),
        name="gla_sample",
    )(qk.reshape(bd, 1, -1), v.reshape(bd, 1, -1), r.reshape(bd, 1, -1), glow.reshape(bd, 1, -1),
      wgu, gate_b.reshape(1, GLA_DK), norm_g.reshape(1, hv), state)
    return o.reshape(bd, GLA_DV), nst


def _row_tile(n, cap):
    t = cap
    while n % t:
        t //= 2
    return t


def _project(x, w, layer, col0, width, rope=None, n_rope=None, tn=None):
    tm = _row_tile(x.shape[0], 1024)
    if tn is None:
        tn = 512 if width % 512 == 0 else LANES
    return _linear(x, w, layer, col0, width, tm=tm, tn=tn, rope=rope, n_rope=n_rope)


def _pages_per_step(n_pages):
    for n in (16, 4, 2):
        if n_pages % n == 0:
            return n
    return 1


def _project_t(x, w_t, nb, rope=None, with_bf16=False):
    s = x.shape[0] // nb
    return _linear_t(x, w_t, nb, tc=min(512, w_t.shape[0]), tt=_row_tile(s, 512), rope=rope, with_bf16=with_bf16)


def _heads_last(x_t, n_heads):
    b, c, s = x_t.shape
    return x_t.reshape(b, n_heads, c // n_heads, s).transpose(0, 3, 1, 2)


def _diff_mixer(xp, xs, dims, cache_k, cache_v, page_ids, w_in, j, lam_params, subln_g, lam_init, ropes):
    b, s, bd = dims
    d = D_MODEL
    rope_p, rope_s, rope_pt, rope_st = ropes
    w_k_t = w_in[j][:, d:2 * d].T
    tm = _row_tile(b * s, 512)
    q2_p = _q_pairs(xp, w_in, j, rope_p, tm=_row_tile(s, 512))
    k_pt, k_pt16 = _project_t(xp, w_k_t, b, rope_pt, with_bf16=True)
    v_p, v_pg = _linear_heads(xp, w_in, j, 2 * d, DIFF_HEADS, tm=tm, groups=True)
    o_p = _flash_attention(q2_p, k_pt16, v_pg, b, mode="diff", t=_row_tile(s, 512),
                           lam_params=lam_params, subln_g=subln_g, lam_init=lam_init)
    q_s = _project(xs, w_in, j, 0, d, rope_s)
    k_s = _project(xs, w_in, j, d, d, rope_s)
    v_s = _project(xs, w_in, j, 2 * d, d)
    k_st = _project_t(xs, w_k_t, 1, rope_st)
    v_sh = _linear_heads(xs, w_in, j, 2 * d, DIFF_HEADS, tm=_row_tile(bd, 512))
    o_s = _decode_attention(q_s, cache_k, cache_v, page_ids, k_s, v_s, mode="diff",
                            n_pp=_pages_per_step(page_ids.shape[1]),
                            lam_params=lam_params, subln_g=subln_g, lam_init=lam_init)
    k_leaf_p = _heads_last(k_pt, 2 * DIFF_HEADS)
    v_leaf_p = v_p.reshape(b, s, DIFF_HEADS, LANES)
    k_leaf_s = _heads_last(k_st, 2 * DIFF_HEADS).reshape(bd, 1, 2 * DIFF_HEADS, DIFF_HEAD_DIM)
    v_leaf_s = v_sh.reshape(bd, 1, DIFF_HEADS, LANES)
    return o_p.reshape(b * s, d), o_s, k_leaf_p, v_leaf_p, k_leaf_s, v_leaf_s


def _dsa_mixer(xp, xs, dims, cache_k, cache_v, cache_kidx, page_ids, w_in, j, ropes, past_len):
    b, s, bd = dims
    d = D_MODEL
    rope_p, rope_s, rope_pt, rope_st = ropes
    w = w_in[j]
    c_qi, c_ki, c_wi = 3 * d, 3 * d + IDX_HEADS * IDX_DIM, 3 * d + IDX_HEADS * IDX_DIM + IDX_DIM
    w_idx = jnp.concatenate([w[:, c_qi:c_ki], w[:, c_ki:c_wi], w[:, c_ki:c_wi], w[:, c_wi:c_wi + IDX_HEADS],
                             jnp.zeros((d, LANES - IDX_HEADS), F32)], axis=1)[None]
    idx_w = w_idx.shape[2]
    w_k_t = w[:, d:2 * d].T
    w_v_t = w[:, 2 * d:3 * d].T
    w_ki2_t = jnp.concatenate([w[:, c_ki:c_wi], w[:, c_ki:c_wi]], axis=1).T

    q2_p = _q_pairs(xp, w_in, j, rope_p, tm=_row_tile(s, 512))
    k_pt, k_pt16 = _project_t(xp, w_k_t, b, rope_pt, with_bf16=True)
    v_pt = _project_t(xp, w_v_t, b)
    v_pg = _linear_heads(xp, w_in, j, 2 * d, N_GROUPS, tm=_row_tile(b * s, 512), leaf=False, groups=True)
    kd_pt = _project_t(xp, w_ki2_t, b, rope_pt)
    idx_p = _project(xp, w_idx, 0, 0, idx_w, rope_p, n_rope=5, tn=LANES)
    t = _row_tile(s, 512)
    bias_p = _dsa_index_prompt(idx_p.reshape(b, s, idx_w), kd_pt, ch=t)
    o_p = _flash_attention(q2_p, k_pt16, v_pg, b, mode="dsa", t=t, bias=bias_p)

    q_s = _project(xs, w_in, j, 0, d, rope_s)
    k_s = _project(xs, w_in, j, d, d, rope_s)
    v_s = _project(xs, w_in, j, 2 * d, d)
    k_st = _project_t(xs, w_k_t, 1, rope_st)
    v_st = _project_t(xs, w_v_t, 1)
    kd_st = _project_t(xs, w_ki2_t, 1, rope_st)
    idx_s = _project(xs, w_idx, 0, 0, idx_w, rope_s, n_rope=5, tn=LANES)
    qi_s = idx_s[:, :IDX_HEADS * IDX_DIM].reshape(bd, IDX_HEADS, IDX_DIM)
    ki_s = idx_s[:, IDX_HEADS * IDX_DIM:IDX_HEADS * IDX_DIM + IDX_DIM]
    wi_s = idx_s[:, 5 * LANES:5 * LANES + IDX_HEADS].reshape(bd, IDX_HEADS, 1)
    n_pages = page_ids.shape[1]
    scores = _dsa_score_sample(qi_s, wi_s, ki_s.reshape(bd, 1, IDX_DIM), cache_kidx, page_ids)
    n_keys = past_len + 1
    bias_t = _select_bias(scores.reshape(bd, (n_pages + 1) * PAGE_SIZE).T, n_keys, min(IDX_TOPK_MAX, n_keys // 4))
    bias_s = bias_t.T.reshape(bd, n_pages + 1, PAGE_SIZE)
    o_s = _decode_attention(q_s, cache_k, cache_v, page_ids, k_s, v_s, mode="dsa",
                            n_pp=_pages_per_step(n_pages), bias=bias_s)
    leaves_p = (_heads_last(k_pt, DSA_HEADS), _heads_last(v_pt, DSA_HEADS), kd_pt[:, :IDX_DIM, :].transpose(0, 2, 1))
    leaves_s = (_heads_last(k_st, DSA_HEADS).reshape(bd, 1, DSA_HEADS, DSA_HEAD_DIM),
                _heads_last(v_st, DSA_HEADS).reshape(bd, 1, DSA_HEADS, DSA_HEAD_DIM),
                kd_st[0, :IDX_DIM, :].T.reshape(bd, 1, IDX_DIM))
    return (o_p.reshape(b * s, d), o_s) + leaves_p + leaves_s


def _gla_mixer(xp, xs, dims, state, j, w_in, w_gate_up, gate_b, norm_g):
    b, s, bd = dims
    d = D_MODEL
    c_g = 2 * GLA_DK + 2 * GLA_DV
    w_g = jnp.pad(w_in[j][:, c_g:c_g + GLA_GATE_RANK], ((0, 0), (0, LANES - GLA_GATE_RANK)))[None]
    wgu = jnp.pad(w_gate_up[j], ((0, LANES - GLA_GATE_RANK), (0, 0)))
    outs = []
    for x in (xp, xs):
        qk = _project(x, w_in, j, 0, 2 * GLA_DK)
        v = _project(x, w_in, j, 2 * GLA_DK, GLA_DV)
        r = _project(x, w_in, j, 2 * GLA_DK + GLA_DV, GLA_DV)
        glow = _project(x, w_g, 0, 0, LANES, tn=LANES)
        outs.append((qk, v, r, glow))
    qk, v, r, glow = outs[0]
    o_p, st_p = _gla_prompt(qk.reshape(b, s, -1), v.reshape(b, s, -1), r.reshape(b, s, -1), glow.reshape(b, s, -1),
                            wgu, gate_b[j], norm_g[j], c=GLA_HEAD_K, hps=GLA_HEADS)
    qk, v, r, glow = outs[1]
    n_state = state.shape[1]
    o_s, st_s = _gla_sample(qk, v, r, glow, wgu, gate_b[j], norm_g[j],
                            state.reshape((-1,) + state.shape[2:]), j * n_state)
    return o_p.reshape(b * s, d), o_s, st_p, st_s


def kernel(x_prompt, x_sample, cache_diff_k, cache_diff_v, cache_dsa_k, cache_dsa_v, cache_dsa_kidx, state_gla,
           page_table, ln_mix_g, ln_mix_b, ln_ffn_g, ln_ffn_b, ffn_w_gate_up, ffn_w_down, diff_w_in, diff_lambda,
           diff_subln_g, diff_w_out, dsa_w_in, dsa_w_out, gla_w_in, gla_w_gate_up, gla_gate_b, gla_norm_g,
           gla_w_out):
    b, s, d = x_prompt.shape
    bd, s_d, _ = x_sample.shape
    assert s_d == 1 and d == D_MODEL
    dims = (b, s, bd)
    n_pool, page = cache_diff_k.shape[1], cache_diff_k.shape[2]
    past_len = page_table.shape[1] * page
    xp = x_prompt.reshape(b * s, d)
    xs = x_sample.reshape(bd, d)
    pos_p = jnp.arange(s, dtype=jnp.int32)
    pos_s = jnp.full((bd,), past_len, dtype=jnp.int32)
    ropes = (_rope_tables(pos_p), _rope_tables(pos_s), _rope_tables_t(pos_p), _rope_tables_t(pos_s))

    def feature_major(c):
        perm = (0, 1) + tuple(range(3, c.ndim)) + (2,)
        return jnp.transpose(c, perm).reshape(c.shape[0] * c.shape[1], -1, page)

    cdk, csk, csv, csi = (feature_major(c) for c in (cache_diff_k, cache_dsa_k, cache_dsa_v, cache_dsa_kidx))
    cdv = cache_diff_v.reshape(cache_diff_v.shape[0] * n_pool, page * DIFF_HEADS, LANES)
    tm_p = _row_tile(b * s, 512)
    tm_s = _row_tile(bd, 512)
    tf = 256
    dk_p, dv_p, dk_s, dv_s = [], [], [], []
    sk_p, sv_p, si_p, sk_s, sv_s, si_s = [], [], [], [], [], []
    gs_p, gs_s = [], []
    for i in range(DEPTH):
        kind, j = i % 3, i // 3
        page_ids = page_table + j * n_pool
        if kind == 0:
            lam_init = 0.8 - 0.6 * math.exp(-0.3 * i)
            o_p, o_s, k_p, v_p, k_s, v_s = _diff_mixer(xp, xs, dims, cdk, cdv, page_ids, diff_w_in, j,
                                                       diff_lambda[j], diff_subln_g[j], lam_init, ropes)
            dk_p.append(k_p)
            dv_p.append(v_p)
            dk_s.append(k_s)
            dv_s.append(v_s)
            w_out = diff_w_out
        elif kind == 1:
            o_p, o_s, k_p, v_p, ki_p, k_s, v_s, ki_s = _dsa_mixer(xp, xs, dims, csk, csv, csi, page_ids, dsa_w_in, j,
                                                                  ropes, past_len)
            sk_p.append(k_p)
            sv_p.append(v_p)
            si_p.append(ki_p)
            sk_s.append(k_s)
            sv_s.append(v_s)
            si_s.append(ki_s)
            w_out = dsa_w_out
        else:
            o_p, o_s, st_p, st_s = _gla_mixer(xp, xs, dims, state_gla, j, gla_w_in, gla_w_gate_up, gla_gate_b,
                                              gla_norm_g)
            gs_p.append(st_p)
            gs_s.append(st_s)
            w_out = gla_w_out
        xp = _linear_res_ln(o_p, w_out, j, xp, ln_mix_g, ln_mix_b, i, tm=tm_p)
        xs = _linear_res_ln(o_s, w_out, j, xs, ln_mix_g, ln_mix_b, i, tm=tm_s)
        xp = _ffn_ln(xp, ffn_w_gate_up, ffn_w_down, ln_ffn_g, ln_ffn_b, i, tm=_row_tile(b * s, 1024), tf=tf)
        xs = _ffn_ln(xs, ffn_w_gate_up, ffn_w_down, ln_ffn_g, ln_ffn_b, i, tm=tm_s, tf=tf)
    return (xp.reshape(b, s, d), xs.reshape(bd, 1, d),
            jnp.stack(dk_p), jnp.stack(dv_p), jnp.stack(dk_s), jnp.stack(dv_s),
            jnp.stack(sk_p), jnp.stack(sv_p), jnp.stack(si_p),
            jnp.stack(sk_s), jnp.stack(sv_s), jnp.stack(si_s),
            jnp.stack(gs_p), jnp.stack(gs_s))
```

```python
import functools
import math

import jax
import jax.numpy as jnp
from jax import lax
from jax.experimental import pallas as pl
from jax.experimental.pallas import tpu as pltpu

F32 = jnp.float32
BF16 = jnp.bfloat16

D_MODEL = 1024
DEPTH = 4
PAGE_SIZE = 128
DIFF_HEADS = 8
DIFF_HEAD_DIM = 64
DSA_HEADS = 16
DSA_HEAD_DIM = 64
IDX_HEADS = 8
IDX_DIM = 64
IDX_TOPK_MAX = 256
GLA_HEADS = 4
GLA_DK = D_MODEL // 2
GLA_DV = D_MODEL
GLA_HEAD_K = GLA_DK // GLA_HEADS
GLA_HEAD_V = GLA_DV // GLA_HEADS
GLA_GATE_RANK = 16
GLA_TAU = 16.0
D_FF = (((8 * D_MODEL + 2) // 3 + 255) // 256) * 256
ROPE_THETA = 10000.0
LN_EPS = 1e-5
DEEPNORM_ALPHA = (2.0 * DEPTH) ** 0.25

LANES = 128
SUBLANES = 8
VMEM_LIMIT = 56 * 1024 * 1024
NEG = -1e30
INT_MIN = -2 ** 31

N_GROUPS = D_MODEL // LANES


def _cparams(sem):
    return pltpu.CompilerParams(dimension_semantics=sem, vmem_limit_bytes=VMEM_LIMIT)


def _bdot(a, b):
    return jnp.dot(a.astype(BF16), b.astype(BF16), preferred_element_type=F32)


def _bdot_nt(a, b):
    return lax.dot_general(a.astype(BF16), b.astype(BF16), (((1,), (1,)), ((), ())),
                           preferred_element_type=F32)


def _bdot_tn(a, b):
    return lax.dot_general(a.astype(BF16), b.astype(BF16), (((0,), (0,)), ((), ())),
                           preferred_element_type=F32)


def _layer_norm_rows(z, g, b):
    mu = jnp.mean(z, axis=-1, keepdims=True)
    zc = z - mu
    var = jnp.mean(zc * zc, axis=-1, keepdims=True)
    return zc * lax.rsqrt(var + LN_EPS) * g + b


def _sigmoid(x):
    return 1.0 / (1.0 + jnp.exp(-x))


def _rope_tables(pos):
    d = DIFF_HEAD_DIM
    inv = ROPE_THETA ** (-jnp.arange(0, d, 2, dtype=F32) / d)
    ang = pos.astype(F32)[:, None] * inv[None, :]
    cos = jnp.cos(ang)
    sin = jnp.sin(ang)
    return (jnp.concatenate([cos, cos, cos, cos], axis=-1),
            jnp.concatenate([-sin, sin, -sin, sin], axis=-1))


def _rope_apply(y, cos, sin, first_half):
    partner = jnp.where(first_half, pltpu.roll(y, 96, 1), pltpu.roll(y, 32, 1))
    return y * cos + partner * sin


def _linear_kernel(x_ref, w_ref, *rest, n_rope, single_tile):
    if n_rope:
        cos_ref, sin_ref, o_ref = rest
    else:
        (o_ref,) = rest
    acc = _bdot(x_ref[...], w_ref[...])
    tm, tn = acc.shape
    if not n_rope:
        o_ref[...] = acc
        return

    def roped(n_chunks):
        cos = cos_ref[...]
        sin = sin_ref[...]
        lane = lax.broadcasted_iota(jnp.int32, (tm, LANES), 1)
        first_half = (lane % DIFF_HEAD_DIM) < (DIFF_HEAD_DIM // 2)
        for c in range(tn // LANES):
            sl = slice(c * LANES, (c + 1) * LANES)
            o_ref[:, sl] = _rope_apply(acc[:, sl], cos, sin, first_half) if c < n_chunks else acc[:, sl]

    if single_tile:
        roped(n_rope)
        return
    j = pl.program_id(1)
    pl.when(j < n_rope)(lambda: roped(tn // LANES))

    @pl.when(j >= n_rope)
    def _():
        o_ref[...] = acc


def _linear(x, w, layer, col0, width, *, tm, tn, rope=None, n_rope=None):
    n, k = x.shape
    assert n % tm == 0 and width % tn == 0 and col0 % tn == 0
    nj = width // tn
    if rope is None:
        n_rope = 0
    elif n_rope is None:
        n_rope = nj if nj > 1 else tn // LANES
    in_specs = [pl.BlockSpec((tm, k), lambda i, j: (i, 0)),
                pl.BlockSpec((None, k, tn), lambda i, j: (layer, 0, col0 // tn + j))]
    args = [x, w]
    if n_rope:
        p_blocks = rope[0].shape[0] // tm
        assert rope[0].shape[0] % tm == 0
        spec = pl.BlockSpec((tm, LANES), lambda i, j: (i % p_blocks, 0))
        in_specs += [spec, spec]
        args += [rope[0], rope[1]]
    return pl.pallas_call(
        functools.partial(_linear_kernel, n_rope=n_rope, single_tile=nj == 1),
        out_shape=jax.ShapeDtypeStruct((n, width), F32),
        grid=(n // tm, nj),
        in_specs=in_specs,
        out_specs=pl.BlockSpec((tm, tn), lambda i, j: (i, j)),
        compiler_params=_cparams(("parallel", "arbitrary")),
        name="linear",
    )(*args)


def _rope_tables_t(pos):
    cos, sin = _rope_tables(pos)
    return cos.T, sin.T


def _linear_t_kernel(w_ref, x_ref, *rest, rope, with_bf16, extra):
    rest = list(rest)
    we_ref = rest.pop(0) if extra else None
    cos_ref, sin_ref = (rest.pop(0), rest.pop(0)) if rope else (None, None)
    o_ref = rest.pop(0)
    o16_ref = rest.pop(0) if with_bf16 else None
    oe_ref = rest.pop(0) if extra else None
    tt = x_ref.shape[0]
    xb = x_ref[...].astype(BF16)

    def roped(y):
        row = lax.broadcasted_iota(jnp.int32, (LANES, tt), 0)
        first_half = (row % DIFF_HEAD_DIM) < (DIFF_HEAD_DIM // 2)
        partner = jnp.where(first_half, pltpu.roll(y, LANES - DIFF_HEAD_DIM // 2, 0),
                            pltpu.roll(y, DIFF_HEAD_DIM // 2, 0))
        return y * cos_ref[...] + partner * sin_ref[...]

    acc = _bdot_nt(w_ref[...], xb)
    for c in range(acc.shape[0] // LANES):
        sl = slice(c * LANES, (c + 1) * LANES)
        y = roped(acc[sl, :]) if rope else acc[sl, :]
        o_ref[sl, :] = y
        if with_bf16:
            o16_ref[sl, :] = y.astype(BF16)

    if extra:
        @pl.when(pl.program_id(2) == 0)
        def _():
            ye = _bdot_nt(we_ref[...], xb)
            oe_ref[...] = roped(ye) if rope else ye


def _linear_t(x, w_t, nb, *, tc, tt, rope=None, with_bf16=False, w_extra_t=None):
    n, k = x.shape
    c = w_t.shape[0]
    s = n // nb
    assert n == nb * s and s % tt == 0 and c % tc == 0 and tc % LANES == 0
    nt = s // tt
    extra = w_extra_t is not None
    in_specs = [pl.BlockSpec((tc, k), lambda b, ti, j: (j, 0)),
                pl.BlockSpec((tt, k), lambda b, ti, j: (b * nt + ti, 0))]
    args = [w_t, x]
    if extra:
        assert w_extra_t.shape == (LANES, k)
        in_specs.append(pl.BlockSpec((LANES, k), lambda b, ti, j: (0, 0)))
        args.append(w_extra_t)
    if rope is not None:
        p_blocks = rope[0].shape[1] // tt
        spec = pl.BlockSpec((LANES, tt), lambda b, ti, j: (0, ti % p_blocks))
        in_specs += [spec, spec]
        args += [rope[0], rope[1]]
    out_spec = pl.BlockSpec((None, tc, tt), lambda b, ti, j: (b, j, ti))
    out_shape = [jax.ShapeDtypeStruct((nb, c, s), F32)]
    out_specs = [out_spec]
    if with_bf16:
        out_shape.append(jax.ShapeDtypeStruct((nb, c, s), BF16))
        out_specs.append(out_spec)
    if extra:
        out_shape.append(jax.ShapeDtypeStruct((nb, LANES, s), F32))
        out_specs.append(pl.BlockSpec((None, LANES, tt), lambda b, ti, j: (b, 0, ti)))
    outs = pl.pallas_call(
        functools.partial(_linear_t_kernel, rope=rope is not None, with_bf16=with_bf16, extra=extra),
        out_shape=out_shape,
        grid=(nb, nt, c // tc),
        in_specs=in_specs,
        out_specs=out_specs,
        compiler_params=_cparams(("parallel", "parallel", "arbitrary")),
        name="linear_t",
    )(*args)
    return tuple(outs) if len(outs) > 1 else outs[0]


def _linear_heads_kernel(x_ref, w_ref, *o_refs, leaf, groups):
    xb = x_ref[...].astype(BF16)
    o_refs = list(o_refs)
    leaf_ref = o_refs.pop(0) if leaf else None
    grp_ref = o_refs.pop(0) if groups else None
    for h in range(w_ref.shape[1] // LANES):
        y = jnp.dot(xb, w_ref[:, h * LANES:(h + 1) * LANES].astype(BF16), preferred_element_type=F32)
        if leaf:
            leaf_ref[:, h, :] = y
        if groups:
            grp_ref[h] = y.astype(BF16)


def _linear_heads(x, w, layer, col0, n_heads, *, tm, leaf=True, groups=False):
    n, k = x.shape
    width = n_heads * LANES
    assert n % tm == 0 and col0 % width == 0 and (leaf or groups)
    out_shape, out_specs = [], []
    if leaf:
        out_shape.append(jax.ShapeDtypeStruct((n, n_heads, LANES), F32))
        out_specs.append(pl.BlockSpec((tm, n_heads, LANES), lambda i: (i, 0, 0)))
    if groups:
        out_shape.append(jax.ShapeDtypeStruct((n_heads, n, LANES), BF16))
        out_specs.append(pl.BlockSpec((n_heads, tm, LANES), lambda i: (0, i, 0)))
    outs = pl.pallas_call(
        functools.partial(_linear_heads_kernel, leaf=leaf, groups=groups),
        out_shape=out_shape,
        grid=(n // tm,),
        in_specs=[pl.BlockSpec((tm, k), lambda i: (i, 0)),
                  pl.BlockSpec((None, k, width), lambda i: (layer, 0, col0 // width))],
        out_specs=out_specs,
        compiler_params=_cparams(("parallel",)),
        name="linear_heads",
    )(x, w)
    return outs[0] if len(outs) == 1 else tuple(outs)


LOG2E = 1.4426950408889634


def _q_pairs_kernel(x_ref, w_ref, cos_ref, sin_ref, o_ref, *, scale):
    acc = _bdot(x_ref[...], w_ref[...])
    tm = acc.shape[0]
    cos = cos_ref[...]
    sin = sin_ref[...]
    lane = lax.broadcasted_iota(jnp.int32, (tm, LANES), 1)
    first_half = (lane % DIFF_HEAD_DIM) < (DIFF_HEAD_DIM // 2)
    lo = lane < DIFF_HEAD_DIM
    for g in range(N_GROUPS):
        y = _rope_apply(acc[:, g * LANES:(g + 1) * LANES], cos, sin, first_half) * scale
        o_ref[g, 0] = jnp.where(lo, y, 0.0).astype(BF16)
        o_ref[g, 1] = jnp.where(lo, 0.0, y).astype(BF16)


def _q_pairs(x, w, layer, rope, *, tm):
    n, k = x.shape
    d = D_MODEL
    assert n % tm == 0 and rope[0].shape[0] % tm == 0
    p_blocks = rope[0].shape[0] // tm
    rspec = pl.BlockSpec((tm, LANES), lambda i: (i % p_blocks, 0))
    return pl.pallas_call(
        functools.partial(_q_pairs_kernel, scale=DIFF_HEAD_DIM ** -0.5 * LOG2E),
        out_shape=jax.ShapeDtypeStruct((N_GROUPS, 2, n, LANES), BF16),
        grid=(n // tm,),
        in_specs=[pl.BlockSpec((tm, k), lambda i: (i, 0)),
                  pl.BlockSpec((None, k, d), lambda i: (layer, 0, 0)), rspec, rspec],
        out_specs=pl.BlockSpec((N_GROUPS, 2, tm, LANES), lambda i: (0, 0, i, 0)),
        compiler_params=_cparams(("parallel",)),
        name="q_pairs",
    )(x, w, rope[0], rope[1])


def _res_ln_kernel(x_ref, w_ref, r_ref, g_ref, b_ref, o_ref):
    y = _bdot(x_ref[...], w_ref[...])
    z = DEEPNORM_ALPHA * r_ref[...] + y
    o_ref[...] = _layer_norm_rows(z, g_ref[...], b_ref[...])


def _linear_res_ln(x, w, layer, resid, g, b, ln_layer, *, tm):
    n, k = x.shape
    d = w.shape[2]
    assert n % tm == 0
    row = lambda i: (i, 0)
    ln_row = lambda i: (ln_layer, 0, 0)
    return pl.pallas_call(
        _res_ln_kernel,
        out_shape=jax.ShapeDtypeStruct((n, d), F32),
        grid=(n // tm,),
        in_specs=[pl.BlockSpec((tm, k), row), pl.BlockSpec((None, k, d), lambda i: (layer, 0, 0)),
                  pl.BlockSpec((tm, d), row),
                  pl.BlockSpec((None, 1, d), ln_row), pl.BlockSpec((None, 1, d), ln_row)],
        out_specs=pl.BlockSpec((tm, d), row),
        compiler_params=_cparams(("parallel",)),
        name="out_proj_ln",
    )(x, w, resid, g.reshape(-1, 1, d), b.reshape(-1, 1, d))


def _ffn_kernel(x_ref, wg_ref, wu_ref, wd_ref, g_ref, b_ref, o_ref, acc_ref, xb_ref):
    f = pl.program_id(1)

    @pl.when(f == 0)
    def _():
        acc_ref[...] = jnp.zeros_like(acc_ref)
        xb_ref[...] = x_ref[...].astype(BF16)

    xb = xb_ref[...]
    gate = jnp.dot(xb, wg_ref[...].astype(BF16), preferred_element_type=F32)
    up = jnp.dot(xb, wu_ref[...].astype(BF16), preferred_element_type=F32)
    h = gate * _sigmoid(gate) * up
    acc_ref[...] += _bdot(h, wd_ref[...])

    @pl.when(f == pl.num_programs(1) - 1)
    def _():
        z = DEEPNORM_ALPHA * x_ref[...] + acc_ref[...]
        o_ref[...] = _layer_norm_rows(z, g_ref[...], b_ref[...])


def _ffn_ln(x, w_gate_up, w_down, g, b, layer, *, tm, tf):
    n, d = x.shape
    dff = w_down.shape[1]
    assert n % tm == 0 and dff % tf == 0
    nf = dff // tf
    ln_row = lambda i, f: (layer, 0, 0)
    return pl.pallas_call(
        _ffn_kernel,
        out_shape=jax.ShapeDtypeStruct((n, d), F32),
        grid=(n // tm, nf),
        in_specs=[pl.BlockSpec((tm, d), lambda i, f: (i, 0)),
                  pl.BlockSpec((None, d, tf), lambda i, f: (layer, 0, f)),
                  pl.BlockSpec((None, d, tf), lambda i, f: (layer, 0, nf + f)),
                  pl.BlockSpec((None, tf, d), lambda i, f: (layer, f, 0)),
                  pl.BlockSpec((None, 1, d), ln_row), pl.BlockSpec((None, 1, d), ln_row)],
        out_specs=pl.BlockSpec((tm, d), lambda i, f: (i, 0)),
        scratch_shapes=[pltpu.VMEM((tm, d), F32), pltpu.VMEM((tm, d), BF16)],
        compiler_params=_cparams(("parallel", "arbitrary")),
        name="ffn_ln",
    )(x, w_gate_up, w_gate_up, w_down, g.reshape(-1, 1, d), b.reshape(-1, 1, d))


def _diff_lambda(lam_ref, lam_init):
    l = lam_ref[...]
    a = jnp.sum(l[0:1] * l[1:2], axis=-1, keepdims=True)
    c = jnp.sum(l[2:3] * l[3:4], axis=-1, keepdims=True)
    return jnp.exp(a) - jnp.exp(c) + lam_init


def _rms_rows(o, g):
    return o * lax.rsqrt(jnp.mean(o * o, axis=-1, keepdims=True) + LN_EPS) * g


def _flash_kernel(qi_ref, ki_ref, *refs, mode, t, lam_init):
    if mode == "diff":
        q_ref, k_ref, v_ref, lam_ref, g_ref, o_ref, m_sc, l_sc, acc_sc = refs
    else:
        q_ref, k_ref, v_ref, bias_ref, o_ref, m_sc, l_sc, acc_sc = refs
    step = pl.program_id(1)
    qi = qi_ref[step]
    ki = ki_ref[step]

    @pl.when(ki == 0)
    def _():
        m_sc[...] = jnp.full_like(m_sc, NEG)
        l_sc[...] = jnp.zeros_like(l_sc)
        acc_sc[...] = jnp.zeros_like(acc_sc)

    def block_update(causal):
        if causal:
            row = lax.broadcasted_iota(jnp.int32, (2 * t, t), 0)
            col = lax.broadcasted_iota(jnp.int32, (2 * t, t), 1)
            allowed = col <= jnp.where(row >= t, row - t, row)

        def group(g, carry):
            q2 = q_ref[g].reshape(2 * t, LANES)
            s = jnp.dot(q2, k_ref[g], preferred_element_type=F32)
            if mode == "dsa":
                bias = bias_ref[...]
                s = s + jnp.concatenate([bias, bias], axis=0)
            if causal:
                s = jnp.where(allowed, s, NEG)
            m_prev = m_sc[g]
            m_next = jnp.maximum(m_prev, jnp.max(s, axis=-1, keepdims=True))
            p = jnp.exp2(s - jnp.concatenate([m_next] * (t // LANES), axis=-1))
            alpha = jnp.exp2(m_prev - m_next)
            l_sc[g] = alpha * l_sc[g] + jnp.sum(p, axis=-1, keepdims=True)
            acc_sc[g] = alpha * acc_sc[g] + jnp.dot(p.astype(BF16), v_ref[g], preferred_element_type=F32)
            m_sc[g] = m_next
            return carry

        lax.fori_loop(0, N_GROUPS, group, 0, unroll=2)

    if mode == "diff":
        pl.when(ki < qi)(lambda: block_update(False))
        pl.when(ki == qi)(lambda: block_update(True))
    else:
        block_update(False)

    @pl.when(ki == qi)
    def _():
        lane = lax.broadcasted_iota(jnp.int32, (1, LANES), 1)
        lo = lane < DIFF_HEAD_DIM
        if mode == "diff":
            lam = _diff_lambda(lam_ref, lam_init)
            gain = g_ref[...] * (1.0 - lam_init)
        for g in range(N_GROUPS):
            o_lo = acc_sc[g, :t, :] / l_sc[g, :t, :]
            o_hi = acc_sc[g, t:, :] / l_sc[g, t:, :]
            if mode == "diff":
                o_ref[:, g * LANES:(g + 1) * LANES] = _rms_rows(o_lo - lam * o_hi, gain)
            else:
                o_ref[:, g * LANES:(g + 1) * LANES] = jnp.where(lo, o_lo, o_hi)


def _flash_attention(q2, k_t, v_g, b, *, mode, t, lam_params=None, subln_g=None, lam_init=0.0, bias=None):
    n = q2.shape[2]
    s = n // b
    d = D_MODEL
    assert s % t == 0 and n == b * s
    nblk = s // t
    k4 = k_t.reshape(b, N_GROUPS, LANES, s)
    pairs = [(qb, kb) for qb in range(nblk) for kb in range(qb + 1)]
    qi_of = jnp.asarray([p[0] for p in pairs], dtype=jnp.int32)
    ki_of = jnp.asarray([p[1] for p in pairs], dtype=jnp.int32)
    qspec = pl.BlockSpec((N_GROUPS, 2, t, LANES), lambda bi, st, qo, ko: (0, 0, bi * nblk + qo[st], 0))
    kspec = pl.BlockSpec((None, N_GROUPS, LANES, t), lambda bi, st, qo, ko: (bi, 0, 0, ko[st]))
    vspec = pl.BlockSpec((N_GROUPS, t, LANES), lambda bi, st, qo, ko: (0, bi * nblk + ko[st], 0))
    in_specs = [qspec, kspec, vspec]
    args = [q2, k4, v_g]
    if mode == "diff":
        in_specs += [pl.BlockSpec((4, DIFF_HEAD_DIM), lambda bi, st, qo, ko: (0, 0)),
                     pl.BlockSpec((1, LANES), lambda bi, st, qo, ko: (0, 0))]
        args += [lam_params, subln_g.reshape(1, LANES)]
    else:
        assert bias.shape == (b, nblk, s, t)
        in_specs += [pl.BlockSpec((None, None, t, t), lambda bi, st, qo, ko: (bi, ko[st], qo[st], 0))]
        args += [bias]
    stat = pltpu.VMEM((N_GROUPS, 2 * t, LANES), F32)
    return pl.pallas_call(
        functools.partial(_flash_kernel, mode=mode, t=t, lam_init=lam_init),
        out_shape=jax.ShapeDtypeStruct((n, d), F32),
        grid_spec=pltpu.PrefetchScalarGridSpec(
            num_scalar_prefetch=2,
            grid=(b, len(pairs)),
            in_specs=in_specs,
            out_specs=pl.BlockSpec((t, d), lambda bi, st, qo, ko: (bi * nblk + qo[st], 0)),
            scratch_shapes=[stat, stat, stat]),
        compiler_params=_cparams(("parallel", "arbitrary")),
        name="flash_" + mode,
    )(qi_of, ki_of, *args)


def _decode_kernel(pt_ref, q_ref, *refs, mode, n_pp, n_pages, scale, lam_init):
    del pt_ref
    k_refs = refs[:n_pp]
    v_refs = refs[n_pp:2 * n_pp]
    knew_ref, vnew_ref = refs[2 * n_pp:2 * n_pp + 2]
    rest = refs[2 * n_pp + 2:]
    if mode == "diff":
        lam_ref, g_ref, o_ref, qm_sc, m_sc, l_sc, acc_sc, e_sc = rest
    else:
        bias_ref, o_ref, qm_sc, m_sc, l_sc, acc_sc = rest
    step = pl.program_id(1)
    n_sub = 2 * N_GROUPS
    row = lax.broadcasted_iota(jnp.int32, (n_sub, D_MODEL), 0)
    lane = lax.broadcasted_iota(jnp.int32, (n_sub, D_MODEL), 1)

    @pl.when(step == 0)
    def _():
        qm_sc[...] = jnp.where(lane // DIFF_HEAD_DIM == row, q_ref[...] * scale, 0.0)
        m_sc[...] = jnp.full_like(m_sc, NEG)
        l_sc[...] = jnp.zeros_like(l_sc)
        acc_sc[...] = jnp.zeros_like(acc_sc)
        if mode == "diff":
            tok = lax.broadcasted_iota(jnp.int32, (PAGE_SIZE, D_MODEL), 0)
            col = lax.broadcasted_iota(jnp.int32, (PAGE_SIZE, D_MODEL), 1)
            e_sc[...] = jnp.where(col // DIFF_HEADS == tok, 1.0, 0.0).astype(BF16)

    qm = qm_sc[...]
    s_parts = []
    for i in range(n_pp):
        s = _bdot(qm, k_refs[i][...])
        if mode == "dsa":
            s = s + bias_ref[pl.ds(step * n_pp + i, 1), :]
        s_parts.append(s)
    s_all = jnp.concatenate(s_parts, axis=-1)
    m_prev = m_sc[...]
    m_new = jnp.maximum(m_prev, jnp.max(s_all, axis=-1, keepdims=True))
    alpha = jnp.exp(m_prev - m_new)
    p_all = jnp.exp(s_all - m_new)
    l_sc[...] = alpha * l_sc[...] + jnp.sum(p_all, axis=-1, keepdims=True)
    pv = None
    for i in range(n_pp):
        p_i = p_all[:, i * PAGE_SIZE:(i + 1) * PAGE_SIZE]
        if mode == "diff":
            p_rows = jnp.where(lane % DIFF_HEADS == row // 2,
                               jnp.dot(p_i.astype(BF16), e_sc[...], preferred_element_type=F32), 0.0)
            part = _bdot(p_rows, v_refs[i][...])
        else:
            part = _bdot_nt(p_i, v_refs[i][...])
        pv = part if pv is None else pv + part
    acc_sc[...] = alpha * acc_sc[...] + pv
    m_sc[...] = m_new

    @pl.when(step == pl.num_programs(1) - 1)
    def _():
        s_new = jnp.sum(qm * knew_ref[...], axis=-1, keepdims=True)
        if mode == "dsa":
            s_new = s_new + bias_ref[n_pages:n_pages + 1, 0:1]
        m_prev = m_sc[...]
        m_fin = jnp.maximum(m_prev, s_new)
        alpha = jnp.exp(m_prev - m_fin)
        p_new = jnp.exp(s_new - m_fin)
        l_fin = alpha * l_sc[...] + p_new
        if mode == "diff":
            row_h = lax.broadcasted_iota(jnp.int32, (n_sub, LANES), 0) // 2
            v_rows = jnp.zeros((n_sub, LANES), F32)
            for h in range(DIFF_HEADS):
                v_rows = jnp.where(row_h == h, vnew_ref[:, h * LANES:(h + 1) * LANES], v_rows)
            o16 = (alpha * acc_sc[...] + p_new * v_rows) / l_fin
            lam = _diff_lambda(lam_ref, lam_init)
            gain = g_ref[...] * (1.0 - lam_init)
            for h in range(DIFF_HEADS):
                o_h = o16[2 * h:2 * h + 1, :] - lam * o16[2 * h + 1:2 * h + 2, :]
                o_ref[:, h * LANES:(h + 1) * LANES] = _rms_rows(o_h, gain)
        else:
            o16 = (alpha * acc_sc[...] + p_new * vnew_ref[...]) / l_fin
            o_ref[...] = jnp.sum(jnp.where(lane // DSA_HEAD_DIM == row, o16, 0.0), axis=0, keepdims=True)


def _decode_attention(q, cache_k, cache_v, page_ids, k_new, v_new, *, mode, n_pp,
                      lam_params=None, subln_g=None, lam_init=0.0, bias=None):
    bd, d = q.shape
    n_pages = page_ids.shape[1]
    assert n_pages % n_pp == 0 and cache_k.shape[1:] == (d, PAGE_SIZE) and cache_v.shape[1:] == (d, PAGE_SIZE)
    n_steps = n_pages // n_pp
    row_spec = pl.BlockSpec((None, 1, d), lambda b, s, pt: (b, 0, 0))

    def page_spec(i):
        return pl.BlockSpec((None, d, PAGE_SIZE), lambda b, s, pt: (pt[b * n_pages + s * n_pp + i], 0, 0))

    in_specs = [row_spec] + [page_spec(i) for i in range(n_pp)] * 2 + [row_spec, row_spec]
    args = [q.reshape(bd, 1, d)] + [cache_k] * n_pp + [cache_v] * n_pp + [k_new.reshape(bd, 1, d), v_new.reshape(bd, 1, d)]
    n_sub = 2 * N_GROUPS
    scratch = [pltpu.VMEM((n_sub, d), F32), pltpu.VMEM((n_sub, 1), F32), pltpu.VMEM((n_sub, 1), F32)]
    if mode == "diff":
        in_specs += [pl.BlockSpec((4, DIFF_HEAD_DIM), lambda b, s, pt: (0, 0)),
                     pl.BlockSpec((1, LANES), lambda b, s, pt: (0, 0))]
        args += [lam_params, subln_g.reshape(1, LANES)]
        scratch += [pltpu.VMEM((n_sub, LANES), F32), pltpu.VMEM((PAGE_SIZE, d), BF16)]
    else:
        in_specs += [pl.BlockSpec((None, n_pages + 1, PAGE_SIZE), lambda b, s, pt: (b, 0, 0))]
        args += [bias]
        scratch += [pltpu.VMEM((n_sub, d), F32)]
    out = pl.pallas_call(
        functools.partial(_decode_kernel, mode=mode, n_pp=n_pp, n_pages=n_pages,
                          scale=DIFF_HEAD_DIM ** -0.5, lam_init=lam_init),
        out_shape=jax.ShapeDtypeStruct((bd, 1, d), F32),
        grid_spec=pltpu.PrefetchScalarGridSpec(
            num_scalar_prefetch=1,
            grid=(bd, n_steps),
            in_specs=in_specs,
            out_specs=row_spec,
            scratch_shapes=scratch),
        compiler_params=_cparams(("parallel", "arbitrary")),
        name="decode_" + mode,
    )(page_ids.reshape(-1), *args)
    return out.reshape(bd, d)


def _sortable_keys(score):
    bits = lax.bitcast_convert_type(score, jnp.int32)
    key = jnp.where(bits < 0, bits ^ jnp.int32(0x7FFFFFFF), bits)
    return jnp.where(score == 0.0, 0, key)


def _kth_threshold(key_ref, nch, ch, ksel):
    def count_ge(cand):
        def body(c, acc):
            blk = key_ref[pl.ds(pl.multiple_of(c * ch, ch), ch), :]
            hit = jnp.where(blk >= cand, 1, 0).astype(jnp.int32)
            return acc + jnp.sum(hit.reshape(ch // SUBLANES, SUBLANES, LANES), axis=0)

        acc = lax.fori_loop(0, nch, body, jnp.zeros((SUBLANES, LANES), jnp.int32))
        return jnp.sum(acc, axis=0, keepdims=True)

    def bit_body(i, t):
        cand = t ^ lax.shift_left(jnp.int32(1), jnp.asarray(31 - i, dtype=jnp.int32))
        return jnp.where(count_ge(cand) >= ksel, cand, t)

    t = lax.fori_loop(0, 32, bit_body, jnp.full((1, LANES), INT_MIN, jnp.int32))
    n_above = count_ge(t + 1)
    need = ksel - n_above
    surplus = jnp.logical_and(t > INT_MIN, count_ge(t) - n_above > need)
    return t, need.astype(F32), surplus


def _tri_ones(n):
    row = lax.broadcasted_iota(jnp.int32, (n, n), 0)
    col = lax.broadcasted_iota(jnp.int32, (n, n), 1)
    return jnp.where(col <= row, 1.0, 0.0).astype(BF16)


def _select_chunk(key, t, need, carry, tri):
    eq = key == t
    pref = jnp.dot(tri, jnp.where(eq, 1.0, 0.0).astype(BF16), preferred_element_type=F32) + carry
    sel = jnp.logical_or(key > t, jnp.logical_and(eq, pref <= need))
    sel = jnp.logical_and(sel, key > INT_MIN)
    return sel, pref[key.shape[0] - 1:, :]


def _dsa_index_prompt_kernel(idx_ref, kd_ref, o_ref, key_sc, *, tq, ch, ksel):
    qi = pl.program_id(1)
    n_ch_total = o_ref.shape[0]
    nch = (qi * tq + tq + ch - 1) // ch
    qblk = idx_ref[...]
    lane = lax.broadcasted_iota(jnp.int32, (1, LANES), 1)
    lo = lane < IDX_DIM
    parts = []
    for h in range(IDX_HEADS):
        grp = qblk[:, (h // 2) * LANES:(h // 2 + 1) * LANES]
        parts.append(jnp.where(lo if h % 2 == 0 else jnp.logical_not(lo), grp, 0.0).astype(BF16))
    qstack = jnp.concatenate(parts, axis=0)
    w = qblk[:, 5 * LANES:6 * LANES] * (IDX_HEADS ** -0.5 * IDX_DIM ** -0.5)
    w_cols = [w[:, h:h + 1] for h in range(IDX_HEADS)]
    qpos = qi * tq + lane

    def score_body(c, _):
        base = pl.multiple_of(c * ch, ch)
        r = _bdot(qstack, kd_ref[:, pl.ds(base, ch)])
        score = jnp.zeros((tq, ch), F32)
        for h in range(IDX_HEADS):
            score = score + w_cols[h] * jnp.maximum(r[h * tq:(h + 1) * tq, :], 0.0)
        kpos = base + lax.broadcasted_iota(jnp.int32, (ch, tq), 0)
        key_sc[pl.ds(base, ch), :] = jnp.where(kpos <= qpos, _sortable_keys(jnp.transpose(score)), INT_MIN)
        return 0

    lax.fori_loop(0, nch, score_body, 0)
    t, need, surplus = _kth_threshold(key_sc, nch, ch, ksel)
    any_surplus = jnp.max(jnp.where(surplus, 1, 0)) > 0

    @pl.when(any_surplus)
    def _():
        tri = _tri_ones(ch)

        def out_body(c, carry):
            base = pl.multiple_of(c * ch, ch)
            sel, carry = _select_chunk(key_sc[pl.ds(base, ch), :], t, need, carry, tri)
            o_ref[c] = jnp.transpose(jnp.where(sel, 0.0, NEG))
            return carry

        lax.fori_loop(0, nch, out_body, jnp.zeros((1, LANES), F32))

    @pl.when(jnp.logical_not(any_surplus))
    def _():
        def out_body(c, _):
            key = key_sc[pl.ds(pl.multiple_of(c * ch, ch), ch), :]
            sel = jnp.logical_and(key >= t, key > INT_MIN)
            o_ref[c] = jnp.transpose(jnp.where(sel, 0.0, NEG))
            return 0

        lax.fori_loop(0, nch, out_body, 0)

    def fill_body(c, _):
        o_ref[c] = jnp.full((tq, ch), NEG, F32)
        return 0

    lax.fori_loop(nch, n_ch_total, fill_body, 0)


def _dsa_index_prompt(idx, kd_t, *, ch):
    b, s, w = idx.shape
    tq = LANES
    assert s % ch == 0 and ch % tq == 0
    ksel = min(IDX_TOPK_MAX, s // 4)
    return pl.pallas_call(
        functools.partial(_dsa_index_prompt_kernel, tq=tq, ch=ch, ksel=ksel),
        out_shape=jax.ShapeDtypeStruct((b, s // ch, s, ch), F32),
        grid=(b, s // tq),
        in_specs=[pl.BlockSpec((None, tq, w), lambda bi, qi: (bi, qi, 0)),
                  pl.BlockSpec((None, LANES, s), lambda bi, qi: (bi, 0, 0))],
        out_specs=pl.BlockSpec((None, s // ch, tq, ch), lambda bi, qi: (bi, 0, qi, 0)),
        scratch_shapes=[pltpu.VMEM((s, LANES), jnp.int32)],
        compiler_params=_cparams(("parallel", "arbitrary")),
        name="dsa_index_prompt",
    )(idx, kd_t)


def _dsa_score_sample_kernel(pt_ref, q_ref, w_ref, knew_ref, *refs, n_pages):
    del pt_ref
    page_refs = refs[:n_pages]
    o_ref = refs[n_pages]
    q = q_ref[...]
    w = w_ref[...] * (IDX_HEADS ** -0.5 * IDX_DIM ** -0.5)
    for p in range(n_pages):
        s = _bdot(q, page_refs[p][...])
        o_ref[p:p + 1, :] = jnp.sum(w * jnp.maximum(s, 0.0), axis=0, keepdims=True)
    s_new = jnp.sum(q * knew_ref[...], axis=-1, keepdims=True)
    sc_new = jnp.sum(w * jnp.maximum(s_new, 0.0), axis=0, keepdims=True)
    lane = lax.broadcasted_iota(jnp.int32, (1, PAGE_SIZE), 1)
    o_ref[n_pages:n_pages + 1, :] = jnp.where(lane == 0, sc_new, NEG)


def _dsa_score_sample(qi, wi, ki_new, cache_kidx, page_ids):
    bd = qi.shape[0]
    n_pages = page_ids.shape[1]
    assert cache_kidx.shape[1:] == (IDX_DIM, PAGE_SIZE)

    def page_spec(p):
        return pl.BlockSpec((None, IDX_DIM, PAGE_SIZE), lambda b, pt: (pt[b * n_pages + p], 0, 0))

    return pl.pallas_call(
        functools.partial(_dsa_score_sample_kernel, n_pages=n_pages),
        out_shape=jax.ShapeDtypeStruct((bd, n_pages + 1, PAGE_SIZE), F32),
        grid_spec=pltpu.PrefetchScalarGridSpec(
            num_scalar_prefetch=1,
            grid=(bd,),
            in_specs=[pl.BlockSpec((None, IDX_HEADS, IDX_DIM), lambda b, pt: (b, 0, 0)),
                      pl.BlockSpec((None, IDX_HEADS, 1), lambda b, pt: (b, 0, 0)),
                      pl.BlockSpec((None, 1, IDX_DIM), lambda b, pt: (b, 0, 0))]
                     + [page_spec(p) for p in range(n_pages)],
            out_specs=pl.BlockSpec((None, n_pages + 1, PAGE_SIZE), lambda b, pt: (b, 0, 0))),
        compiler_params=_cparams(("parallel",)),
        name="dsa_score_sample",
    )(page_ids.reshape(-1), qi, wi, ki_new, *([cache_kidx] * n_pages))


def _select_bias_kernel(s_ref, o_ref, key_sc, *, n_keys, ch, ksel):
    n_rows = s_ref.shape[0]
    nch = n_rows // ch
    kpos = lax.broadcasted_iota(jnp.int32, (n_rows, LANES), 0)
    key_sc[...] = jnp.where(kpos < n_keys, _sortable_keys(s_ref[...]), INT_MIN)
    t, need, _ = _kth_threshold(key_sc, nch, ch, ksel)
    tri = _tri_ones(ch)
    carry = jnp.zeros((1, LANES), F32)
    for c in range(nch):
        sel, carry = _select_chunk(key_sc[c * ch:(c + 1) * ch, :], t, need, carry, tri)
        o_ref[c * ch:(c + 1) * ch, :] = jnp.where(sel, 0.0, NEG)


def _select_bias(score_t, n_keys, ksel):
    n_rows, n_q = score_t.shape
    assert n_q == LANES and n_rows % LANES == 0
    return pl.pallas_call(
        functools.partial(_select_bias_kernel, n_keys=n_keys, ch=LANES, ksel=ksel),
        out_shape=jax.ShapeDtypeStruct((n_rows, LANES), F32),
        scratch_shapes=[pltpu.VMEM((n_rows, LANES), jnp.int32)],
        compiler_params=pltpu.CompilerParams(vmem_limit_bytes=VMEM_LIMIT),
        name="dsa_select_sample",
    )(score_t)


def _log_sigmoid(x):
    return -(jnp.maximum(-x, 0.0) + jnp.log1p(jnp.exp(-jnp.abs(x))))


def _gla_gate(glow, wgu, gate_b):
    return _log_sigmoid(_bdot(glow, wgu) + gate_b) / GLA_TAU


def _column(row_vec, eye):
    n = eye.shape[0]
    return jnp.sum(jnp.where(eye, jnp.broadcast_to(row_vec, (n, n)), 0.0), axis=-1, keepdims=True)


def _level_reference(b_ref, half, c):
    sub = lax.broadcasted_iota(jnp.int32, (SUBLANES, LANES), 0)
    slabs = []
    for t0 in range(0, c, SUBLANES):
        cur = None
        for t in range(t0, t0 + SUBLANES, min(SUBLANES, 2 * half)):
            r = (t // (2 * half)) * (2 * half) + half - 1
            bc = jnp.broadcast_to(b_ref[r:r + 1, :], (SUBLANES, LANES))
            cur = bc if cur is None else jnp.where(sub >= (t - t0), bc, cur)
        slabs.append(cur)
    return jnp.concatenate(slabs, axis=0)


def _gla_prompt_kernel(q_ref, k_ref, v_ref, r_ref, gl_ref, wgu_ref, gb_ref, ng_ref, o_ref, st_ref,
                       s_sc, b_sc, *, c, hps):
    ci = pl.program_id(2)
    hk, hv = GLA_HEAD_K, GLA_HEAD_V

    @pl.when(ci == 0)
    def _():
        s_sc[...] = jnp.zeros_like(s_sc)

    g_all = _gla_gate(gl_ref[...], wgu_ref[...], gb_ref[...])
    row = lax.broadcasted_iota(jnp.int32, (c, c), 0)
    col = lax.broadcasted_iota(jnp.int32, (c, c), 1)
    tri = jnp.where(col <= row, 1.0, 0.0).astype(BF16)
    eye = row == col
    for hh in range(hps):
        ksl = slice(hh * hk, (hh + 1) * hk)
        vsl = slice(hh * hv, (hh + 1) * hv)
        g = g_all[:, ksl]
        g1 = g.astype(BF16)
        rem = g - g1.astype(F32)
        g2 = rem.astype(BF16)
        g3 = (rem - g2.astype(F32)).astype(BF16)
        bcum = (jnp.dot(tri, g1, preferred_element_type=F32) + jnp.dot(tri, g2, preferred_element_type=F32)
                + jnp.dot(tri, g3, preferred_element_type=F32))
        b_ref = b_sc.at[hh]
        b_ref[...] = bcum
        q = q_ref[:, ksl] * (GLA_HEAD_K ** -0.5)
        k = k_ref[:, ksl]
        v = v_ref[:, vsl]
        att = jnp.where(eye, jnp.sum(q * k, axis=-1, keepdims=True), 0.0)
        half = c // 2
        while half >= 1:
            ref = _level_reference(b_ref, half, c)
            qt = q * jnp.exp(jnp.minimum(bcum - ref, 0.0))
            kt = k * jnp.exp(jnp.minimum(ref - bcum, 0.0))
            valid = jnp.logical_and(row // (2 * half) == col // (2 * half),
                                    jnp.logical_and(row % (2 * half) >= half, col % (2 * half) < half))
            att = jnp.where(valid, _bdot_nt(qt, kt), att)
            half //= 2
        s0 = s_sc[hh]
        o = _bdot(q * jnp.exp(bcum), s0) + _bdot(att, v)
        o = _rms_rows(o, ng_ref[...])
        rr = r_ref[:, vsl]
        o_ref[:, vsl] = o * (rr * _sigmoid(rr))
        b_last = b_ref[c - 1:c, :]
        khat = k * jnp.exp(b_last - bcum)
        s_new = _column(jnp.exp(b_last), eye) * s0 + _bdot_tn(khat, v)
        s_sc[hh] = s_new

    @pl.when(ci == pl.num_programs(2) - 1)
    def _():
        st_ref[...] = s_sc[...]


def _gla_prompt(qk, v, r, glow, wgu, gate_b, norm_g, *, c, hps):
    b, s, _ = qk.shape
    assert s % c == 0 and c == GLA_HEAD_K and GLA_HEADS % hps == 0
    hk, hv = GLA_HEAD_K, GLA_HEAD_V
    nhb = GLA_HEADS // hps
    return pl.pallas_call(
        functools.partial(_gla_prompt_kernel, c=c, hps=hps),
        out_shape=(jax.ShapeDtypeStruct((b, s, GLA_DV), F32),
                   jax.ShapeDtypeStruct((b, GLA_HEADS, hk, hv), F32)),
        grid=(b, nhb, s // c),
        in_specs=[pl.BlockSpec((None, c, hps * hk), lambda bi, h, ci: (bi, ci, h)),
                  pl.BlockSpec((None, c, hps * hk), lambda bi, h, ci: (bi, ci, nhb + h)),
                  pl.BlockSpec((None, c, hps * hv), lambda bi, h, ci: (bi, ci, h)),
                  pl.BlockSpec((None, c, hps * hv), lambda bi, h, ci: (bi, ci, h)),
                  pl.BlockSpec((None, c, LANES), lambda bi, h, ci: (bi, ci, 0)),
                  pl.BlockSpec((LANES, hps * hk), lambda bi, h, ci: (0, h)),
                  pl.BlockSpec((1, hps * hk), lambda bi, h, ci: (0, h)),
                  pl.BlockSpec((1, hv), lambda bi, h, ci: (0, 0))],
        out_specs=(pl.BlockSpec((None, c, hps * hv), lambda bi, h, ci: (bi, ci, h)),
                   pl.BlockSpec((None, hps, hk, hv), lambda bi, h, ci: (bi, h, 0, 0))),
        scratch_shapes=[pltpu.VMEM((hps, hk, hv), F32), pltpu.VMEM((hps, c, hk), F32)],
        compiler_params=_cparams(("parallel", "parallel", "arbitrary")),
        name="gla_prompt",
    )(qk, qk, v, r, glow, wgu, gate_b.reshape(1, GLA_DK), norm_g.reshape(1, hv))


def _gla_sample_kernel(qk_ref, v_ref, r_ref, gl_ref, wgu_ref, gb_ref, ng_ref, st_ref, o_ref, nst_ref):
    hk, hv = GLA_HEAD_K, GLA_HEAD_V
    glow = jnp.broadcast_to(gl_ref[...], (SUBLANES, LANES))
    g = _gla_gate(glow, wgu_ref[...], gb_ref[...])[0:1, :]
    eye = (lax.broadcasted_iota(jnp.int32, (hk, hk), 0) == lax.broadcasted_iota(jnp.int32, (hk, hk), 1))
    for h in range(GLA_HEADS):
        ksl = slice(h * hk, (h + 1) * hk)
        vsl = slice(h * hv, (h + 1) * hv)
        q_col = _column(qk_ref[:, ksl] * (GLA_HEAD_K ** -0.5), eye)
        k_col = _column(qk_ref[:, GLA_DK + h * hk:GLA_DK + (h + 1) * hk], eye)
        a_col = _column(jnp.exp(g[:, ksl]), eye)
        s_new = a_col * st_ref[h] + k_col * v_ref[:, vsl]
        nst_ref[h] = s_new
        o = jnp.sum(q_col * s_new, axis=0, keepdims=True)
        rr = r_ref[:, vsl]
        o_ref[:, vsl] = _rms_rows(o, ng_ref[...]) * (rr * _sigmoid(rr))


def _gla_sample(qk, v, r, glow, wgu, gate_b, norm_g, state, state_row0):
    bd = qk.shape[0]
    hk, hv = GLA_HEAD_K, GLA_HEAD_V
    rowspec = lambda w: pl.BlockSpec((None, 1, w), lambda b: (b, 0, 0))
    fixed2 = lambda b: (0, 0)
    o, nst = pl.pallas_call(
        _gla_sample_kernel,
        out_shape=(jax.ShapeDtypeStruct((bd, 1, GLA_DV), F32),
                   jax.ShapeDtypeStruct((bd, GLA_HEADS, hk, hv), F32)),
        grid=(bd,),
        in_specs=[rowspec(2 * GLA_DK), rowspec(GLA_DV), rowspec(GLA_DV), rowspec(LANES),
                  pl.BlockSpec((LANES, GLA_DK), fixed2), pl.BlockSpec((1, GLA_DK), fixed2),
                  pl.BlockSpec((1, hv), fixed2),
                  pl.BlockSpec((None, GLA_HEADS, hk, hv), lambda b: (state_row0 + b, 0, 0, 0))],
        out_specs=(rowspec(GLA_DV), pl.BlockSpec((None, GLA_HEADS, hk, hv), lambda b: (b, 0, 0, 0))),
        compiler_params=_cparams(("parallel",)),
        name="gla_sample",
    )(qk.reshape(bd, 1, -1), v.reshape(bd, 1, -1), r.reshape(bd, 1, -1), glow.reshape(bd, 1, -1),
      wgu, gate_b.reshape(1, GLA_DK), norm_g.reshape(1, hv), state)
    return o.reshape(bd, GLA_DV), nst


def _row_tile(n, cap):
    t = cap
    while n % t:
        t //= 2
    return t


def _project(x, w, layer, col0, width, rope=None, n_rope=None, tn=None):
    tm = _row_tile(x.shape[0], 1024)
    if tn is None:
        tn = 512 if width % 512 == 0 else LANES
    return _linear(x, w, layer, col0, width, tm=tm, tn=tn, rope=rope, n_rope=n_rope)


def _pages_per_step(n_pages):
    for n in (16, 4, 2):
        if n_pages % n == 0:
            return n
    return 1


def _project_t(x, w_t, nb, rope=None, with_bf16=False, w_extra_t=None):
    s = x.shape[0] // nb
    return _linear_t(x, w_t, nb, tc=min(512, w_t.shape[0]), tt=_row_tile(s, 512), rope=rope, with_bf16=with_bf16,
                     w_extra_t=w_extra_t)


def _heads_last(x_t, n_heads):
    b, c, s = x_t.shape
    return x_t.reshape(b, n_heads, c // n_heads, s).transpose(0, 3, 1, 2)


def _diff_mixer(xp, xs, dims, cache_k, cache_v, page_ids, w_in, j, lam_params, subln_g, lam_init, ropes):
    b, s, bd = dims
    d = D_MODEL
    rope_p, rope_s, rope_pt, rope_st = ropes
    w_k_t = w_in[j][:, d:2 * d].T
    tm = _row_tile(b * s, 512)
    q2_p = _q_pairs(xp, w_in, j, rope_p, tm=_row_tile(s, 512))
    k_pt, k_pt16 = _project_t(xp, w_k_t, b, rope_pt, with_bf16=True)
    v_p, v_pg = _linear_heads(xp, w_in, j, 2 * d, DIFF_HEADS, tm=tm, groups=True)
    o_p = _flash_attention(q2_p, k_pt16, v_pg, b, mode="diff", t=_row_tile(s, 512),
                           lam_params=lam_params, subln_g=subln_g, lam_init=lam_init)
    q_s = _project(xs, w_in, j, 0, d, rope_s)
    k_s = _project(xs, w_in, j, d, d, rope_s)
    v_s = _project(xs, w_in, j, 2 * d, d)
    k_st = _project_t(xs, w_k_t, 1, rope_st)
    v_sh = _linear_heads(xs, w_in, j, 2 * d, DIFF_HEADS, tm=_row_tile(bd, 512))
    o_s = _decode_attention(q_s, cache_k, cache_v, page_ids, k_s, v_s, mode="diff",
                            n_pp=_pages_per_step(page_ids.shape[1]),
                            lam_params=lam_params, subln_g=subln_g, lam_init=lam_init)
    k_leaf_p = _heads_last(k_pt, 2 * DIFF_HEADS)
    v_leaf_p = v_p.reshape(b, s, DIFF_HEADS, LANES)
    k_leaf_s = _heads_last(k_st, 2 * DIFF_HEADS).reshape(bd, 1, 2 * DIFF_HEADS, DIFF_HEAD_DIM)
    v_leaf_s = v_sh.reshape(bd, 1, DIFF_HEADS, LANES)
    return o_p.reshape(b * s, d), o_s, k_leaf_p, v_leaf_p, k_leaf_s, v_leaf_s


def _dsa_mixer(xp, xs, dims, cache_k, cache_v, cache_kidx, page_ids, w_in, j, ropes, past_len):
    b, s, bd = dims
    d = D_MODEL
    rope_p, rope_s, rope_pt, rope_st = ropes
    w = w_in[j]
    c_qi, c_ki, c_wi = 3 * d, 3 * d + IDX_HEADS * IDX_DIM, 3 * d + IDX_HEADS * IDX_DIM + IDX_DIM
    w_idx = jnp.concatenate([w[:, c_qi:c_ki], w[:, c_ki:c_wi], w[:, c_ki:c_wi], w[:, c_wi:c_wi + IDX_HEADS],
                             jnp.zeros((d, LANES - IDX_HEADS), F32)], axis=1)[None]
    idx_w = w_idx.shape[2]
    w_k_t = w[:, d:2 * d].T
    w_v_t = w[:, 2 * d:3 * d].T
    w_ki2_t = jnp.concatenate([w[:, c_ki:c_wi], w[:, c_ki:c_wi]], axis=1).T

    q2_p = _q_pairs(xp, w_in, j, rope_p, tm=_row_tile(s, 512))
    k_pt, k_pt16, kd_pt = _project_t(xp, w_k_t, b, rope_pt, with_bf16=True, w_extra_t=w_ki2_t)
    v_pt = _project_t(xp, w_v_t, b)
    v_pg = _linear_heads(xp, w_in, j, 2 * d, N_GROUPS, tm=_row_tile(b * s, 512), leaf=False, groups=True)
    idx_p = _project(xp, w_idx, 0, 0, idx_w, rope_p, n_rope=5, tn=idx_w)
    t = _row_tile(s, 512)
    bias_p = _dsa_index_prompt(idx_p.reshape(b, s, idx_w), kd_pt, ch=t)
    o_p = _flash_attention(q2_p, k_pt16, v_pg, b, mode="dsa", t=t, bias=bias_p)

    q_s = _project(xs, w_in, j, 0, d, rope_s)
    k_s = _project(xs, w_in, j, d, d, rope_s)
    v_s = _project(xs, w_in, j, 2 * d, d)
    k_st, kd_st = _project_t(xs, w_k_t, 1, rope_st, w_extra_t=w_ki2_t)
    v_st = _project_t(xs, w_v_t, 1)
    idx_s = _project(xs, w_idx, 0, 0, idx_w, rope_s, n_rope=5, tn=idx_w)
    qi_s = idx_s[:, :IDX_HEADS * IDX_DIM].reshape(bd, IDX_HEADS, IDX_DIM)
    ki_s = idx_s[:, IDX_HEADS * IDX_DIM:IDX_HEADS * IDX_DIM + IDX_DIM]
    wi_s = idx_s[:, 5 * LANES:5 * LANES + IDX_HEADS].reshape(bd, IDX_HEADS, 1)
    n_pages = page_ids.shape[1]
    scores = _dsa_score_sample(qi_s, wi_s, ki_s.reshape(bd, 1, IDX_DIM), cache_kidx, page_ids)
    n_keys = past_len + 1
    bias_t = _select_bias(scores.reshape(bd, (n_pages + 1) * PAGE_SIZE).T, n_keys, min(IDX_TOPK_MAX, n_keys // 4))
    bias_s = bias_t.T.reshape(bd, n_pages + 1, PAGE_SIZE)
    o_s = _decode_attention(q_s, cache_k, cache_v, page_ids, k_s, v_s, mode="dsa",
                            n_pp=_pages_per_step(n_pages), bias=bias_s)
    leaves_p = (_heads_last(k_pt, DSA_HEADS), _heads_last(v_pt, DSA_HEADS), kd_pt[:, :IDX_DIM, :].transpose(0, 2, 1))
    leaves_s = (_heads_last(k_st, DSA_HEADS).reshape(bd, 1, DSA_HEADS, DSA_HEAD_DIM),
                _heads_last(v_st, DSA_HEADS).reshape(bd, 1, DSA_HEADS, DSA_HEAD_DIM),
                kd_st[0, :IDX_DIM, :].T.reshape(bd, 1, IDX_DIM))
    return (o_p.reshape(b * s, d), o_s) + leaves_p + leaves_s


def _gla_mixer(xp, xs, dims, state, j, w_in, w_gate_up, gate_b, norm_g):
    b, s, bd = dims
    d = D_MODEL
    c_g = 2 * GLA_DK + 2 * GLA_DV
    w_g = jnp.pad(w_in[j][:, c_g:c_g + GLA_GATE_RANK], ((0, 0), (0, LANES - GLA_GATE_RANK)))[None]
    wgu = jnp.pad(w_gate_up[j], ((0, LANES - GLA_GATE_RANK), (0, 0)))
    outs = []
    for x in (xp, xs):
        qk = _project(x, w_in, j, 0, 2 * GLA_DK)
        v = _project(x, w_in, j, 2 * GLA_DK, GLA_DV)
        r = _project(x, w_in, j, 2 * GLA_DK + GLA_DV, GLA_DV)
        glow = _project(x, w_g, 0, 0, LANES, tn=LANES)
        outs.append((qk, v, r, glow))
    qk, v, r, glow = outs[0]
    o_p, st_p = _gla_prompt(qk.reshape(b, s, -1), v.reshape(b, s, -1), r.reshape(b, s, -1), glow.reshape(b, s, -1),
                            wgu, gate_b[j], norm_g[j], c=GLA_HEAD_K, hps=GLA_HEADS)
    qk, v, r, glow = outs[1]
    n_state = state.shape[1]
    o_s, st_s = _gla_sample(qk, v, r, glow, wgu, gate_b[j], norm_g[j],
                            state.reshape((-1,) + state.shape[2:]), j * n_state)
    return o_p.reshape(b * s, d), o_s, st_p, st_s


def kernel(x_prompt, x_sample, cache_diff_k, cache_diff_v, cache_dsa_k, cache_dsa_v, cache_dsa_kidx, state_gla,
           page_table, ln_mix_g, ln_mix_b, ln_ffn_g, ln_ffn_b, ffn_w_gate_up, ffn_w_down, diff_w_in, diff_lambda,
           diff_subln_g, diff_w_out, dsa_w_in, dsa_w_out, gla_w_in, gla_w_gate_up, gla_gate_b, gla_norm_g,
           gla_w_out):
    b, s, d = x_prompt.shape
    bd, s_d, _ = x_sample.shape
    assert s_d == 1 and d == D_MODEL
    dims = (b, s, bd)
    n_pool, page = cache_diff_k.shape[1], cache_diff_k.shape[2]
    past_len = page_table.shape[1] * page
    xp = x_prompt.reshape(b * s, d)
    xs = x_sample.reshape(bd, d)
    pos_p = jnp.arange(s, dtype=jnp.int32)
    pos_s = jnp.full((bd,), past_len, dtype=jnp.int32)
    ropes = (_rope_tables(pos_p), _rope_tables(pos_s), _rope_tables_t(pos_p), _rope_tables_t(pos_s))

    def feature_major(c):
        perm = (0, 1) + tuple(range(3, c.ndim)) + (2,)
        return jnp.transpose(c, perm).reshape(c.shape[0] * c.shape[1], -1, page)

    cdk, csk, csv, csi = (feature_major(c) for c in (cache_diff_k, cache_dsa_k, cache_dsa_v, cache_dsa_kidx))
    cdv = cache_diff_v.reshape(cache_diff_v.shape[0] * n_pool, page * DIFF_HEADS, LANES)
    tm_p = _row_tile(b * s, 512)
    tm_s = _row_tile(bd, 512)
    tf = 256
    dk_p, dv_p, dk_s, dv_s = [], [], [], []
    sk_p, sv_p, si_p, sk_s, sv_s, si_s = [], [], [], [], [], []
    gs_p, gs_s = [], []
    for i in range(DEPTH):
        kind, j = i % 3, i // 3
        page_ids = page_table + j * n_pool
        if kind == 0:
            lam_init = 0.8 - 0.6 * math.exp(-0.3 * i)
            o_p, o_s, k_p, v_p, k_s, v_s = _diff_mixer(xp, xs, dims, cdk, cdv, page_ids, diff_w_in, j,
                                                       diff_lambda[j], diff_subln_g[j], lam_init, ropes)
            dk_p.append(k_p)
            dv_p.append(v_p)
            dk_s.append(k_s)
            dv_s.append(v_s)
            w_out = diff_w_out
        elif kind == 1:
            o_p, o_s, k_p, v_p, ki_p, k_s, v_s, ki_s = _dsa_mixer(xp, xs, dims, csk, csv, csi, page_ids, dsa_w_in, j,
                                                                  ropes, past_len)
            sk_p.append(k_p)
            sv_p.append(v_p)
            si_p.append(ki_p)
            sk_s.append(k_s)
            sv_s.append(v_s)
            si_s.append(ki_s)
            w_out = dsa_w_out
        else:
            o_p, o_s, st_p, st_s = _gla_mixer(xp, xs, dims, state_gla, j, gla_w_in, gla_w_gate_up, gla_gate_b,
                                              gla_norm_g)
            gs_p.append(st_p)
            gs_s.append(st_s)
            w_out = gla_w_out
        xp = _linear_res_ln(o_p, w_out, j, xp, ln_mix_g, ln_mix_b, i, tm=tm_p)
        xs = _linear_res_ln(o_s, w_out, j, xs, ln_mix_g, ln_mix_b, i, tm=tm_s)
        xp = _ffn_ln(xp, ffn_w_gate_up, ffn_w_down, ln_ffn_g, ln_ffn_b, i, tm=_row_tile(b * s, 1024), tf=tf)
        xs = _ffn_ln(xs, ffn_w_gate_up, ffn_w_down, ln_ffn_g, ln_ffn_b, i, tm=tm_s, tf=tf)
    return (xp.reshape(b, s, d), xs.reshape(bd, 1, d),
            jnp.stack(dk_p), jnp.stack(dv_p), jnp.stack(dk_s), jnp.stack(dv_s),
            jnp.stack(sk_p), jnp.stack(sv_p), jnp.stack(si_p),
            jnp.stack(sk_s), jnp.stack(sv_s), jnp.stack(si_s),
            jnp.stack(gs_p), jnp.stack(gs_s))
```

```python
import functools
import math

import jax
import jax.numpy as jnp
from jax import lax
from jax.experimental import pallas as pl
from jax.experimental.pallas import tpu as pltpu

F32 = jnp.float32
BF16 = jnp.bfloat16

D_MODEL = 1024
DEPTH = 4
PAGE_SIZE = 128
DIFF_HEADS = 8
DIFF_HEAD_DIM = 64
DSA_HEADS = 16
DSA_HEAD_DIM = 64
IDX_HEADS = 8
IDX_DIM = 64
IDX_TOPK_MAX = 256
GLA_HEADS = 4
GLA_DK = D_MODEL // 2
GLA_DV = D_MODEL
GLA_HEAD_K = GLA_DK // GLA_HEADS
GLA_HEAD_V = GLA_DV // GLA_HEADS
GLA_GATE_RANK = 16
GLA_TAU = 16.0
D_FF = (((8 * D_MODEL + 2) // 3 + 255) // 256) * 256
ROPE_THETA = 10000.0
LN_EPS = 1e-5
DEEPNORM_ALPHA = (2.0 * DEPTH) ** 0.25

LANES = 128
SUBLANES = 8
VMEM_LIMIT = 56 * 1024 * 1024
NEG = -1e30
INT_MIN = -2 ** 31

N_GROUPS = D_MODEL // LANES


def _cparams(sem):
    return pltpu.CompilerParams(dimension_semantics=sem, vmem_limit_bytes=VMEM_LIMIT)


def _bdot(a, b):
    return jnp.dot(a.astype(BF16), b.astype(BF16), preferred_element_type=F32)


def _bdot_nt(a, b):
    return lax.dot_general(a.astype(BF16), b.astype(BF16), (((1,), (1,)), ((), ())),
                           preferred_element_type=F32)


def _bdot_tn(a, b):
    return lax.dot_general(a.astype(BF16), b.astype(BF16), (((0,), (0,)), ((), ())),
                           preferred_element_type=F32)


def _layer_norm_rows(z, g, b):
    mu = jnp.mean(z, axis=-1, keepdims=True)
    zc = z - mu
    var = jnp.mean(zc * zc, axis=-1, keepdims=True)
    return zc * lax.rsqrt(var + LN_EPS) * g + b


def _sigmoid(x):
    return 1.0 / (1.0 + jnp.exp(-x))


def _rope_tables(pos):
    d = DIFF_HEAD_DIM
    inv = ROPE_THETA ** (-jnp.arange(0, d, 2, dtype=F32) / d)
    ang = pos.astype(F32)[:, None] * inv[None, :]
    cos = jnp.cos(ang)
    sin = jnp.sin(ang)
    return (jnp.concatenate([cos, cos, cos, cos], axis=-1),
            jnp.concatenate([-sin, sin, -sin, sin], axis=-1))


def _rope_apply(y, cos, sin, first_half):
    partner = jnp.where(first_half, pltpu.roll(y, 96, 1), pltpu.roll(y, 32, 1))
    return y * cos + partner * sin


def _linear_kernel(x_ref, w_ref, *rest, n_rope, single_tile):
    if n_rope:
        cos_ref, sin_ref, o_ref = rest
    else:
        (o_ref,) = rest
    acc = _bdot(x_ref[...], w_ref[...])
    tm, tn = acc.shape
    if not n_rope:
        o_ref[...] = acc
        return

    def roped(n_chunks):
        cos = cos_ref[...]
        sin = sin_ref[...]
        lane = lax.broadcasted_iota(jnp.int32, (tm, LANES), 1)
        first_half = (lane % DIFF_HEAD_DIM) < (DIFF_HEAD_DIM // 2)
        for c in range(tn // LANES):
            sl = slice(c * LANES, (c + 1) * LANES)
            o_ref[:, sl] = _rope_apply(acc[:, sl], cos, sin, first_half) if c < n_chunks else acc[:, sl]

    if single_tile:
        roped(n_rope)
        return
    j = pl.program_id(1)
    pl.when(j < n_rope)(lambda: roped(tn // LANES))

    @pl.when(j >= n_rope)
    def _():
        o_ref[...] = acc


def _linear(x, w, layer, col0, width, *, tm, tn, rope=None, n_rope=None):
    n, k = x.shape
    assert n % tm == 0 and width % tn == 0 and col0 % tn == 0
    nj = width // tn
    if rope is None:
        n_rope = 0
    elif n_rope is None:
        n_rope = nj if nj > 1 else tn // LANES
    in_specs = [pl.BlockSpec((tm, k), lambda i, j: (i, 0)),
                pl.BlockSpec((None, k, tn), lambda i, j: (layer, 0, col0 // tn + j))]
    args = [x, w]
    if n_rope:
        p_blocks = rope[0].shape[0] // tm
        assert rope[0].shape[0] % tm == 0
        spec = pl.BlockSpec((tm, LANES), lambda i, j: (i % p_blocks, 0))
        in_specs += [spec, spec]
        args += [rope[0], rope[1]]
    return pl.pallas_call(
        functools.partial(_linear_kernel, n_rope=n_rope, single_tile=nj == 1),
        out_shape=jax.ShapeDtypeStruct((n, width), F32),
        grid=(n // tm, nj),
        in_specs=in_specs,
        out_specs=pl.BlockSpec((tm, tn), lambda i, j: (i, j)),
        compiler_params=_cparams(("parallel", "arbitrary")),
        name="linear",
    )(*args)


def _rope_tables_t(pos):
    cos, sin = _rope_tables(pos)
    return cos.T, sin.T


def _linear_t_kernel(w_ref, x_ref, *rest, rope, with_bf16, extra, aliased):
    rest = list(rest)
    we_ref = rest.pop(0) if extra else None
    cos_ref, sin_ref = (rest.pop(0), rest.pop(0)) if rope else (None, None)
    if aliased:
        rest.pop(0)
    o_ref = rest.pop(0)
    o16_ref = rest.pop(0) if with_bf16 else None
    oe_ref = rest.pop(0) if extra else None
    tt = x_ref.shape[0]
    xb = x_ref[...].astype(BF16)

    def roped(y):
        row = lax.broadcasted_iota(jnp.int32, (LANES, tt), 0)
        first_half = (row % DIFF_HEAD_DIM) < (DIFF_HEAD_DIM // 2)
        partner = jnp.where(first_half, pltpu.roll(y, LANES - DIFF_HEAD_DIM // 2, 0),
                            pltpu.roll(y, DIFF_HEAD_DIM // 2, 0))
        return y * cos_ref[...] + partner * sin_ref[...]

    acc = _bdot_nt(w_ref[...], xb)
    for c in range(acc.shape[0] // LANES):
        sl = slice(c * LANES, (c + 1) * LANES)
        y = roped(acc[sl, :]) if rope else acc[sl, :]
        o_ref[sl, :] = y
        if with_bf16:
            o16_ref[sl, :] = y.astype(BF16)

    if extra:
        @pl.when(pl.program_id(2) == 0)
        def _():
            ye = _bdot_nt(we_ref[...], xb)
            oe_ref[...] = roped(ye) if rope else ye


def _linear_t(x, w_t, nb, *, tc, tt, rope=None, with_bf16=False, w_extra_t=None, stack=None):
    n, k = x.shape
    c = w_t.shape[0]
    s = n // nb
    assert n == nb * s and s % tt == 0 and c % tc == 0 and tc % LANES == 0
    nt = s // tt
    extra = w_extra_t is not None
    n_layers, layer, prev = stack if stack is not None else (1, 0, None)
    in_specs = [pl.BlockSpec((tc, k), lambda b, ti, j: (j, 0)),
                pl.BlockSpec((tt, k), lambda b, ti, j: (b * nt + ti, 0))]
    args = [w_t, x]
    if extra:
        assert w_extra_t.shape == (LANES, k)
        in_specs.append(pl.BlockSpec((LANES, k), lambda b, ti, j: (0, 0)))
        args.append(w_extra_t)
    if rope is not None:
        p_blocks = rope[0].shape[1] // tt
        spec = pl.BlockSpec((LANES, tt), lambda b, ti, j: (0, ti % p_blocks))
        in_specs += [spec, spec]
        args += [rope[0], rope[1]]
    aliases = {}
    if prev is not None:
        assert prev.shape == (n_layers * nb, c, s)
        aliases = {len(args): 0}
        in_specs.append(pl.BlockSpec(memory_space=pl.ANY))
        args.append(prev)
    out_spec = pl.BlockSpec((None, tc, tt), lambda b, ti, j: (b, j, ti))
    out_shape = [jax.ShapeDtypeStruct((n_layers * nb, c, s), F32)]
    out_specs = [pl.BlockSpec((None, tc, tt), lambda b, ti, j: (layer * nb + b, j, ti))]
    if with_bf16:
        out_shape.append(jax.ShapeDtypeStruct((nb, c, s), BF16))
        out_specs.append(out_spec)
    if extra:
        out_shape.append(jax.ShapeDtypeStruct((nb, LANES, s), F32))
        out_specs.append(pl.BlockSpec((None, LANES, tt), lambda b, ti, j: (b, 0, ti)))
    outs = pl.pallas_call(
        functools.partial(_linear_t_kernel, rope=rope is not None, with_bf16=with_bf16, extra=extra,
                          aliased=prev is not None),
        out_shape=out_shape,
        grid=(nb, nt, c // tc),
        in_specs=in_specs,
        out_specs=out_specs,
        input_output_aliases=aliases,
        compiler_params=_cparams(("parallel", "parallel", "arbitrary")),
        name="linear_t",
    )(*args)
    return tuple(outs) if len(outs) > 1 else outs[0]


def _linear_heads_kernel(x_ref, w_ref, *o_refs, leaf, groups, aliased):
    xb = x_ref[...].astype(BF16)
    o_refs = list(o_refs)
    if aliased:
        o_refs.pop(0)
    leaf_ref = o_refs.pop(0) if leaf else None
    grp_ref = o_refs.pop(0) if groups else None
    for h in range(w_ref.shape[1] // LANES):
        y = jnp.dot(xb, w_ref[:, h * LANES:(h + 1) * LANES].astype(BF16), preferred_element_type=F32)
        if leaf:
            leaf_ref[:, h, :] = y
        if groups:
            grp_ref[h] = y.astype(BF16)


def _linear_heads(x, w, layer, col0, n_heads, *, tm, leaf=True, groups=False, stack=None):
    n, k = x.shape
    width = n_heads * LANES
    assert n % tm == 0 and col0 % width == 0 and (leaf or groups)
    n_slabs, slab, prev = stack if stack is not None else (1, 0, None)
    in_specs = [pl.BlockSpec((tm, k), lambda i: (i, 0)),
                pl.BlockSpec((None, k, width), lambda i: (layer, 0, col0 // width))]
    args = [x, w]
    aliases = {}
    if prev is not None:
        assert leaf and prev.shape == (n_slabs * n, n_heads, LANES)
        aliases = {len(args): 0}
        in_specs.append(pl.BlockSpec(memory_space=pl.ANY))
        args.append(prev)
    out_shape, out_specs = [], []
    if leaf:
        out_shape.append(jax.ShapeDtypeStruct((n_slabs * n, n_heads, LANES), F32))
        out_specs.append(pl.BlockSpec((tm, n_heads, LANES), lambda i: (slab * (n // tm) + i, 0, 0)))
    if groups:
        out_shape.append(jax.ShapeDtypeStruct((n_heads, n, LANES), BF16))
        out_specs.append(pl.BlockSpec((n_heads, tm, LANES), lambda i: (0, i, 0)))
    outs = pl.pallas_call(
        functools.partial(_linear_heads_kernel, leaf=leaf, groups=groups, aliased=prev is not None),
        out_shape=out_shape,
        grid=(n // tm,),
        in_specs=in_specs,
        out_specs=out_specs,
        input_output_aliases=aliases,
        compiler_params=_cparams(("parallel",)),
        name="linear_heads",
    )(*args)
    return outs[0] if len(outs) == 1 else tuple(outs)


LOG2E = 1.4426950408889634


def _q_pairs_kernel(x_ref, w_ref, cos_ref, sin_ref, o_ref, *, scale):
    acc = _bdot(x_ref[...], w_ref[...])
    tm = acc.shape[0]
    cos = cos_ref[...]
    sin = sin_ref[...]
    lane = lax.broadcasted_iota(jnp.int32, (tm, LANES), 1)
    first_half = (lane % DIFF_HEAD_DIM) < (DIFF_HEAD_DIM // 2)
    lo = lane < DIFF_HEAD_DIM
    for g in range(N_GROUPS):
        y = _rope_apply(acc[:, g * LANES:(g + 1) * LANES], cos, sin, first_half) * scale
        o_ref[g, 0] = jnp.where(lo, y, 0.0).astype(BF16)
        o_ref[g, 1] = jnp.where(lo, 0.0, y).astype(BF16)


def _q_pairs(x, w, layer, rope, *, tm):
    n, k = x.shape
    d = D_MODEL
    assert n % tm == 0 and rope[0].shape[0] % tm == 0
    p_blocks = rope[0].shape[0] // tm
    rspec = pl.BlockSpec((tm, LANES), lambda i: (i % p_blocks, 0))
    return pl.pallas_call(
        functools.partial(_q_pairs_kernel, scale=DIFF_HEAD_DIM ** -0.5 * LOG2E),
        out_shape=jax.ShapeDtypeStruct((N_GROUPS, 2, n, LANES), BF16),
        grid=(n // tm,),
        in_specs=[pl.BlockSpec((tm, k), lambda i: (i, 0)),
                  pl.BlockSpec((None, k, d), lambda i: (layer, 0, 0)), rspec, rspec],
        out_specs=pl.BlockSpec((N_GROUPS, 2, tm, LANES), lambda i: (0, 0, i, 0)),
        compiler_params=_cparams(("parallel",)),
        name="q_pairs",
    )(x, w, rope[0], rope[1])


def _res_ln_kernel(x_ref, w_ref, r_ref, g_ref, b_ref, o_ref):
    y = _bdot(x_ref[...], w_ref[...])
    z = DEEPNORM_ALPHA * r_ref[...] + y
    o_ref[...] = _layer_norm_rows(z, g_ref[...], b_ref[...])


def _linear_res_ln(x, w, layer, resid, g, b, ln_layer, *, tm):
    n, k = x.shape
    d = w.shape[2]
    assert n % tm == 0
    row = lambda i: (i, 0)
    ln_row = lambda i: (ln_layer, 0, 0)
    return pl.pallas_call(
        _res_ln_kernel,
        out_shape=jax.ShapeDtypeStruct((n, d), F32),
        grid=(n // tm,),
        in_specs=[pl.BlockSpec((tm, k), row), pl.BlockSpec((None, k, d), lambda i: (layer, 0, 0)),
                  pl.BlockSpec((tm, d), row),
                  pl.BlockSpec((None, 1, d), ln_row), pl.BlockSpec((None, 1, d), ln_row)],
        out_specs=pl.BlockSpec((tm, d), row),
        compiler_params=_cparams(("parallel",)),
        name="out_proj_ln",
    )(x, w, resid, g.reshape(-1, 1, d), b.reshape(-1, 1, d))


def _ffn_kernel(x_ref, wg_ref, wu_ref, wd_ref, g_ref, b_ref, o_ref, acc_ref, xb_ref):
    f = pl.program_id(1)

    @pl.when(f == 0)
    def _():
        acc_ref[...] = jnp.zeros_like(acc_ref)
        xb_ref[...] = x_ref[...].astype(BF16)

    xb = xb_ref[...]
    gate = jnp.dot(xb, wg_ref[...].astype(BF16), preferred_element_type=F32)
    up = jnp.dot(xb, wu_ref[...].astype(BF16), preferred_element_type=F32)
    h = gate * _sigmoid(gate) * up
    acc_ref[...] += _bdot(h, wd_ref[...])

    @pl.when(f == pl.num_programs(1) - 1)
    def _():
        z = DEEPNORM_ALPHA * x_ref[...] + acc_ref[...]
        o_ref[...] = _layer_norm_rows(z, g_ref[...], b_ref[...])


def _ffn_ln(x, w_gate_up, w_down, g, b, layer, *, tm, tf):
    n, d = x.shape
    dff = w_down.shape[1]
    assert n % tm == 0 and dff % tf == 0
    nf = dff // tf
    ln_row = lambda i, f: (layer, 0, 0)
    return pl.pallas_call(
        _ffn_kernel,
        out_shape=jax.ShapeDtypeStruct((n, d), F32),
        grid=(n // tm, nf),
        in_specs=[pl.BlockSpec((tm, d), lambda i, f: (i, 0)),
                  pl.BlockSpec((None, d, tf), lambda i, f: (layer, 0, f)),
                  pl.BlockSpec((None, d, tf), lambda i, f: (layer, 0, nf + f)),
                  pl.BlockSpec((None, tf, d), lambda i, f: (layer, f, 0)),
                  pl.BlockSpec((None, 1, d), ln_row), pl.BlockSpec((None, 1, d), ln_row)],
        out_specs=pl.BlockSpec((tm, d), lambda i, f: (i, 0)),
        scratch_shapes=[pltpu.VMEM((tm, d), F32), pltpu.VMEM((tm, d), BF16)],
        compiler_params=_cparams(("parallel", "arbitrary")),
        name="ffn_ln",
    )(x, w_gate_up, w_gate_up, w_down, g.reshape(-1, 1, d), b.reshape(-1, 1, d))


def _diff_lambda(lam_ref, lam_init):
    l = lam_ref[...]
    a = jnp.sum(l[0:1] * l[1:2], axis=-1, keepdims=True)
    c = jnp.sum(l[2:3] * l[3:4], axis=-1, keepdims=True)
    return jnp.exp(a) - jnp.exp(c) + lam_init


def _rms_rows(o, g):
    return o * lax.rsqrt(jnp.mean(o * o, axis=-1, keepdims=True) + LN_EPS) * g


def _flash_kernel(qi_ref, ki_ref, *refs, mode, t, lam_init):
    if mode == "diff":
        q_ref, k_ref, v_ref, lam_ref, g_ref, o_ref, m_sc, l_sc, acc_sc = refs
    else:
        q_ref, k_ref, v_ref, bias_ref, o_ref, m_sc, l_sc, acc_sc = refs
    step = pl.program_id(1)
    qi = qi_ref[step]
    ki = ki_ref[step]

    @pl.when(ki == 0)
    def _():
        m_sc[...] = jnp.full_like(m_sc, NEG)
        l_sc[...] = jnp.zeros_like(l_sc)
        acc_sc[...] = jnp.zeros_like(acc_sc)

    def block_update(causal):
        if causal:
            row = lax.broadcasted_iota(jnp.int32, (2 * t, t), 0)
            col = lax.broadcasted_iota(jnp.int32, (2 * t, t), 1)
            allowed = col <= jnp.where(row >= t, row - t, row)

        def group(g, carry):
            q2 = q_ref[g].reshape(2 * t, LANES)
            s = jnp.dot(q2, k_ref[g], preferred_element_type=F32)
            if mode == "dsa":
                bias = bias_ref[...]
                s = s + jnp.concatenate([bias, bias], axis=0)
            if causal:
                s = jnp.where(allowed, s, NEG)
            m_prev = m_sc[g]
            m_next = jnp.maximum(m_prev, jnp.max(s, axis=-1, keepdims=True))
            p = jnp.exp2(s - jnp.concatenate([m_next] * (t // LANES), axis=-1))
            alpha = jnp.exp2(m_prev - m_next)
            l_sc[g] = alpha * l_sc[g] + jnp.sum(p, axis=-1, keepdims=True)
            acc_sc[g] = alpha * acc_sc[g] + jnp.dot(p.astype(BF16), v_ref[g], preferred_element_type=F32)
            m_sc[g] = m_next
            return carry

        lax.fori_loop(0, N_GROUPS, group, 0, unroll=2)

    if mode == "diff":
        pl.when(ki < qi)(lambda: block_update(False))
        pl.when(ki == qi)(lambda: block_update(True))
    else:
        block_update(False)

    @pl.when(ki == qi)
    def _():
        lane = lax.broadcasted_iota(jnp.int32, (1, LANES), 1)
        lo = lane < DIFF_HEAD_DIM
        if mode == "diff":
            lam = _diff_lambda(lam_ref, lam_init)
            gain = g_ref[...] * (1.0 - lam_init)
        for g in range(N_GROUPS):
            o_lo = acc_sc[g, :t, :] / l_sc[g, :t, :]
            o_hi = acc_sc[g, t:, :] / l_sc[g, t:, :]
            if mode == "diff":
                o_ref[:, g * LANES:(g + 1) * LANES] = _rms_rows(o_lo - lam * o_hi, gain)
            else:
                o_ref[:, g * LANES:(g + 1) * LANES] = jnp.where(lo, o_lo, o_hi)


def _flash_attention(q2, k_t, v_g, b, *, mode, t, lam_params=None, subln_g=None, lam_init=0.0, bias=None):
    n = q2.shape[2]
    s = n // b
    d = D_MODEL
    assert s % t == 0 and n == b * s
    nblk = s // t
    k4 = k_t.reshape(b, N_GROUPS, LANES, s)
    pairs = [(qb, kb) for qb in range(nblk) for kb in range(qb + 1)]
    qi_of = jnp.asarray([p[0] for p in pairs], dtype=jnp.int32)
    ki_of = jnp.asarray([p[1] for p in pairs], dtype=jnp.int32)
    qspec = pl.BlockSpec((N_GROUPS, 2, t, LANES), lambda bi, st, qo, ko: (0, 0, bi * nblk + qo[st], 0))
    kspec = pl.BlockSpec((None, N_GROUPS, LANES, t), lambda bi, st, qo, ko: (bi, 0, 0, ko[st]))
    vspec = pl.BlockSpec((N_GROUPS, t, LANES), lambda bi, st, qo, ko: (0, bi * nblk + ko[st], 0))
    in_specs = [qspec, kspec, vspec]
    args = [q2, k4, v_g]
    if mode == "diff":
        in_specs += [pl.BlockSpec((4, DIFF_HEAD_DIM), lambda bi, st, qo, ko: (0, 0)),
                     pl.BlockSpec((1, LANES), lambda bi, st, qo, ko: (0, 0))]
        args += [lam_params, subln_g.reshape(1, LANES)]
    else:
        assert bias.shape == (b, nblk, s, t)
        in_specs += [pl.BlockSpec((None, None, t, t), lambda bi, st, qo, ko: (bi, ko[st], qo[st], 0))]
        args += [bias]
    stat = pltpu.VMEM((N_GROUPS, 2 * t, LANES), F32)
    return pl.pallas_call(
        functools.partial(_flash_kernel, mode=mode, t=t, lam_init=lam_init),
        out_shape=jax.ShapeDtypeStruct((n, d), F32),
        grid_spec=pltpu.PrefetchScalarGridSpec(
            num_scalar_prefetch=2,
            grid=(b, len(pairs)),
            in_specs=in_specs,
            out_specs=pl.BlockSpec((t, d), lambda bi, st, qo, ko: (bi * nblk + qo[st], 0)),
            scratch_shapes=[stat, stat, stat]),
        compiler_params=_cparams(("parallel", "arbitrary")),
        name="flash_" + mode,
    )(qi_of, ki_of, *args)


def _decode_kernel(pt_ref, q_ref, *refs, mode, n_pp, n_pages, scale, lam_init):
    del pt_ref
    k_refs = refs[:n_pp]
    v_refs = refs[n_pp:2 * n_pp]
    knew_ref, vnew_ref = refs[2 * n_pp:2 * n_pp + 2]
    rest = refs[2 * n_pp + 2:]
    if mode == "diff":
        lam_ref, g_ref, o_ref, qm_sc, m_sc, l_sc, acc_sc, e_sc = rest
    else:
        bias_ref, o_ref, qm_sc, m_sc, l_sc, acc_sc = rest
    step = pl.program_id(1)
    n_sub = 2 * N_GROUPS
    row = lax.broadcasted_iota(jnp.int32, (n_sub, D_MODEL), 0)
    lane = lax.broadcasted_iota(jnp.int32, (n_sub, D_MODEL), 1)

    @pl.when(step == 0)
    def _():
        qm_sc[...] = jnp.where(lane // DIFF_HEAD_DIM == row, q_ref[...] * scale, 0.0)
        m_sc[...] = jnp.full_like(m_sc, NEG)
        l_sc[...] = jnp.zeros_like(l_sc)
        acc_sc[...] = jnp.zeros_like(acc_sc)
        if mode == "diff":
            tok = lax.broadcasted_iota(jnp.int32, (PAGE_SIZE, D_MODEL), 0)
            col = lax.broadcasted_iota(jnp.int32, (PAGE_SIZE, D_MODEL), 1)
            e_sc[...] = jnp.where(col // DIFF_HEADS == tok, 1.0, 0.0).astype(BF16)

    qm = qm_sc[...]
    s_parts = []
    for i in range(n_pp):
        s = _bdot(qm, k_refs[i][...])
        if mode == "dsa":
            s = s + bias_ref[pl.ds(step * n_pp + i, 1), :]
        s_parts.append(s)
    s_all = jnp.concatenate(s_parts, axis=-1)
    m_prev = m_sc[...]
    m_new = jnp.maximum(m_prev, jnp.max(s_all, axis=-1, keepdims=True))
    alpha = jnp.exp(m_prev - m_new)
    p_all = jnp.exp(s_all - m_new)
    l_sc[...] = alpha * l_sc[...] + jnp.sum(p_all, axis=-1, keepdims=True)
    pv = None
    for i in range(n_pp):
        p_i = p_all[:, i * PAGE_SIZE:(i + 1) * PAGE_SIZE]
        if mode == "diff":
            p_rows = jnp.where(lane % DIFF_HEADS == row // 2,
                               jnp.dot(p_i.astype(BF16), e_sc[...], preferred_element_type=F32), 0.0)
            part = _bdot(p_rows, v_refs[i][...])
        else:
            part = _bdot_nt(p_i, v_refs[i][...])
        pv = part if pv is None else pv + part
    acc_sc[...] = alpha * acc_sc[...] + pv
    m_sc[...] = m_new

    @pl.when(step == pl.num_programs(1) - 1)
    def _():
        s_new = jnp.sum(qm * knew_ref[...], axis=-1, keepdims=True)
        if mode == "dsa":
            s_new = s_new + bias_ref[n_pages:n_pages + 1, 0:1]
        m_prev = m_sc[...]
        m_fin = jnp.maximum(m_prev, s_new)
        alpha = jnp.exp(m_prev - m_fin)
        p_new = jnp.exp(s_new - m_fin)
        l_fin = alpha * l_sc[...] + p_new
        if mode == "diff":
            row_h = lax.broadcasted_iota(jnp.int32, (n_sub, LANES), 0) // 2
            v_rows = jnp.zeros((n_sub, LANES), F32)
            for h in range(DIFF_HEADS):
                v_rows = jnp.where(row_h == h, vnew_ref[:, h * LANES:(h + 1) * LANES], v_rows)
            o16 = (alpha * acc_sc[...] + p_new * v_rows) / l_fin
            lam = _diff_lambda(lam_ref, lam_init)
            gain = g_ref[...] * (1.0 - lam_init)
            for h in range(DIFF_HEADS):
                o_h = o16[2 * h:2 * h + 1, :] - lam * o16[2 * h + 1:2 * h + 2, :]
                o_ref[:, h * LANES:(h + 1) * LANES] = _rms_rows(o_h, gain)
        else:
            o16 = (alpha * acc_sc[...] + p_new * vnew_ref[...]) / l_fin
            o_ref[...] = jnp.sum(jnp.where(lane // DSA_HEAD_DIM == row, o16, 0.0), axis=0, keepdims=True)


def _decode_attention(q, cache_k, cache_v, page_ids, k_new, v_new, *, mode, n_pp,
                      lam_params=None, subln_g=None, lam_init=0.0, bias=None):
    bd, d = q.shape
    n_pages = page_ids.shape[1]
    assert n_pages % n_pp == 0 and cache_k.shape[1:] == (d, PAGE_SIZE) and cache_v.shape[1:] == (d, PAGE_SIZE)
    n_steps = n_pages // n_pp
    row_spec = pl.BlockSpec((None, 1, d), lambda b, s, pt: (b, 0, 0))

    def page_spec(i):
        return pl.BlockSpec((None, d, PAGE_SIZE), lambda b, s, pt: (pt[b * n_pages + s * n_pp + i], 0, 0))

    in_specs = [row_spec] + [page_spec(i) for i in range(n_pp)] * 2 + [row_spec, row_spec]
    args = [q.reshape(bd, 1, d)] + [cache_k] * n_pp + [cache_v] * n_pp + [k_new.reshape(bd, 1, d), v_new.reshape(bd, 1, d)]
    n_sub = 2 * N_GROUPS
    scratch = [pltpu.VMEM((n_sub, d), F32), pltpu.VMEM((n_sub, 1), F32), pltpu.VMEM((n_sub, 1), F32)]
    if mode == "diff":
        in_specs += [pl.BlockSpec((4, DIFF_HEAD_DIM), lambda b, s, pt: (0, 0)),
                     pl.BlockSpec((1, LANES), lambda b, s, pt: (0, 0))]
        args += [lam_params, subln_g.reshape(1, LANES)]
        scratch += [pltpu.VMEM((n_sub, LANES), F32), pltpu.VMEM((PAGE_SIZE, d), BF16)]
    else:
        in_specs += [pl.BlockSpec((None, n_pages + 1, PAGE_SIZE), lambda b, s, pt: (b, 0, 0))]
        args += [bias]
        scratch += [pltpu.VMEM((n_sub, d), F32)]
    out = pl.pallas_call(
        functools.partial(_decode_kernel, mode=mode, n_pp=n_pp, n_pages=n_pages,
                          scale=DIFF_HEAD_DIM ** -0.5, lam_init=lam_init),
        out_shape=jax.ShapeDtypeStruct((bd, 1, d), F32),
        grid_spec=pltpu.PrefetchScalarGridSpec(
            num_scalar_prefetch=1,
            grid=(bd, n_steps),
            in_specs=in_specs,
            out_specs=row_spec,
            scratch_shapes=scratch),
        compiler_params=_cparams(("parallel", "arbitrary")),
        name="decode_" + mode,
    )(page_ids.reshape(-1), *args)
    return out.reshape(bd, d)


def _sortable_keys(score):
    bits = lax.bitcast_convert_type(score, jnp.int32)
    key = jnp.where(bits < 0, bits ^ jnp.int32(0x7FFFFFFF), bits)
    return jnp.where(score == 0.0, 0, key)


def _kth_threshold(key_ref, nch, ch, ksel):
    def count_ge(cand):
        def body(c, acc):
            blk = key_ref[pl.ds(pl.multiple_of(c * ch, ch), ch), :]
            hit = jnp.where(blk >= cand, 1, 0).astype(jnp.int32)
            return acc + jnp.sum(hit.reshape(ch // SUBLANES, SUBLANES, LANES), axis=0)

        acc = lax.fori_loop(0, nch, body, jnp.zeros((SUBLANES, LANES), jnp.int32))
        return jnp.sum(acc, axis=0, keepdims=True)

    def bit_body(i, t):
        cand = t ^ lax.shift_left(jnp.int32(1), jnp.asarray(31 - i, dtype=jnp.int32))
        return jnp.where(count_ge(cand) >= ksel, cand, t)

    t = lax.fori_loop(0, 32, bit_body, jnp.full((1, LANES), INT_MIN, jnp.int32))
    n_above = count_ge(t + 1)
    need = ksel - n_above
    surplus = jnp.logical_and(t > INT_MIN, count_ge(t) - n_above > need)
    return t, need.astype(F32), surplus


def _tri_ones(n):
    row = lax.broadcasted_iota(jnp.int32, (n, n), 0)
    col = lax.broadcasted_iota(jnp.int32, (n, n), 1)
    return jnp.where(col <= row, 1.0, 0.0).astype(BF16)


def _select_chunk(key, t, need, carry, tri):
    eq = key == t
    pref = jnp.dot(tri, jnp.where(eq, 1.0, 0.0).astype(BF16), preferred_element_type=F32) + carry
    sel = jnp.logical_or(key > t, jnp.logical_and(eq, pref <= need))
    sel = jnp.logical_and(sel, key > INT_MIN)
    return sel, pref[key.shape[0] - 1:, :]


def _dsa_index_prompt_kernel(idx_ref, kd_ref, o_ref, key_sc, *, tq, ch, ksel):
    qi = pl.program_id(1)
    n_ch_total = o_ref.shape[0]
    nch = (qi * tq + tq + ch - 1) // ch
    qblk = idx_ref[...]
    lane = lax.broadcasted_iota(jnp.int32, (1, LANES), 1)
    lo = lane < IDX_DIM
    parts = []
    for h in range(IDX_HEADS):
        grp = qblk[:, (h // 2) * LANES:(h // 2 + 1) * LANES]
        parts.append(jnp.where(lo if h % 2 == 0 else jnp.logical_not(lo), grp, 0.0).astype(BF16))
    qstack = jnp.concatenate(parts, axis=0)
    w = qblk[:, 5 * LANES:6 * LANES] * (IDX_HEADS ** -0.5 * IDX_DIM ** -0.5)
    w_cols = [w[:, h:h + 1] for h in range(IDX_HEADS)]
    qpos = qi * tq + lane

    def score_body(c, _):
        base = pl.multiple_of(c * ch, ch)
        r = _bdot(qstack, kd_ref[:, pl.ds(base, ch)])
        score = jnp.zeros((tq, ch), F32)
        for h in range(IDX_HEADS):
            score = score + w_cols[h] * jnp.maximum(r[h * tq:(h + 1) * tq, :], 0.0)
        kpos = base + lax.broadcasted_iota(jnp.int32, (ch, tq), 0)
        key_sc[pl.ds(base, ch), :] = jnp.where(kpos <= qpos, _sortable_keys(jnp.transpose(score)), INT_MIN)
        return 0

    lax.fori_loop(0, nch, score_body, 0)
    t, need, surplus = _kth_threshold(key_sc, nch, ch, ksel)
    any_surplus = jnp.max(jnp.where(surplus, 1, 0)) > 0

    @pl.when(any_surplus)
    def _():
        tri = _tri_ones(ch)

        def out_body(c, carry):
            base = pl.multiple_of(c * ch, ch)
            sel, carry = _select_chunk(key_sc[pl.ds(base, ch), :], t, need, carry, tri)
            o_ref[c] = jnp.transpose(jnp.where(sel, 0.0, NEG))
            return carry

        lax.fori_loop(0, nch, out_body, jnp.zeros((1, LANES), F32))

    @pl.when(jnp.logical_not(any_surplus))
    def _():
        def out_body(c, _):
            key = key_sc[pl.ds(pl.multiple_of(c * ch, ch), ch), :]
            sel = jnp.logical_and(key >= t, key > INT_MIN)
            o_ref[c] = jnp.transpose(jnp.where(sel, 0.0, NEG))
            return 0

        lax.fori_loop(0, nch, out_body, 0)

    def fill_body(c, _):
        o_ref[c] = jnp.full((tq, ch), NEG, F32)
        return 0

    lax.fori_loop(nch, n_ch_total, fill_body, 0)


def _dsa_index_prompt(idx, kd_t, *, ch):
    b, s, w = idx.shape
    tq = LANES
    assert s % ch == 0 and ch % tq == 0
    ksel = min(IDX_TOPK_MAX, s // 4)
    return pl.pallas_call(
        functools.partial(_dsa_index_prompt_kernel, tq=tq, ch=ch, ksel=ksel),
        out_shape=jax.ShapeDtypeStruct((b, s // ch, s, ch), F32),
        grid=(b, s // tq),
        in_specs=[pl.BlockSpec((None, tq, w), lambda bi, qi: (bi, qi, 0)),
                  pl.BlockSpec((None, LANES, s), lambda bi, qi: (bi, 0, 0))],
        out_specs=pl.BlockSpec((None, s // ch, tq, ch), lambda bi, qi: (bi, 0, qi, 0)),
        scratch_shapes=[pltpu.VMEM((s, LANES), jnp.int32)],
        compiler_params=_cparams(("parallel", "arbitrary")),
        name="dsa_index_prompt",
    )(idx, kd_t)


def _dsa_score_sample_kernel(pt_ref, q_ref, w_ref, knew_ref, *refs, n_pages):
    del pt_ref
    page_refs = refs[:n_pages]
    o_ref = refs[n_pages]
    q = q_ref[...]
    w = w_ref[...] * (IDX_HEADS ** -0.5 * IDX_DIM ** -0.5)
    for p in range(n_pages):
        s = _bdot(q, page_refs[p][...])
        o_ref[p:p + 1, :] = jnp.sum(w * jnp.maximum(s, 0.0), axis=0, keepdims=True)
    s_new = jnp.sum(q * knew_ref[...], axis=-1, keepdims=True)
    sc_new = jnp.sum(w * jnp.maximum(s_new, 0.0), axis=0, keepdims=True)
    lane = lax.broadcasted_iota(jnp.int32, (1, PAGE_SIZE), 1)
    o_ref[n_pages:n_pages + 1, :] = jnp.where(lane == 0, sc_new, NEG)


def _dsa_score_sample(qi, wi, ki_new, cache_kidx, page_ids):
    bd = qi.shape[0]
    n_pages = page_ids.shape[1]
    assert cache_kidx.shape[1:] == (IDX_DIM, PAGE_SIZE)

    def page_spec(p):
        return pl.BlockSpec((None, IDX_DIM, PAGE_SIZE), lambda b, pt: (pt[b * n_pages + p], 0, 0))

    return pl.pallas_call(
        functools.partial(_dsa_score_sample_kernel, n_pages=n_pages),
        out_shape=jax.ShapeDtypeStruct((bd, n_pages + 1, PAGE_SIZE), F32),
        grid_spec=pltpu.PrefetchScalarGridSpec(
            num_scalar_prefetch=1,
            grid=(bd,),
            in_specs=[pl.BlockSpec((None, IDX_HEADS, IDX_DIM), lambda b, pt: (b, 0, 0)),
                      pl.BlockSpec((None, IDX_HEADS, 1), lambda b, pt: (b, 0, 0)),
                      pl.BlockSpec((None, 1, IDX_DIM), lambda b, pt: (b, 0, 0))]
                     + [page_spec(p) for p in range(n_pages)],
            out_specs=pl.BlockSpec((None, n_pages + 1, PAGE_SIZE), lambda b, pt: (b, 0, 0))),
        compiler_params=_cparams(("parallel",)),
        name="dsa_score_sample",
    )(page_ids.reshape(-1), qi, wi, ki_new, *([cache_kidx] * n_pages))


def _select_bias_kernel(s_ref, o_ref, key_sc, *, n_keys, ch, ksel):
    n_rows = s_ref.shape[0]
    nch = n_rows // ch
    kpos = lax.broadcasted_iota(jnp.int32, (n_rows, LANES), 0)
    key_sc[...] = jnp.where(kpos < n_keys, _sortable_keys(s_ref[...]), INT_MIN)
    t, need, _ = _kth_threshold(key_sc, nch, ch, ksel)
    tri = _tri_ones(ch)
    carry = jnp.zeros((1, LANES), F32)
    for c in range(nch):
        sel, carry = _select_chunk(key_sc[c * ch:(c + 1) * ch, :], t, need, carry, tri)
        o_ref[c * ch:(c + 1) * ch, :] = jnp.where(sel, 0.0, NEG)


def _select_bias(score_t, n_keys, ksel):
    n_rows, n_q = score_t.shape
    assert n_q == LANES and n_rows % LANES == 0
    return pl.pallas_call(
        functools.partial(_select_bias_kernel, n_keys=n_keys, ch=LANES, ksel=ksel),
        out_shape=jax.ShapeDtypeStruct((n_rows, LANES), F32),
        scratch_shapes=[pltpu.VMEM((n_rows, LANES), jnp.int32)],
        compiler_params=pltpu.CompilerParams(vmem_limit_bytes=VMEM_LIMIT),
        name="dsa_select_sample",
    )(score_t)


def _log_sigmoid(x):
    return -(jnp.maximum(-x, 0.0) + jnp.log1p(jnp.exp(-jnp.abs(x))))


def _gla_gate(glow, wgu, gate_b):
    return _log_sigmoid(_bdot(glow, wgu) + gate_b) / GLA_TAU


def _column(row_vec, eye):
    n = eye.shape[0]
    return jnp.sum(jnp.where(eye, jnp.broadcast_to(row_vec, (n, n)), 0.0), axis=-1, keepdims=True)


def _level_reference(b_ref, half, c):
    sub = lax.broadcasted_iota(jnp.int32, (SUBLANES, LANES), 0)
    slabs = []
    for t0 in range(0, c, SUBLANES):
        cur = None
        for t in range(t0, t0 + SUBLANES, min(SUBLANES, 2 * half)):
            r = (t // (2 * half)) * (2 * half) + half - 1
            bc = jnp.broadcast_to(b_ref[r:r + 1, :], (SUBLANES, LANES))
            cur = bc if cur is None else jnp.where(sub >= (t - t0), bc, cur)
        slabs.append(cur)
    return jnp.concatenate(slabs, axis=0)


def _gla_prompt_kernel(q_ref, k_ref, v_ref, r_ref, gl_ref, wgu_ref, gb_ref, ng_ref, o_ref, st_ref,
                       s_sc, b_sc, *, c, hps):
    ci = pl.program_id(2)
    hk, hv = GLA_HEAD_K, GLA_HEAD_V

    @pl.when(ci == 0)
    def _():
        s_sc[...] = jnp.zeros_like(s_sc)

    g_all = _gla_gate(gl_ref[...], wgu_ref[...], gb_ref[...])
    row = lax.broadcasted_iota(jnp.int32, (c, c), 0)
    col = lax.broadcasted_iota(jnp.int32, (c, c), 1)
    tri = jnp.where(col <= row, 1.0, 0.0).astype(BF16)
    eye = row == col
    for hh in range(hps):
        ksl = slice(hh * hk, (hh + 1) * hk)
        vsl = slice(hh * hv, (hh + 1) * hv)
        g = g_all[:, ksl]
        g1 = g.astype(BF16)
        rem = g - g1.astype(F32)
        g2 = rem.astype(BF16)
        g3 = (rem - g2.astype(F32)).astype(BF16)
        bcum = (jnp.dot(tri, g1, preferred_element_type=F32) + jnp.dot(tri, g2, preferred_element_type=F32)
                + jnp.dot(tri, g3, preferred_element_type=F32))
        b_ref = b_sc.at[hh]
        b_ref[...] = bcum
        q = q_ref[:, ksl] * (GLA_HEAD_K ** -0.5)
        k = k_ref[:, ksl]
        v = v_ref[:, vsl]
        att = jnp.where(eye, jnp.sum(q * k, axis=-1, keepdims=True), 0.0)
        half = c // 2
        while half >= 1:
            ref = _level_reference(b_ref, half, c)
            qt = q * jnp.exp(jnp.minimum(bcum - ref, 0.0))
            kt = k * jnp.exp(jnp.minimum(ref - bcum, 0.0))
            valid = jnp.logical_and(row // (2 * half) == col // (2 * half),
                                    jnp.logical_and(row % (2 * half) >= half, col % (2 * half) < half))
            att = jnp.where(valid, _bdot_nt(qt, kt), att)
            half //= 2
        s0 = s_sc[hh]
        o = _bdot(q * jnp.exp(bcum), s0) + _bdot(att, v)
        o = _rms_rows(o, ng_ref[...])
        rr = r_ref[:, vsl]
        o_ref[:, vsl] = o * (rr * _sigmoid(rr))
        b_last = b_ref[c - 1:c, :]
        khat = k * jnp.exp(b_last - bcum)
        s_new = _column(jnp.exp(b_last), eye) * s0 + _bdot_tn(khat, v)
        s_sc[hh] = s_new

    @pl.when(ci == pl.num_programs(2) - 1)
    def _():
        st_ref[...] = s_sc[...]


def _gla_prompt(qk, v, r, glow, wgu, gate_b, norm_g, *, c, hps):
    b, s, _ = qk.shape
    assert s % c == 0 and c == GLA_HEAD_K and GLA_HEADS % hps == 0
    hk, hv = GLA_HEAD_K, GLA_HEAD_V
    nhb = GLA_HEADS // hps
    return pl.pallas_call(
        functools.partial(_gla_prompt_kernel, c=c, hps=hps),
        out_shape=(jax.ShapeDtypeStruct((b, s, GLA_DV), F32),
                   jax.ShapeDtypeStruct((b, GLA_HEADS, hk, hv), F32)),
        grid=(b, nhb, s // c),
        in_specs=[pl.BlockSpec((None, c, hps * hk), lambda bi, h, ci: (bi, ci, h)),
                  pl.BlockSpec((None, c, hps * hk), lambda bi, h, ci: (bi, ci, nhb + h)),
                  pl.BlockSpec((None, c, hps * hv), lambda bi, h, ci: (bi, ci, h)),
                  pl.BlockSpec((None, c, hps * hv), lambda bi, h, ci: (bi, ci, h)),
                  pl.BlockSpec((None, c, LANES), lambda bi, h, ci: (bi, ci, 0)),
                  pl.BlockSpec((LANES, hps * hk), lambda bi, h, ci: (0, h)),
                  pl.BlockSpec((1, hps * hk), lambda bi, h, ci: (0, h)),
                  pl.BlockSpec((1, hv), lambda bi, h, ci: (0, 0))],
        out_specs=(pl.BlockSpec((None, c, hps * hv), lambda bi, h, ci: (bi, ci, h)),
                   pl.BlockSpec((None, hps, hk, hv), lambda bi, h, ci: (bi, h, 0, 0))),
        scratch_shapes=[pltpu.VMEM((hps, hk, hv), F32), pltpu.VMEM((hps, c, hk), F32)],
        compiler_params=_cparams(("parallel", "parallel", "arbitrary")),
        name="gla_prompt",
    )(qk, qk, v, r, glow, wgu, gate_b.reshape(1, GLA_DK), norm_g.reshape(1, hv))


def _gla_sample_kernel(qk_ref, v_ref, r_ref, gl_ref, wgu_ref, gb_ref, ng_ref, st_ref, o_ref, nst_ref):
    hk, hv = GLA_HEAD_K, GLA_HEAD_V
    glow = jnp.broadcast_to(gl_ref[...], (SUBLANES, LANES))
    g = _gla_gate(glow, wgu_ref[...], gb_ref[...])[0:1, :]
    eye = (lax.broadcasted_iota(jnp.int32, (hk, hk), 0) == lax.broadcasted_iota(jnp.int32, (hk, hk), 1))
    for h in range(GLA_HEADS):
        ksl = slice(h * hk, (h + 1) * hk)
        vsl = slice(h * hv, (h + 1) * hv)
        q_col = _column(qk_ref[:, ksl] * (GLA_HEAD_K ** -0.5), eye)
        k_col = _column(qk_ref[:, GLA_DK + h * hk:GLA_DK + (h + 1) * hk], eye)
        a_col = _column(jnp.exp(g[:, ksl]), eye)
        s_new = a_col * st_ref[h] + k_col * v_ref[:, vsl]
        nst_ref[h] = s_new
        o = jnp.sum(q_col * s_new, axis=0, keepdims=True)
        rr = r_ref[:, vsl]
        o_ref[:, vsl] = _rms_rows(o, ng_ref[...]) * (rr * _sigmoid(rr))


def _gla_sample(qk, v, r, glow, wgu, gate_b, norm_g, state, state_row0):
    bd = qk.shape[0]
    hk, hv = GLA_HEAD_K, GLA_HEAD_V
    rowspec = lambda w: pl.BlockSpec((None, 1, w), lambda b: (b, 0, 0))
    fixed2 = lambda b: (0, 0)
    o, nst = pl.pallas_call(
        _gla_sample_kernel,
        out_shape=(jax.ShapeDtypeStruct((bd, 1, GLA_DV), F32),
                   jax.ShapeDtypeStruct((bd, GLA_HEADS, hk, hv), F32)),
        grid=(bd,),
        in_specs=[rowspec(2 * GLA_DK), rowspec(GLA_DV), rowspec(GLA_DV), rowspec(LANES),
                  pl.BlockSpec((LANES, GLA_DK), fixed2), pl.BlockSpec((1, GLA_DK), fixed2),
                  pl.BlockSpec((1, hv), fixed2),
                  pl.BlockSpec((None, GLA_HEADS, hk, hv), lambda b: (state_row0 + b, 0, 0, 0))],
        out_specs=(rowspec(GLA_DV), pl.BlockSpec((None, GLA_HEADS, hk, hv), lambda b: (b, 0, 0, 0))),
        compiler_params=_cparams(("parallel",)),
        name="gla_sample",
    )(qk.reshape(bd, 1, -1), v.reshape(bd, 1, -1), r.reshape(bd, 1, -1), glow.reshape(bd, 1, -1),
      wgu, gate_b.reshape(1, GLA_DK), norm_g.reshape(1, hv), state)
    return o.reshape(bd, GLA_DV), nst


def _row_tile(n, cap):
    t = cap
    while n % t:
        t //= 2
    return t


def _project(x, w, layer, col0, width, rope=None, n_rope=None, tn=None):
    tm = _row_tile(x.shape[0], 1024)
    if tn is None:
        tn = width if width <= D_MODEL else (512 if width % 512 == 0 else LANES)
    return _linear(x, w, layer, col0, width, tm=tm, tn=tn, rope=rope, n_rope=n_rope)


def _pages_per_step(n_pages):
    for n in (16, 4, 2):
        if n_pages % n == 0:
            return n
    return 1


def _project_t(x, w_t, nb, rope=None, with_bf16=False, w_extra_t=None, stack=None):
    s = x.shape[0] // nb
    return _linear_t(x, w_t, nb, tc=min(D_MODEL, w_t.shape[0]), tt=_row_tile(s, 512), rope=rope, with_bf16=with_bf16,
                     w_extra_t=w_extra_t, stack=stack)


def _heads_last(x_t, n_heads):
    b, c, s = x_t.shape
    return x_t.reshape(b, n_heads, c // n_heads, s).transpose(0, 3, 1, 2)


def _diff_mixer(xp, xs, dims, cache_k, cache_v, page_ids, w_in, j, lam_params, subln_g, lam_init, ropes,
                k_stack, v_stack):
    b, s, bd = dims
    d = D_MODEL
    n_diff = w_in.shape[0]
    rope_p, rope_s, rope_pt, rope_st = ropes
    w_k_t = w_in[j][:, d:2 * d].T
    tm = _row_tile(b * s, 512)
    q2_p = _q_pairs(xp, w_in, j, rope_p, tm=_row_tile(s, 512))
    k_stack, k_pt16 = _project_t(xp, w_k_t, b, rope_pt, with_bf16=True,
                                 stack=(n_diff, j, k_stack))
    v_stack, v_pg = _linear_heads(xp, w_in, j, 2 * d, DIFF_HEADS, tm=tm, groups=True,
                                  stack=(n_diff, j, v_stack))
    o_p = _flash_attention(q2_p, k_pt16, v_pg, b, mode="diff", t=_row_tile(s, 512),
                           lam_params=lam_params, subln_g=subln_g, lam_init=lam_init)
    q_s = _project(xs, w_in, j, 0, d, rope_s)
    k_s = _project(xs, w_in, j, d, d, rope_s)
    v_s = _project(xs, w_in, j, 2 * d, d)
    k_st = _project_t(xs, w_k_t, 1, rope_st)
    v_sh = _linear_heads(xs, w_in, j, 2 * d, DIFF_HEADS, tm=_row_tile(bd, 512))
    o_s = _decode_attention(q_s, cache_k, cache_v, page_ids, k_s, v_s, mode="diff",
                            n_pp=_pages_per_step(page_ids.shape[1]),
                            lam_params=lam_params, subln_g=subln_g, lam_init=lam_init)
    k_leaf_s = _heads_last(k_st, 2 * DIFF_HEADS).reshape(bd, 1, 2 * DIFF_HEADS, DIFF_HEAD_DIM)
    v_leaf_s = v_sh.reshape(bd, 1, DIFF_HEADS, LANES)
    return o_p.reshape(b * s, d), o_s, k_stack, v_stack, k_leaf_s, v_leaf_s


def _dsa_mixer(xp, xs, dims, cache_k, cache_v, cache_kidx, page_ids, w_in, j, ropes, past_len):
    b, s, bd = dims
    d = D_MODEL
    rope_p, rope_s, rope_pt, rope_st = ropes
    w = w_in[j]
    c_qi, c_ki, c_wi = 3 * d, 3 * d + IDX_HEADS * IDX_DIM, 3 * d + IDX_HEADS * IDX_DIM + IDX_DIM
    w_idx = jnp.concatenate([w[:, c_qi:c_ki], w[:, c_ki:c_wi], w[:, c_ki:c_wi], w[:, c_wi:c_wi + IDX_HEADS],
                             jnp.zeros((d, LANES - IDX_HEADS), F32)], axis=1)[None]
    idx_w = w_idx.shape[2]
    w_k_t = w[:, d:2 * d].T
    w_v_t = w[:, 2 * d:3 * d].T
    w_ki2_t = jnp.concatenate([w[:, c_ki:c_wi], w[:, c_ki:c_wi]], axis=1).T

    q2_p = _q_pairs(xp, w_in, j, rope_p, tm=_row_tile(s, 512))
    k_pt, k_pt16, kd_pt = _project_t(xp, w_k_t, b, rope_pt, with_bf16=True, w_extra_t=w_ki2_t)
    v_pt = _project_t(xp, w_v_t, b)
    v_pg = _linear_heads(xp, w_in, j, 2 * d, N_GROUPS, tm=_row_tile(b * s, 512), leaf=False, groups=True)
    idx_p = _project(xp, w_idx, 0, 0, idx_w, rope_p, n_rope=5, tn=idx_w)
    t = _row_tile(s, 512)
    bias_p = _dsa_index_prompt(idx_p.reshape(b, s, idx_w), kd_pt, ch=t)
    o_p = _flash_attention(q2_p, k_pt16, v_pg, b, mode="dsa", t=t, bias=bias_p)

    q_s = _project(xs, w_in, j, 0, d, rope_s)
    k_s = _project(xs, w_in, j, d, d, rope_s)
    v_s = _project(xs, w_in, j, 2 * d, d)
    k_st, kd_st = _project_t(xs, w_k_t, 1, rope_st, w_extra_t=w_ki2_t)
    v_st = _project_t(xs, w_v_t, 1)
    idx_s = _project(xs, w_idx, 0, 0, idx_w, rope_s, n_rope=5, tn=idx_w)
    qi_s = idx_s[:, :IDX_HEADS * IDX_DIM].reshape(bd, IDX_HEADS, IDX_DIM)
    ki_s = idx_s[:, IDX_HEADS * IDX_DIM:IDX_HEADS * IDX_DIM + IDX_DIM]
    wi_s = idx_s[:, 5 * LANES:5 * LANES + IDX_HEADS].reshape(bd, IDX_HEADS, 1)
    n_pages = page_ids.shape[1]
    scores = _dsa_score_sample(qi_s, wi_s, ki_s.reshape(bd, 1, IDX_DIM), cache_kidx, page_ids)
    n_keys = past_len + 1
    bias_t = _select_bias(scores.reshape(bd, (n_pages + 1) * PAGE_SIZE).T, n_keys, min(IDX_TOPK_MAX, n_keys // 4))
    bias_s = bias_t.T.reshape(bd, n_pages + 1, PAGE_SIZE)
    o_s = _decode_attention(q_s, cache_k, cache_v, page_ids, k_s, v_s, mode="dsa",
                            n_pp=_pages_per_step(n_pages), bias=bias_s)
    leaves_p = (_heads_last(k_pt, DSA_HEADS), _heads_last(v_pt, DSA_HEADS), kd_pt[:, :IDX_DIM, :].transpose(0, 2, 1))
    leaves_s = (_heads_last(k_st, DSA_HEADS).reshape(bd, 1, DSA_HEADS, DSA_HEAD_DIM),
                _heads_last(v_st, DSA_HEADS).reshape(bd, 1, DSA_HEADS, DSA_HEAD_DIM),
                kd_st[0, :IDX_DIM, :].T.reshape(bd, 1, IDX_DIM))
    return (o_p.reshape(b * s, d), o_s) + leaves_p + leaves_s


def _gla_mixer(xp, xs, dims, state, j, w_in, w_gate_up, gate_b, norm_g):
    b, s, bd = dims
    d = D_MODEL
    c_g = 2 * GLA_DK + 2 * GLA_DV
    w_g = jnp.pad(w_in[j][:, c_g:c_g + GLA_GATE_RANK], ((0, 0), (0, LANES - GLA_GATE_RANK)))[None]
    wgu = jnp.pad(w_gate_up[j], ((0, LANES - GLA_GATE_RANK), (0, 0)))
    outs = []
    for x in (xp, xs):
        qk = _project(x, w_in, j, 0, 2 * GLA_DK)
        v = _project(x, w_in, j, 2 * GLA_DK, GLA_DV)
        r = _project(x, w_in, j, 2 * GLA_DK + GLA_DV, GLA_DV)
        glow = _project(x, w_g, 0, 0, LANES, tn=LANES)
        outs.append((qk, v, r, glow))
    qk, v, r, glow = outs[0]
    o_p, st_p = _gla_prompt(qk.reshape(b, s, -1), v.reshape(b, s, -1), r.reshape(b, s, -1), glow.reshape(b, s, -1),
                            wgu, gate_b[j], norm_g[j], c=GLA_HEAD_K, hps=GLA_HEADS)
    qk, v, r, glow = outs[1]
    n_state = state.shape[1]
    o_s, st_s = _gla_sample(qk, v, r, glow, wgu, gate_b[j], norm_g[j],
                            state.reshape((-1,) + state.shape[2:]), j * n_state)
    return o_p.reshape(b * s, d), o_s, st_p, st_s


def kernel(x_prompt, x_sample, cache_diff_k, cache_diff_v, cache_dsa_k, cache_dsa_v, cache_dsa_kidx, state_gla,
           page_table, ln_mix_g, ln_mix_b, ln_ffn_g, ln_ffn_b, ffn_w_gate_up, ffn_w_down, diff_w_in, diff_lambda,
           diff_subln_g, diff_w_out, dsa_w_in, dsa_w_out, gla_w_in, gla_w_gate_up, gla_gate_b, gla_norm_g,
           gla_w_out):
    b, s, d = x_prompt.shape
    bd, s_d, _ = x_sample.shape
    assert s_d == 1 and d == D_MODEL
    dims = (b, s, bd)
    n_pool, page = cache_diff_k.shape[1], cache_diff_k.shape[2]
    past_len = page_table.shape[1] * page
    xp = x_prompt.reshape(b * s, d)
    xs = x_sample.reshape(bd, d)
    pos_p = jnp.arange(s, dtype=jnp.int32)
    pos_s = jnp.full((bd,), past_len, dtype=jnp.int32)
    ropes = (_rope_tables(pos_p), _rope_tables(pos_s), _rope_tables_t(pos_p), _rope_tables_t(pos_s))

    def feature_major(c):
        perm = (0, 1) + tuple(range(3, c.ndim)) + (2,)
        return jnp.transpose(c, perm).reshape(c.shape[0] * c.shape[1], -1, page)

    cdk, csk, csv, csi = (feature_major(c) for c in (cache_diff_k, cache_dsa_k, cache_dsa_v, cache_dsa_kidx))
    cdv = cache_diff_v.reshape(cache_diff_v.shape[0] * n_pool, page * DIFF_HEADS, LANES)
    tm_p = _row_tile(b * s, 512)
    tm_s = _row_tile(bd, 512)
    tf = 256
    dk_stack = dv_stack = None
    dk_s, dv_s = [], []
    sk_p, sv_p, si_p, sk_s, sv_s, si_s = [], [], [], [], [], []
    gs_p, gs_s = [], []
    n_diff = diff_w_in.shape[0]
    for i in range(DEPTH):
        kind, j = i % 3, i // 3
        page_ids = page_table + j * n_pool
        if kind == 0:
            lam_init = 0.8 - 0.6 * math.exp(-0.3 * i)
            o_p, o_s, dk_stack, dv_stack, k_s, v_s = _diff_mixer(
                xp, xs, dims, cdk, cdv, page_ids, diff_w_in, j, diff_lambda[j], diff_subln_g[j], lam_init, ropes,
                dk_stack, dv_stack)
            dk_s.append(k_s)
            dv_s.append(v_s)
            w_out = diff_w_out
        elif kind == 1:
            o_p, o_s, k_p, v_p, ki_p, k_s, v_s, ki_s = _dsa_mixer(xp, xs, dims, csk, csv, csi, page_ids, dsa_w_in, j,
                                                                  ropes, past_len)
            sk_p.append(k_p)
            sv_p.append(v_p)
            si_p.append(ki_p)
            sk_s.append(k_s)
            sv_s.append(v_s)
            si_s.append(ki_s)
            w_out = dsa_w_out
        else:
            o_p, o_s, st_p, st_s = _gla_mixer(xp, xs, dims, state_gla, j, gla_w_in, gla_w_gate_up, gla_gate_b,
                                              gla_norm_g)
            gs_p.append(st_p)
            gs_s.append(st_s)
            w_out = gla_w_out
        xp = _linear_res_ln(o_p, w_out, j, xp, ln_mix_g, ln_mix_b, i, tm=tm_p)
        xs = _linear_res_ln(o_s, w_out, j, xs, ln_mix_g, ln_mix_b, i, tm=tm_s)
        xp = _ffn_ln(xp, ffn_w_gate_up, ffn_w_down, ln_ffn_g, ln_ffn_b, i, tm=_row_tile(b * s, 1024), tf=tf)
        xs = _ffn_ln(xs, ffn_w_gate_up, ffn_w_down, ln_ffn_g, ln_ffn_b, i, tm=tm_s, tf=tf)
    dk_p = dk_stack.reshape(n_diff, b, 2 * DIFF_HEADS, DIFF_HEAD_DIM, s).transpose(0, 1, 4, 2, 3)
    dv_p = dv_stack.reshape(n_diff, b, s, DIFF_HEADS, LANES)
    return (xp.reshape(b, s, d), xs.reshape(bd, 1, d),
            dk_p, dv_p, jnp.stack(dk_s), jnp.stack(dv_s),
            jnp.stack(sk_p), jnp.stack(sv_p), jnp.stack(si_p),
            jnp.stack(sk_s), jnp.stack(sv_s), jnp.stack(si_s),
            jnp.stack(gs_p), jnp.stack(gs_s))
```

```python
import functools
import math

import jax
import jax.numpy as jnp
from jax import lax
from jax.experimental import pallas as pl
from jax.experimental.pallas import tpu as pltpu

F32 = jnp.float32
BF16 = jnp.bfloat16

D_MODEL = 1024
DEPTH = 4
PAGE_SIZE = 128
DIFF_HEADS = 8
DIFF_HEAD_DIM = 64
DSA_HEADS = 16
DSA_HEAD_DIM = 64
IDX_HEADS = 8
IDX_DIM = 64
IDX_TOPK_MAX = 256
GLA_HEADS = 4
GLA_DK = D_MODEL // 2
GLA_DV = D_MODEL
GLA_HEAD_K = GLA_DK // GLA_HEADS
GLA_HEAD_V = GLA_DV // GLA_HEADS
GLA_GATE_RANK = 16
GLA_TAU = 16.0
D_FF = (((8 * D_MODEL + 2) // 3 + 255) // 256) * 256
ROPE_THETA = 10000.0
LN_EPS = 1e-5
DEEPNORM_ALPHA = (2.0 * DEPTH) ** 0.25

LANES = 128
SUBLANES = 8
VMEM_LIMIT = 56 * 1024 * 1024
NEG = -1e30
INT_MIN = -2 ** 31

N_GROUPS = D_MODEL // LANES


def _cparams(sem):
    return pltpu.CompilerParams(dimension_semantics=sem, vmem_limit_bytes=VMEM_LIMIT)


def _bdot(a, b):
    return jnp.dot(a.astype(BF16), b.astype(BF16), preferred_element_type=F32)


def _bdot_nt(a, b):
    return lax.dot_general(a.astype(BF16), b.astype(BF16), (((1,), (1,)), ((), ())),
                           preferred_element_type=F32)


def _bdot_tn(a, b):
    return lax.dot_general(a.astype(BF16), b.astype(BF16), (((0,), (0,)), ((), ())),
                           preferred_element_type=F32)


def _layer_norm_rows(z, g, b):
    mu = jnp.mean(z, axis=-1, keepdims=True)
    zc = z - mu
    var = jnp.mean(zc * zc, axis=-1, keepdims=True)
    return zc * lax.rsqrt(var + LN_EPS) * g + b


def _sigmoid(x):
    return 1.0 / (1.0 + jnp.exp(-x))


def _rope_tables(pos):
    d = DIFF_HEAD_DIM
    inv = ROPE_THETA ** (-jnp.arange(0, d, 2, dtype=F32) / d)
    ang = pos.astype(F32)[:, None] * inv[None, :]
    cos = jnp.cos(ang)
    sin = jnp.sin(ang)
    return (jnp.concatenate([cos, cos, cos, cos], axis=-1),
            jnp.concatenate([-sin, sin, -sin, sin], axis=-1))


def _rope_apply(y, cos, sin, first_half):
    partner = jnp.where(first_half, pltpu.roll(y, 96, 1), pltpu.roll(y, 32, 1))
    return y * cos + partner * sin


def _linear_kernel(x_ref, w_ref, *rest, n_rope, single_tile):
    if n_rope:
        cos_ref, sin_ref, o_ref = rest
    else:
        (o_ref,) = rest
    acc = _bdot(x_ref[...], w_ref[...])
    tm, tn = acc.shape
    if not n_rope:
        o_ref[...] = acc
        return

    def roped(n_chunks):
        cos = cos_ref[...]
        sin = sin_ref[...]
        lane = lax.broadcasted_iota(jnp.int32, (tm, LANES), 1)
        first_half = (lane % DIFF_HEAD_DIM) < (DIFF_HEAD_DIM // 2)
        for c in range(tn // LANES):
            sl = slice(c * LANES, (c + 1) * LANES)
            o_ref[:, sl] = _rope_apply(acc[:, sl], cos, sin, first_half) if c < n_chunks else acc[:, sl]

    if single_tile:
        roped(n_rope)
        return
    j = pl.program_id(1)
    pl.when(j < n_rope)(lambda: roped(tn // LANES))

    @pl.when(j >= n_rope)
    def _():
        o_ref[...] = acc


def _linear(x, w, layer, col0, width, *, tm, tn, rope=None, n_rope=None):
    n, k = x.shape
    assert n % tm == 0 and width % tn == 0 and col0 % tn == 0
    nj = width // tn
    if rope is None:
        n_rope = 0
    elif n_rope is None:
        n_rope = nj if nj > 1 else tn // LANES
    in_specs = [pl.BlockSpec((tm, k), lambda i, j: (i, 0)),
                pl.BlockSpec((None, k, tn), lambda i, j: (layer, 0, col0 // tn + j))]
    args = [x, w]
    if n_rope:
        p_blocks = rope[0].shape[0] // tm
        assert rope[0].shape[0] % tm == 0
        spec = pl.BlockSpec((tm, LANES), lambda i, j: (i % p_blocks, 0))
        in_specs += [spec, spec]
        args += [rope[0], rope[1]]
    return pl.pallas_call(
        functools.partial(_linear_kernel, n_rope=n_rope, single_tile=nj == 1),
        out_shape=jax.ShapeDtypeStruct((n, width), F32),
        grid=(n // tm, nj),
        in_specs=in_specs,
        out_specs=pl.BlockSpec((tm, tn), lambda i, j: (i, j)),
        compiler_params=_cparams(("parallel", "arbitrary")),
        name="linear",
    )(*args)


def _rope_tables_t(pos):
    cos, sin = _rope_tables(pos)
    return cos.T, sin.T


def _linear_t_kernel(w_ref, x_ref, *rest, rope, with_bf16, extra, aliased):
    rest = list(rest)
    we_ref = rest.pop(0) if extra else None
    cos_ref, sin_ref = (rest.pop(0), rest.pop(0)) if rope else (None, None)
    if aliased:
        rest.pop(0)
    o_ref = rest.pop(0)
    o16_ref = rest.pop(0) if with_bf16 else None
    oe_ref = rest.pop(0) if extra else None
    tt = x_ref.shape[0]
    xb = x_ref[...].astype(BF16)

    def roped(y):
        row = lax.broadcasted_iota(jnp.int32, (LANES, tt), 0)
        first_half = (row % DIFF_HEAD_DIM) < (DIFF_HEAD_DIM // 2)
        partner = jnp.where(first_half, pltpu.roll(y, LANES - DIFF_HEAD_DIM // 2, 0),
                            pltpu.roll(y, DIFF_HEAD_DIM // 2, 0))
        return y * cos_ref[...] + partner * sin_ref[...]

    acc = _bdot_nt(w_ref[...], xb)
    for c in range(acc.shape[0] // LANES):
        sl = slice(c * LANES, (c + 1) * LANES)
        y = roped(acc[sl, :]) if rope else acc[sl, :]
        o_ref[sl, :] = y
        if with_bf16:
            o16_ref[sl, :] = y.astype(BF16)

    if extra:
        @pl.when(pl.program_id(2) == 0)
        def _():
            ye = _bdot_nt(we_ref[...], xb)
            oe_ref[...] = roped(ye) if rope else ye


def _linear_t(x, w_t, nb, *, tc, tt, rope=None, with_bf16=False, w_extra_t=None, stack=None):
    n, k = x.shape
    c = w_t.shape[0]
    s = n // nb
    assert n == nb * s and s % tt == 0 and c % tc == 0 and tc % LANES == 0
    nt = s // tt
    extra = w_extra_t is not None
    n_layers, layer, prev = stack if stack is not None else (1, 0, None)
    in_specs = [pl.BlockSpec((tc, k), lambda b, ti, j: (j, 0)),
                pl.BlockSpec((tt, k), lambda b, ti, j: (b * nt + ti, 0))]
    args = [w_t, x]
    if extra:
        assert w_extra_t.shape == (LANES, k)
        in_specs.append(pl.BlockSpec((LANES, k), lambda b, ti, j: (0, 0)))
        args.append(w_extra_t)
    if rope is not None:
        p_blocks = rope[0].shape[1] // tt
        spec = pl.BlockSpec((LANES, tt), lambda b, ti, j: (0, ti % p_blocks))
        in_specs += [spec, spec]
        args += [rope[0], rope[1]]
    aliases = {}
    if prev is not None:
        assert prev.shape == (n_layers * nb, c, s)
        aliases = {len(args): 0}
        in_specs.append(pl.BlockSpec(memory_space=pl.ANY))
        args.append(prev)
    out_spec = pl.BlockSpec((None, tc, tt), lambda b, ti, j: (b, j, ti))
    out_shape = [jax.ShapeDtypeStruct((n_layers * nb, c, s), F32)]
    out_specs = [pl.BlockSpec((None, tc, tt), lambda b, ti, j: (layer * nb + b, j, ti))]
    if with_bf16:
        out_shape.append(jax.ShapeDtypeStruct((nb, c, s), BF16))
        out_specs.append(out_spec)
    if extra:
        out_shape.append(jax.ShapeDtypeStruct((nb, LANES, s), F32))
        out_specs.append(pl.BlockSpec((None, LANES, tt), lambda b, ti, j: (b, 0, ti)))
    outs = pl.pallas_call(
        functools.partial(_linear_t_kernel, rope=rope is not None, with_bf16=with_bf16, extra=extra,
                          aliased=prev is not None),
        out_shape=out_shape,
        grid=(nb, nt, c // tc),
        in_specs=in_specs,
        out_specs=out_specs,
        input_output_aliases=aliases,
        compiler_params=_cparams(("parallel", "parallel", "arbitrary")),
        name="linear_t",
    )(*args)
    return tuple(outs) if len(outs) > 1 else outs[0]


def _linear_heads_kernel(x_ref, w_ref, *o_refs, leaf, groups, aliased):
    xb = x_ref[...].astype(BF16)
    o_refs = list(o_refs)
    if aliased:
        o_refs.pop(0)
    leaf_ref = o_refs.pop(0) if leaf else None
    grp_ref = o_refs.pop(0) if groups else None
    for h in range(w_ref.shape[1] // LANES):
        y = jnp.dot(xb, w_ref[:, h * LANES:(h + 1) * LANES].astype(BF16), preferred_element_type=F32)
        if leaf:
            leaf_ref[:, h, :] = y
        if groups:
            grp_ref[h] = y.astype(BF16)


def _linear_heads(x, w, layer, col0, n_heads, *, tm, leaf=True, groups=False, stack=None):
    n, k = x.shape
    width = n_heads * LANES
    assert n % tm == 0 and col0 % width == 0 and (leaf or groups)
    n_slabs, slab, prev = stack if stack is not None else (1, 0, None)
    in_specs = [pl.BlockSpec((tm, k), lambda i: (i, 0)),
                pl.BlockSpec((None, k, width), lambda i: (layer, 0, col0 // width))]
    args = [x, w]
    aliases = {}
    if prev is not None:
        assert leaf and prev.shape == (n_slabs * n, n_heads, LANES)
        aliases = {len(args): 0}
        in_specs.append(pl.BlockSpec(memory_space=pl.ANY))
        args.append(prev)
    out_shape, out_specs = [], []
    if leaf:
        out_shape.append(jax.ShapeDtypeStruct((n_slabs * n, n_heads, LANES), F32))
        out_specs.append(pl.BlockSpec((tm, n_heads, LANES), lambda i: (slab * (n // tm) + i, 0, 0)))
    if groups:
        out_shape.append(jax.ShapeDtypeStruct((n_heads, n, LANES), BF16))
        out_specs.append(pl.BlockSpec((n_heads, tm, LANES), lambda i: (0, i, 0)))
    outs = pl.pallas_call(
        functools.partial(_linear_heads_kernel, leaf=leaf, groups=groups, aliased=prev is not None),
        out_shape=out_shape,
        grid=(n // tm,),
        in_specs=in_specs,
        out_specs=out_specs,
        input_output_aliases=aliases,
        compiler_params=_cparams(("parallel",)),
        name="linear_heads",
    )(*args)
    return outs[0] if len(outs) == 1 else tuple(outs)


LOG2E = 1.4426950408889634


def _q_pairs_kernel(x_ref, w_ref, cos_ref, sin_ref, o_ref, *, scale):
    acc = _bdot(x_ref[...], w_ref[...])
    tm = acc.shape[0]
    cos = cos_ref[...]
    sin = sin_ref[...]
    lane = lax.broadcasted_iota(jnp.int32, (tm, LANES), 1)
    first_half = (lane % DIFF_HEAD_DIM) < (DIFF_HEAD_DIM // 2)
    lo = lane < DIFF_HEAD_DIM
    for g in range(N_GROUPS):
        y = _rope_apply(acc[:, g * LANES:(g + 1) * LANES], cos, sin, first_half) * scale
        o_ref[g, 0] = jnp.where(lo, y, 0.0).astype(BF16)
        o_ref[g, 1] = jnp.where(lo, 0.0, y).astype(BF16)


def _q_pairs(x, w, layer, rope, *, tm):
    n, k = x.shape
    d = D_MODEL
    assert n % tm == 0 and rope[0].shape[0] % tm == 0
    p_blocks = rope[0].shape[0] // tm
    rspec = pl.BlockSpec((tm, LANES), lambda i: (i % p_blocks, 0))
    return pl.pallas_call(
        functools.partial(_q_pairs_kernel, scale=DIFF_HEAD_DIM ** -0.5 * LOG2E),
        out_shape=jax.ShapeDtypeStruct((N_GROUPS, 2, n, LANES), BF16),
        grid=(n // tm,),
        in_specs=[pl.BlockSpec((tm, k), lambda i: (i, 0)),
                  pl.BlockSpec((None, k, d), lambda i: (layer, 0, 0)), rspec, rspec],
        out_specs=pl.BlockSpec((N_GROUPS, 2, tm, LANES), lambda i: (0, 0, i, 0)),
        compiler_params=_cparams(("parallel",)),
        name="q_pairs",
    )(x, w, rope[0], rope[1])


def _res_ln_kernel(x_ref, w_ref, r_ref, g_ref, b_ref, o_ref):
    y = _bdot(x_ref[...], w_ref[...])
    z = DEEPNORM_ALPHA * r_ref[...] + y
    o_ref[...] = _layer_norm_rows(z, g_ref[...], b_ref[...])


def _linear_res_ln(x, w, layer, resid, g, b, ln_layer, *, tm):
    n, k = x.shape
    d = w.shape[2]
    assert n % tm == 0
    row = lambda i: (i, 0)
    ln_row = lambda i: (ln_layer, 0, 0)
    return pl.pallas_call(
        _res_ln_kernel,
        out_shape=jax.ShapeDtypeStruct((n, d), F32),
        grid=(n // tm,),
        in_specs=[pl.BlockSpec((tm, k), row), pl.BlockSpec((None, k, d), lambda i: (layer, 0, 0)),
                  pl.BlockSpec((tm, d), row),
                  pl.BlockSpec((None, 1, d), ln_row), pl.BlockSpec((None, 1, d), ln_row)],
        out_specs=pl.BlockSpec((tm, d), row),
        compiler_params=_cparams(("parallel",)),
        name="out_proj_ln",
    )(x, w, resid, g.reshape(-1, 1, d), b.reshape(-1, 1, d))


def _ffn_kernel(x_ref, wg_ref, wu_ref, wd_ref, g_ref, b_ref, o_ref, acc_ref, xb_ref):
    f = pl.program_id(1)

    @pl.when(f == 0)
    def _():
        acc_ref[...] = jnp.zeros_like(acc_ref)
        xb_ref[...] = x_ref[...].astype(BF16)

    xb = xb_ref[...]
    gate = jnp.dot(xb, wg_ref[...].astype(BF16), preferred_element_type=F32)
    up = jnp.dot(xb, wu_ref[...].astype(BF16), preferred_element_type=F32)
    h = gate * _sigmoid(gate) * up
    acc_ref[...] += _bdot(h, wd_ref[...])

    @pl.when(f == pl.num_programs(1) - 1)
    def _():
        z = DEEPNORM_ALPHA * x_ref[...] + acc_ref[...]
        o_ref[...] = _layer_norm_rows(z, g_ref[...], b_ref[...])


def _ffn_ln(x, w_gate_up, w_down, g, b, layer, *, tm, tf):
    n, d = x.shape
    dff = w_down.shape[1]
    assert n % tm == 0 and dff % tf == 0
    nf = dff // tf
    ln_row = lambda i, f: (layer, 0, 0)
    return pl.pallas_call(
        _ffn_kernel,
        out_shape=jax.ShapeDtypeStruct((n, d), F32),
        grid=(n // tm, nf),
        in_specs=[pl.BlockSpec((tm, d), lambda i, f: (i, 0)),
                  pl.BlockSpec((None, d, tf), lambda i, f: (layer, 0, f)),
                  pl.BlockSpec((None, d, tf), lambda i, f: (layer, 0, nf + f)),
                  pl.BlockSpec((None, tf, d), lambda i, f: (layer, f, 0)),
                  pl.BlockSpec((None, 1, d), ln_row), pl.BlockSpec((None, 1, d), ln_row)],
        out_specs=pl.BlockSpec((tm, d), lambda i, f: (i, 0)),
        scratch_shapes=[pltpu.VMEM((tm, d), F32), pltpu.VMEM((tm, d), BF16)],
        compiler_params=_cparams(("parallel", "arbitrary")),
        name="ffn_ln",
    )(x, w_gate_up, w_gate_up, w_down, g.reshape(-1, 1, d), b.reshape(-1, 1, d))


def _diff_lambda(lam_ref, lam_init):
    l = lam_ref[...]
    a = jnp.sum(l[0:1] * l[1:2], axis=-1, keepdims=True)
    c = jnp.sum(l[2:3] * l[3:4], axis=-1, keepdims=True)
    return jnp.exp(a) - jnp.exp(c) + lam_init


def _rms_rows(o, g):
    return o * lax.rsqrt(jnp.mean(o * o, axis=-1, keepdims=True) + LN_EPS) * g


def _flash_kernel(qi_ref, ki_ref, *refs, mode, tq, tk, lam_init):
    if mode == "diff":
        q_ref, k_ref, v_ref, lam_ref, g_ref, o_ref, m_sc, l_sc, acc_sc = refs
    else:
        q_ref, k_ref, v_ref, bias_ref, o_ref, m_sc, l_sc, acc_sc = refs
    step = pl.program_id(1)
    qi = qi_ref[step]
    ki = ki_ref[step]
    k_last = (qi * tq + (tq - 1)) // tk

    @pl.when(ki == 0)
    def _():
        m_sc[...] = jnp.full_like(m_sc, NEG)
        l_sc[...] = jnp.zeros_like(l_sc)
        acc_sc[...] = jnp.zeros_like(acc_sc)

    def block_update(causal):
        if causal:
            row = lax.broadcasted_iota(jnp.int32, (2 * tq, tk), 0)
            col = lax.broadcasted_iota(jnp.int32, (2 * tq, tk), 1)
            allowed = col <= jnp.where(row >= tq, row - tq, row) + (qi * tq) % tk

        def group(g, carry):
            q2 = q_ref[g].reshape(2 * tq, LANES)
            s = jnp.dot(q2, k_ref[g], preferred_element_type=F32)
            if mode == "dsa":
                bias = bias_ref[...]
                s = s + jnp.concatenate([bias, bias], axis=0)
            if causal:
                s = jnp.where(allowed, s, NEG)
            m_prev = m_sc[g]
            m_next = jnp.maximum(m_prev, jnp.max(s, axis=-1, keepdims=True))
            p = jnp.exp2(s - jnp.concatenate([m_next] * (tk // LANES), axis=-1))
            alpha = jnp.exp2(m_prev - m_next)
            l_sc[g] = alpha * l_sc[g] + jnp.sum(p, axis=-1, keepdims=True)
            acc_sc[g] = alpha * acc_sc[g] + jnp.dot(p.astype(BF16), v_ref[g], preferred_element_type=F32)
            m_sc[g] = m_next
            return carry

        lax.fori_loop(0, N_GROUPS, group, 0, unroll=4)

    if mode == "diff":
        pl.when(ki < k_last)(lambda: block_update(False))
        pl.when(ki == k_last)(lambda: block_update(True))
    else:
        block_update(False)

    @pl.when(ki == k_last)
    def _():
        lane = lax.broadcasted_iota(jnp.int32, (1, LANES), 1)
        lo = lane < DIFF_HEAD_DIM
        if mode == "diff":
            lam = _diff_lambda(lam_ref, lam_init)
            gain = g_ref[...] * (1.0 - lam_init)
        for g in range(N_GROUPS):
            o_lo = acc_sc[g, :tq, :] / l_sc[g, :tq, :]
            o_hi = acc_sc[g, tq:, :] / l_sc[g, tq:, :]
            if mode == "diff":
                o_ref[:, g * LANES:(g + 1) * LANES] = _rms_rows(o_lo - lam * o_hi, gain)
            else:
                o_ref[:, g * LANES:(g + 1) * LANES] = jnp.where(lo, o_lo, o_hi)


def _flash_attention(q2, k_t, v_g, b, *, mode, tq, tk, lam_params=None, subln_g=None, lam_init=0.0, bias=None):
    n = q2.shape[2]
    s = n // b
    d = D_MODEL
    assert s % tq == 0 and s % tk == 0 and tk % tq == 0 and n == b * s
    nq, nk = s // tq, s // tk
    k4 = k_t.reshape(b, N_GROUPS, LANES, s)
    pairs = [(qb, kb) for qb in range(nq) for kb in range((qb * tq + tq - 1) // tk + 1)]
    qi_of = jnp.asarray([p[0] for p in pairs], dtype=jnp.int32)
    ki_of = jnp.asarray([p[1] for p in pairs], dtype=jnp.int32)
    qspec = pl.BlockSpec((N_GROUPS, 2, tq, LANES), lambda bi, st, qo, ko: (0, 0, bi * nq + qo[st], 0))
    kspec = pl.BlockSpec((None, N_GROUPS, LANES, tk), lambda bi, st, qo, ko: (bi, 0, 0, ko[st]))
    vspec = pl.BlockSpec((N_GROUPS, tk, LANES), lambda bi, st, qo, ko: (0, bi * nk + ko[st], 0))
    in_specs = [qspec, kspec, vspec]
    args = [q2, k4, v_g]
    if mode == "diff":
        in_specs += [pl.BlockSpec((4, DIFF_HEAD_DIM), lambda bi, st, qo, ko: (0, 0)),
                     pl.BlockSpec((1, LANES), lambda bi, st, qo, ko: (0, 0))]
        args += [lam_params, subln_g.reshape(1, LANES)]
    else:
        assert bias.shape == (b, nk, s, tk)
        in_specs += [pl.BlockSpec((None, None, tq, tk), lambda bi, st, qo, ko: (bi, ko[st], qo[st], 0))]
        args += [bias]
    stat = pltpu.VMEM((N_GROUPS, 2 * tq, LANES), F32)
    return pl.pallas_call(
        functools.partial(_flash_kernel, mode=mode, tq=tq, tk=tk, lam_init=lam_init),
        out_shape=jax.ShapeDtypeStruct((n, d), F32),
        grid_spec=pltpu.PrefetchScalarGridSpec(
            num_scalar_prefetch=2,
            grid=(b, len(pairs)),
            in_specs=in_specs,
            out_specs=pl.BlockSpec((tq, d), lambda bi, st, qo, ko: (bi * nq + qo[st], 0)),
            scratch_shapes=[stat, stat, stat]),
        compiler_params=_cparams(("parallel", "arbitrary")),
        name="flash_" + mode,
    )(qi_of, ki_of, *args)


def _decode_kernel(pt_ref, q_ref, *refs, mode, n_pp, n_pages, scale, lam_init):
    del pt_ref
    k_refs = refs[:n_pp]
    v_refs = refs[n_pp:2 * n_pp]
    knew_ref, vnew_ref = refs[2 * n_pp:2 * n_pp + 2]
    rest = refs[2 * n_pp + 2:]
    if mode == "diff":
        lam_ref, g_ref, o_ref, qm_sc, m_sc, l_sc, acc_sc, e_sc = rest
    else:
        bias_ref, o_ref, qm_sc, m_sc, l_sc, acc_sc = rest
    step = pl.program_id(1)
    n_sub = 2 * N_GROUPS
    row = lax.broadcasted_iota(jnp.int32, (n_sub, D_MODEL), 0)
    lane = lax.broadcasted_iota(jnp.int32, (n_sub, D_MODEL), 1)

    @pl.when(step == 0)
    def _():
        qm_sc[...] = jnp.where(lane // DIFF_HEAD_DIM == row, q_ref[...] * scale, 0.0)
        m_sc[...] = jnp.full_like(m_sc, NEG)
        l_sc[...] = jnp.zeros_like(l_sc)
        acc_sc[...] = jnp.zeros_like(acc_sc)
        if mode == "diff":
            tok = lax.broadcasted_iota(jnp.int32, (PAGE_SIZE, D_MODEL), 0)
            col = lax.broadcasted_iota(jnp.int32, (PAGE_SIZE, D_MODEL), 1)
            e_sc[...] = jnp.where(col // DIFF_HEADS == tok, 1.0, 0.0).astype(BF16)

    qm = qm_sc[...]
    s_parts = []
    for i in range(n_pp):
        s = _bdot(qm, k_refs[i][...])
        if mode == "dsa":
            s = s + bias_ref[pl.ds(step * n_pp + i, 1), :]
        s_parts.append(s)
    s_all = jnp.concatenate(s_parts, axis=-1)
    m_prev = m_sc[...]
    m_new = jnp.maximum(m_prev, jnp.max(s_all, axis=-1, keepdims=True))
    alpha = jnp.exp(m_prev - m_new)
    p_all = jnp.exp(s_all - m_new)
    l_sc[...] = alpha * l_sc[...] + jnp.sum(p_all, axis=-1, keepdims=True)
    pv = None
    for i in range(n_pp):
        p_i = p_all[:, i * PAGE_SIZE:(i + 1) * PAGE_SIZE]
        if mode == "diff":
            p_rows = jnp.where(lane % DIFF_HEADS == row // 2,
                               jnp.dot(p_i.astype(BF16), e_sc[...], preferred_element_type=F32), 0.0)
            part = _bdot(p_rows, v_refs[i][...])
        else:
            part = _bdot_nt(p_i, v_refs[i][...])
        pv = part if pv is None else pv + part
    acc_sc[...] = alpha * acc_sc[...] + pv
    m_sc[...] = m_new

    @pl.when(step == pl.num_programs(1) - 1)
    def _():
        s_new = jnp.sum(qm * knew_ref[...], axis=-1, keepdims=True)
        if mode == "dsa":
            s_new = s_new + bias_ref[n_pages:n_pages + 1, 0:1]
        m_prev = m_sc[...]
        m_fin = jnp.maximum(m_prev, s_new)
        alpha = jnp.exp(m_prev - m_fin)
        p_new = jnp.exp(s_new - m_fin)
        l_fin = alpha * l_sc[...] + p_new
        if mode == "diff":
            row_h = lax.broadcasted_iota(jnp.int32, (n_sub, LANES), 0) // 2
            v_rows = jnp.zeros((n_sub, LANES), F32)
            for h in range(DIFF_HEADS):
                v_rows = jnp.where(row_h == h, vnew_ref[:, h * LANES:(h + 1) * LANES], v_rows)
            o16 = (alpha * acc_sc[...] + p_new * v_rows) / l_fin
            lam = _diff_lambda(lam_ref, lam_init)
            gain = g_ref[...] * (1.0 - lam_init)
            for h in range(DIFF_HEADS):
                o_h = o16[2 * h:2 * h + 1, :] - lam * o16[2 * h + 1:2 * h + 2, :]
                o_ref[:, h * LANES:(h + 1) * LANES] = _rms_rows(o_h, gain)
        else:
            o16 = (alpha * acc_sc[...] + p_new * vnew_ref[...]) / l_fin
            o_ref[...] = jnp.sum(jnp.where(lane // DSA_HEAD_DIM == row, o16, 0.0), axis=0, keepdims=True)


def _decode_attention(q, cache_k, cache_v, page_ids, k_new, v_new, *, mode, n_pp,
                      lam_params=None, subln_g=None, lam_init=0.0, bias=None):
    bd, d = q.shape
    n_pages = page_ids.shape[1]
    assert n_pages % n_pp == 0 and cache_k.shape[1:] == (d, PAGE_SIZE) and cache_v.shape[1:] == (d, PAGE_SIZE)
    n_steps = n_pages // n_pp
    row_spec = pl.BlockSpec((None, 1, d), lambda b, s, pt: (b, 0, 0))

    def page_spec(i):
        return pl.BlockSpec((None, d, PAGE_SIZE), lambda b, s, pt: (pt[b * n_pages + s * n_pp + i], 0, 0))

    in_specs = [row_spec] + [page_spec(i) for i in range(n_pp)] * 2 + [row_spec, row_spec]
    args = [q.reshape(bd, 1, d)] + [cache_k] * n_pp + [cache_v] * n_pp + [k_new.reshape(bd, 1, d), v_new.reshape(bd, 1, d)]
    n_sub = 2 * N_GROUPS
    scratch = [pltpu.VMEM((n_sub, d), F32), pltpu.VMEM((n_sub, 1), F32), pltpu.VMEM((n_sub, 1), F32)]
    if mode == "diff":
        in_specs += [pl.BlockSpec((4, DIFF_HEAD_DIM), lambda b, s, pt: (0, 0)),
                     pl.BlockSpec((1, LANES), lambda b, s, pt: (0, 0))]
        args += [lam_params, subln_g.reshape(1, LANES)]
        scratch += [pltpu.VMEM((n_sub, LANES), F32), pltpu.VMEM((PAGE_SIZE, d), BF16)]
    else:
        in_specs += [pl.BlockSpec((None, n_pages + 1, PAGE_SIZE), lambda b, s, pt: (b, 0, 0))]
        args += [bias]
        scratch += [pltpu.VMEM((n_sub, d), F32)]
    out = pl.pallas_call(
        functools.partial(_decode_kernel, mode=mode, n_pp=n_pp, n_pages=n_pages,
                          scale=DIFF_HEAD_DIM ** -0.5, lam_init=lam_init),
        out_shape=jax.ShapeDtypeStruct((bd, 1, d), F32),
        grid_spec=pltpu.PrefetchScalarGridSpec(
            num_scalar_prefetch=1,
            grid=(bd, n_steps),
            in_specs=in_specs,
            out_specs=row_spec,
            scratch_shapes=scratch),
        compiler_params=_cparams(("parallel", "arbitrary")),
        name="decode_" + mode,
    )(page_ids.reshape(-1), *args)
    return out.reshape(bd, d)


def _sortable_keys(score):
    bits = lax.bitcast_convert_type(score, jnp.int32)
    key = jnp.where(bits < 0, bits ^ jnp.int32(0x7FFFFFFF), bits)
    return jnp.where(score == 0.0, 0, key)


def _kth_threshold(key_ref, nch, ch, ksel):
    def count_ge(cand):
        def body(c, acc):
            blk = key_ref[pl.ds(pl.multiple_of(c * ch, ch), ch), :]
            hit = jnp.where(blk >= cand, 1, 0).astype(jnp.int32)
            return acc + jnp.sum(hit.reshape(ch // SUBLANES, SUBLANES, LANES), axis=0)

        acc = lax.fori_loop(0, nch, body, jnp.zeros((SUBLANES, LANES), jnp.int32))
        return jnp.sum(acc, axis=0, keepdims=True)

    def bit_body(i, t):
        cand = t ^ lax.shift_left(jnp.int32(1), jnp.asarray(31 - i, dtype=jnp.int32))
        return jnp.where(count_ge(cand) >= ksel, cand, t)

    t = lax.fori_loop(0, 32, bit_body, jnp.full((1, LANES), INT_MIN, jnp.int32))
    n_above = count_ge(t + 1)
    need = ksel - n_above
    surplus = jnp.logical_and(t > INT_MIN, count_ge(t) - n_above > need)
    return t, need.astype(F32), surplus


def _tri_ones(n):
    row = lax.broadcasted_iota(jnp.int32, (n, n), 0)
    col = lax.broadcasted_iota(jnp.int32, (n, n), 1)
    return jnp.where(col <= row, 1.0, 0.0).astype(BF16)


def _select_chunk(key, t, need, carry, tri):
    eq = key == t
    pref = jnp.dot(tri, jnp.where(eq, 1.0, 0.0).astype(BF16), preferred_element_type=F32) + carry
    sel = jnp.logical_or(key > t, jnp.logical_and(eq, pref <= need))
    sel = jnp.logical_and(sel, key > INT_MIN)
    return sel, pref[key.shape[0] - 1:, :]


def _dsa_index_prompt_kernel(idx_ref, kd_ref, o_ref, key_sc, *, tq, ch, ksel):
    qi = pl.program_id(1)
    n_ch_total = o_ref.shape[0]
    nch = (qi * tq + tq + ch - 1) // ch
    qblk = idx_ref[...]
    lane = lax.broadcasted_iota(jnp.int32, (1, LANES), 1)
    lo = lane < IDX_DIM
    parts = []
    for h in range(IDX_HEADS):
        grp = qblk[:, (h // 2) * LANES:(h // 2 + 1) * LANES]
        parts.append(jnp.where(lo if h % 2 == 0 else jnp.logical_not(lo), grp, 0.0).astype(BF16))
    qstack = jnp.concatenate(parts, axis=0)
    w = qblk[:, 5 * LANES:6 * LANES] * (IDX_HEADS ** -0.5 * IDX_DIM ** -0.5)
    w_cols = [w[:, h:h + 1] for h in range(IDX_HEADS)]
    qpos = qi * tq + lane

    def score_body(c, _):
        base = pl.multiple_of(c * ch, ch)
        r = _bdot(qstack, kd_ref[:, pl.ds(base, ch)])
        score = jnp.zeros((tq, ch), F32)
        for h in range(IDX_HEADS):
            score = score + w_cols[h] * jnp.maximum(r[h * tq:(h + 1) * tq, :], 0.0)
        kpos = base + lax.broadcasted_iota(jnp.int32, (ch, tq), 0)
        key_sc[pl.ds(base, ch), :] = jnp.where(kpos <= qpos, _sortable_keys(jnp.transpose(score)), INT_MIN)
        return 0

    lax.fori_loop(0, nch, score_body, 0)
    t, need, surplus = _kth_threshold(key_sc, nch, ch, ksel)
    any_surplus = jnp.max(jnp.where(surplus, 1, 0)) > 0

    @pl.when(any_surplus)
    def _():
        tri = _tri_ones(ch)

        def out_body(c, carry):
            base = pl.multiple_of(c * ch, ch)
            sel, carry = _select_chunk(key_sc[pl.ds(base, ch), :], t, need, carry, tri)
            o_ref[c] = jnp.transpose(jnp.where(sel, 0.0, NEG))
            return carry

        lax.fori_loop(0, nch, out_body, jnp.zeros((1, LANES), F32))

    @pl.when(jnp.logical_not(any_surplus))
    def _():
        def out_body(c, _):
            key = key_sc[pl.ds(pl.multiple_of(c * ch, ch), ch), :]
            sel = jnp.logical_and(key >= t, key > INT_MIN)
            o_ref[c] = jnp.transpose(jnp.where(sel, 0.0, NEG))
            return 0

        lax.fori_loop(0, nch, out_body, 0)

    def fill_body(c, _):
        o_ref[c] = jnp.full((tq, ch), NEG, F32)
        return 0

    lax.fori_loop(nch, n_ch_total, fill_body, 0)


def _dsa_index_prompt(idx, kd_t, *, ch):
    b, s, w = idx.shape
    tq = LANES
    assert s % ch == 0 and ch % tq == 0
    ksel = min(IDX_TOPK_MAX, s // 4)
    return pl.pallas_call(
        functools.partial(_dsa_index_prompt_kernel, tq=tq, ch=ch, ksel=ksel),
        out_shape=jax.ShapeDtypeStruct((b, s // ch, s, ch), F32),
        grid=(b, s // tq),
        in_specs=[pl.BlockSpec((None, tq, w), lambda bi, qi: (bi, qi, 0)),
                  pl.BlockSpec((None, LANES, s), lambda bi, qi: (bi, 0, 0))],
        out_specs=pl.BlockSpec((None, s // ch, tq, ch), lambda bi, qi: (bi, 0, qi, 0)),
        scratch_shapes=[pltpu.VMEM((s, LANES), jnp.int32)],
        compiler_params=_cparams(("parallel", "arbitrary")),
        name="dsa_index_prompt",
    )(idx, kd_t)


def _dsa_score_sample_kernel(pt_ref, q_ref, w_ref, knew_ref, *refs, n_pages):
    del pt_ref
    page_refs = refs[:n_pages]
    o_ref = refs[n_pages]
    q = q_ref[...]
    w = w_ref[...] * (IDX_HEADS ** -0.5 * IDX_DIM ** -0.5)
    for p in range(n_pages):
        s = _bdot(q, page_refs[p][...])
        o_ref[p:p + 1, :] = jnp.sum(w * jnp.maximum(s, 0.0), axis=0, keepdims=True)
    s_new = jnp.sum(q * knew_ref[...], axis=-1, keepdims=True)
    sc_new = jnp.sum(w * jnp.maximum(s_new, 0.0), axis=0, keepdims=True)
    lane = lax.broadcasted_iota(jnp.int32, (1, PAGE_SIZE), 1)
    o_ref[n_pages:n_pages + 1, :] = jnp.where(lane == 0, sc_new, NEG)


def _dsa_score_sample(qi, wi, ki_new, cache_kidx, page_ids):
    bd = qi.shape[0]
    n_pages = page_ids.shape[1]
    assert cache_kidx.shape[1:] == (IDX_DIM, PAGE_SIZE)

    def page_spec(p):
        return pl.BlockSpec((None, IDX_DIM, PAGE_SIZE), lambda b, pt: (pt[b * n_pages + p], 0, 0))

    return pl.pallas_call(
        functools.partial(_dsa_score_sample_kernel, n_pages=n_pages),
        out_shape=jax.ShapeDtypeStruct((bd, n_pages + 1, PAGE_SIZE), F32),
        grid_spec=pltpu.PrefetchScalarGridSpec(
            num_scalar_prefetch=1,
            grid=(bd,),
            in_specs=[pl.BlockSpec((None, IDX_HEADS, IDX_DIM), lambda b, pt: (b, 0, 0)),
                      pl.BlockSpec((None, IDX_HEADS, 1), lambda b, pt: (b, 0, 0)),
                      pl.BlockSpec((None, 1, IDX_DIM), lambda b, pt: (b, 0, 0))]
                     + [page_spec(p) for p in range(n_pages)],
            out_specs=pl.BlockSpec((None, n_pages + 1, PAGE_SIZE), lambda b, pt: (b, 0, 0))),
        compiler_params=_cparams(("parallel",)),
        name="dsa_score_sample",
    )(page_ids.reshape(-1), qi, wi, ki_new, *([cache_kidx] * n_pages))


def _select_bias_kernel(s_ref, o_ref, key_sc, *, n_keys, ch, ksel):
    n_rows = s_ref.shape[0]
    nch = n_rows // ch
    kpos = lax.broadcasted_iota(jnp.int32, (n_rows, LANES), 0)
    key_sc[...] = jnp.where(kpos < n_keys, _sortable_keys(s_ref[...]), INT_MIN)
    t, need, _ = _kth_threshold(key_sc, nch, ch, ksel)
    tri = _tri_ones(ch)
    carry = jnp.zeros((1, LANES), F32)
    for c in range(nch):
        sel, carry = _select_chunk(key_sc[c * ch:(c + 1) * ch, :], t, need, carry, tri)
        o_ref[c * ch:(c + 1) * ch, :] = jnp.where(sel, 0.0, NEG)


def _select_bias(score_t, n_keys, ksel):
    n_rows, n_q = score_t.shape
    assert n_q == LANES and n_rows % LANES == 0
    return pl.pallas_call(
        functools.partial(_select_bias_kernel, n_keys=n_keys, ch=LANES, ksel=ksel),
        out_shape=jax.ShapeDtypeStruct((n_rows, LANES), F32),
        scratch_shapes=[pltpu.VMEM((n_rows, LANES), jnp.int32)],
        compiler_params=pltpu.CompilerParams(vmem_limit_bytes=VMEM_LIMIT),
        name="dsa_select_sample",
    )(score_t)


def _log_sigmoid(x):
    return -(jnp.maximum(-x, 0.0) + jnp.log1p(jnp.exp(-jnp.abs(x))))


def _gla_gate(glow, wgu, gate_b):
    return _log_sigmoid(_bdot(glow, wgu) + gate_b) / GLA_TAU


def _column(row_vec, eye):
    n = eye.shape[0]
    return jnp.sum(jnp.where(eye, jnp.broadcast_to(row_vec, (n, n)), 0.0), axis=-1, keepdims=True)


def _level_reference(b_ref, half, c):
    sub = lax.broadcasted_iota(jnp.int32, (SUBLANES, LANES), 0)
    slabs = []
    for t0 in range(0, c, SUBLANES):
        cur = None
        for t in range(t0, t0 + SUBLANES, min(SUBLANES, 2 * half)):
            r = (t // (2 * half)) * (2 * half) + half - 1
            bc = jnp.broadcast_to(b_ref[r:r + 1, :], (SUBLANES, LANES))
            cur = bc if cur is None else jnp.where(sub >= (t - t0), bc, cur)
        slabs.append(cur)
    return jnp.concatenate(slabs, axis=0)


def _gla_prompt_kernel(q_ref, k_ref, v_ref, r_ref, gl_ref, wgu_ref, gb_ref, ng_ref, o_ref, st_ref,
                       s_sc, b_sc, *, c, hps):
    ci = pl.program_id(2)
    hk, hv = GLA_HEAD_K, GLA_HEAD_V

    @pl.when(ci == 0)
    def _():
        s_sc[...] = jnp.zeros_like(s_sc)

    g_all = _gla_gate(gl_ref[...], wgu_ref[...], gb_ref[...])
    row = lax.broadcasted_iota(jnp.int32, (c, c), 0)
    col = lax.broadcasted_iota(jnp.int32, (c, c), 1)
    tri = jnp.where(col <= row, 1.0, 0.0).astype(BF16)
    eye = row == col
    halves = [c >> (lvl + 1) for lvl in range(c.bit_length() - 1)]
    valids = [jnp.logical_and(row // (2 * h) == col // (2 * h),
                              jnp.logical_and(row % (2 * h) >= h, col % (2 * h) < h)) for h in halves]
    for hh in range(hps):
        ksl = slice(hh * hk, (hh + 1) * hk)
        vsl = slice(hh * hv, (hh + 1) * hv)
        g = g_all[:, ksl]
        g1 = g.astype(BF16)
        rem = g - g1.astype(F32)
        g2 = rem.astype(BF16)
        g3 = (rem - g2.astype(F32)).astype(BF16)
        bcum = (jnp.dot(tri, g1, preferred_element_type=F32) + jnp.dot(tri, g2, preferred_element_type=F32)
                + jnp.dot(tri, g3, preferred_element_type=F32))
        b_ref = b_sc.at[hh]
        b_ref[...] = bcum
        q = q_ref[:, ksl] * (GLA_HEAD_K ** -0.5)
        k = k_ref[:, ksl]
        v = v_ref[:, vsl]
        att = jnp.where(eye, jnp.sum(q * k, axis=-1, keepdims=True), 0.0)
        for half, valid in zip(halves, valids):
            ref = _level_reference(b_ref, half, c)
            qt = q * jnp.exp(jnp.minimum(bcum - ref, 0.0))
            kt = k * jnp.exp(jnp.minimum(ref - bcum, 0.0))
            att = jnp.where(valid, _bdot_nt(qt, kt), att)
        s0 = s_sc[hh]
        o = _bdot(q * jnp.exp(bcum), s0) + _bdot(att, v)
        o = _rms_rows(o, ng_ref[...])
        rr = r_ref[:, vsl]
        o_ref[:, vsl] = o * (rr * _sigmoid(rr))
        b_last = b_ref[c - 1:c, :]
        khat = k * jnp.exp(b_last - bcum)
        s_new = _column(jnp.exp(b_last), eye) * s0 + _bdot_tn(khat, v)
        s_sc[hh] = s_new

    @pl.when(ci == pl.num_programs(2) - 1)
    def _():
        st_ref[...] = s_sc[...]


def _gla_prompt(qk, v, r, glow, wgu, gate_b, norm_g, *, c, hps):
    b, s, _ = qk.shape
    assert s % c == 0 and c == GLA_HEAD_K and GLA_HEADS % hps == 0
    hk, hv = GLA_HEAD_K, GLA_HEAD_V
    nhb = GLA_HEADS // hps
    return pl.pallas_call(
        functools.partial(_gla_prompt_kernel, c=c, hps=hps),
        out_shape=(jax.ShapeDtypeStruct((b, s, GLA_DV), F32),
                   jax.ShapeDtypeStruct((b, GLA_HEADS, hk, hv), F32)),
        grid=(b, nhb, s // c),
        in_specs=[pl.BlockSpec((None, c, hps * hk), lambda bi, h, ci: (bi, ci, h)),
                  pl.BlockSpec((None, c, hps * hk), lambda bi, h, ci: (bi, ci, nhb + h)),
                  pl.BlockSpec((None, c, hps * hv), lambda bi, h, ci: (bi, ci, h)),
                  pl.BlockSpec((None, c, hps * hv), lambda bi, h, ci: (bi, ci, h)),
                  pl.BlockSpec((None, c, LANES), lambda bi, h, ci: (bi, ci, 0)),
                  pl.BlockSpec((LANES, hps * hk), lambda bi, h, ci: (0, h)),
                  pl.BlockSpec((1, hps * hk), lambda bi, h, ci: (0, h)),
                  pl.BlockSpec((1, hv), lambda bi, h, ci: (0, 0))],
        out_specs=(pl.BlockSpec((None, c, hps * hv), lambda bi, h, ci: (bi, ci, h)),
                   pl.BlockSpec((None, hps, hk, hv), lambda bi, h, ci: (bi, h, 0, 0))),
        scratch_shapes=[pltpu.VMEM((hps, hk, hv), F32), pltpu.VMEM((hps, c, hk), F32)],
        compiler_params=_cparams(("parallel", "parallel", "arbitrary")),
        name="gla_prompt",
    )(qk, qk, v, r, glow, wgu, gate_b.reshape(1, GLA_DK), norm_g.reshape(1, hv))


def _gla_sample_kernel(qk_ref, v_ref, r_ref, gl_ref, wgu_ref, gb_ref, ng_ref, st_ref, o_ref, nst_ref):
    hk, hv = GLA_HEAD_K, GLA_HEAD_V
    glow = jnp.broadcast_to(gl_ref[...], (SUBLANES, LANES))
    g = _gla_gate(glow, wgu_ref[...], gb_ref[...])[0:1, :]
    eye = (lax.broadcasted_iota(jnp.int32, (hk, hk), 0) == lax.broadcasted_iota(jnp.int32, (hk, hk), 1))
    for h in range(GLA_HEADS):
        ksl = slice(h * hk, (h + 1) * hk)
        vsl = slice(h * hv, (h + 1) * hv)
        q_col = _column(qk_ref[:, ksl] * (GLA_HEAD_K ** -0.5), eye)
        k_col = _column(qk_ref[:, GLA_DK + h * hk:GLA_DK + (h + 1) * hk], eye)
        a_col = _column(jnp.exp(g[:, ksl]), eye)
        s_new = a_col * st_ref[h] + k_col * v_ref[:, vsl]
        nst_ref[h] = s_new
        o = jnp.sum(q_col * s_new, axis=0, keepdims=True)
        rr = r_ref[:, vsl]
        o_ref[:, vsl] = _rms_rows(o, ng_ref[...]) * (rr * _sigmoid(rr))


def _gla_sample(qk, v, r, glow, wgu, gate_b, norm_g, state, state_row0):
    bd = qk.shape[0]
    hk, hv = GLA_HEAD_K, GLA_HEAD_V
    rowspec = lambda w: pl.BlockSpec((None, 1, w), lambda b: (b, 0, 0))
    fixed2 = lambda b: (0, 0)
    o, nst = pl.pallas_call(
        _gla_sample_kernel,
        out_shape=(jax.ShapeDtypeStruct((bd, 1, GLA_DV), F32),
                   jax.ShapeDtypeStruct((bd, GLA_HEADS, hk, hv), F32)),
        grid=(bd,),
        in_specs=[rowspec(2 * GLA_DK), rowspec(GLA_DV), rowspec(GLA_DV), rowspec(LANES),
                  pl.BlockSpec((LANES, GLA_DK), fixed2), pl.BlockSpec((1, GLA_DK), fixed2),
                  pl.BlockSpec((1, hv), fixed2),
                  pl.BlockSpec((None, GLA_HEADS, hk, hv), lambda b: (state_row0 + b, 0, 0, 0))],
        out_specs=(rowspec(GLA_DV), pl.BlockSpec((None, GLA_HEADS, hk, hv), lambda b: (b, 0, 0, 0))),
        compiler_params=_cparams(("parallel",)),
        name="gla_sample",
    )(qk.reshape(bd, 1, -1), v.reshape(bd, 1, -1), r.reshape(bd, 1, -1), glow.reshape(bd, 1, -1),
      wgu, gate_b.reshape(1, GLA_DK), norm_g.reshape(1, hv), state)
    return o.reshape(bd, GLA_DV), nst


FLASH_TQ = 512
FLASH_TK = 512


def _row_tile(n, cap):
    t = cap
    while n % t:
        t //= 2
    return t


def _project(x, w, layer, col0, width, rope=None, n_rope=None, tn=None):
    tm = _row_tile(x.shape[0], 1024)
    if tn is None:
        tn = width if width <= D_MODEL else (512 if width % 512 == 0 else LANES)
    return _linear(x, w, layer, col0, width, tm=tm, tn=tn, rope=rope, n_rope=n_rope)


def _pages_per_step(n_pages):
    for n in (16, 4, 2):
        if n_pages % n == 0:
            return n
    return 1


def _project_t(x, w_t, nb, rope=None, with_bf16=False, w_extra_t=None, stack=None):
    s = x.shape[0] // nb
    return _linear_t(x, w_t, nb, tc=min(D_MODEL, w_t.shape[0]), tt=_row_tile(s, 512), rope=rope, with_bf16=with_bf16,
                     w_extra_t=w_extra_t, stack=stack)


def _heads_last(x_t, n_heads):
    b, c, s = x_t.shape
    return x_t.reshape(b, n_heads, c // n_heads, s).transpose(0, 3, 1, 2)


def _diff_mixer(xp, xs, dims, cache_k, cache_v, page_ids, w_in, j, lam_params, subln_g, lam_init, ropes,
                k_stack, v_stack):
    b, s, bd = dims
    d = D_MODEL
    n_diff = w_in.shape[0]
    rope_p, rope_s, rope_pt, rope_st = ropes
    w_k_t = w_in[j][:, d:2 * d].T
    tm = _row_tile(b * s, 512)
    q2_p = _q_pairs(xp, w_in, j, rope_p, tm=_row_tile(s, 512))
    k_stack, k_pt16 = _project_t(xp, w_k_t, b, rope_pt, with_bf16=True,
                                 stack=(n_diff, j, k_stack))
    v_stack, v_pg = _linear_heads(xp, w_in, j, 2 * d, DIFF_HEADS, tm=tm, groups=True,
                                  stack=(n_diff, j, v_stack))
    o_p = _flash_attention(q2_p, k_pt16, v_pg, b, mode="diff", tq=_row_tile(s, FLASH_TQ), tk=_row_tile(s, FLASH_TK),
                           lam_params=lam_params, subln_g=subln_g, lam_init=lam_init)
    q_s = _project(xs, w_in, j, 0, d, rope_s)
    k_s = _project(xs, w_in, j, d, d, rope_s)
    v_s = _project(xs, w_in, j, 2 * d, d)
    k_st = _project_t(xs, w_k_t, 1, rope_st)
    v_sh = _linear_heads(xs, w_in, j, 2 * d, DIFF_HEADS, tm=_row_tile(bd, 512))
    o_s = _decode_attention(q_s, cache_k, cache_v, page_ids, k_s, v_s, mode="diff",
                            n_pp=_pages_per_step(page_ids.shape[1]),
                            lam_params=lam_params, subln_g=subln_g, lam_init=lam_init)
    k_leaf_s = _heads_last(k_st, 2 * DIFF_HEADS).reshape(bd, 1, 2 * DIFF_HEADS, DIFF_HEAD_DIM)
    v_leaf_s = v_sh.reshape(bd, 1, DIFF_HEADS, LANES)
    return o_p.reshape(b * s, d), o_s, k_stack, v_stack, k_leaf_s, v_leaf_s


def _dsa_mixer(xp, xs, dims, cache_k, cache_v, cache_kidx, page_ids, w_in, j, ropes, past_len):
    b, s, bd = dims
    d = D_MODEL
    rope_p, rope_s, rope_pt, rope_st = ropes
    w = w_in[j]
    c_qi, c_ki, c_wi = 3 * d, 3 * d + IDX_HEADS * IDX_DIM, 3 * d + IDX_HEADS * IDX_DIM + IDX_DIM
    w_idx = jnp.concatenate([w[:, c_qi:c_ki], w[:, c_ki:c_wi], w[:, c_ki:c_wi], w[:, c_wi:c_wi + IDX_HEADS],
                             jnp.zeros((d, LANES - IDX_HEADS), F32)], axis=1)[None]
    idx_w = w_idx.shape[2]
    w_k_t = w[:, d:2 * d].T
    w_v_t = w[:, 2 * d:3 * d].T
    w_ki2_t = jnp.concatenate([w[:, c_ki:c_wi], w[:, c_ki:c_wi]], axis=1).T

    q2_p = _q_pairs(xp, w_in, j, rope_p, tm=_row_tile(s, 512))
    k_pt, k_pt16, kd_pt = _project_t(xp, w_k_t, b, rope_pt, with_bf16=True, w_extra_t=w_ki2_t)
    v_pt = _project_t(xp, w_v_t, b)
    v_pg = _linear_heads(xp, w_in, j, 2 * d, N_GROUPS, tm=_row_tile(b * s, 512), leaf=False, groups=True)
    idx_p = _project(xp, w_idx, 0, 0, idx_w, rope_p, n_rope=5, tn=idx_w)
    t = _row_tile(s, FLASH_TK)
    bias_p =_dsa_index_prompt(idx_p.reshape(b, s, idx_w), kd_pt, ch=t)
    o_p = _flash_attention(q2_p, k_pt16, v_pg, b, mode="dsa", tq=_row_tile(s, FLASH_TQ), tk=t, bias=bias_p)

    q_s = _project(xs, w_in, j, 0, d, rope_s)
    k_s = _project(xs, w_in, j, d, d, rope_s)
    v_s = _project(xs, w_in, j, 2 * d, d)
    k_st, kd_st = _project_t(xs, w_k_t, 1, rope_st, w_extra_t=w_ki2_t)
    v_st = _project_t(xs, w_v_t, 1)
    idx_s = _project(xs, w_idx, 0, 0, idx_w, rope_s, n_rope=5, tn=idx_w)
    qi_s = idx_s[:, :IDX_HEADS * IDX_DIM].reshape(bd, IDX_HEADS, IDX_DIM)
    ki_s = idx_s[:, IDX_HEADS * IDX_DIM:IDX_HEADS * IDX_DIM + IDX_DIM]
    wi_s = idx_s[:, 5 * LANES:5 * LANES + IDX_HEADS].reshape(bd, IDX_HEADS, 1)
    n_pages = page_ids.shape[1]
    scores = _dsa_score_sample(qi_s, wi_s, ki_s.reshape(bd, 1, IDX_DIM), cache_kidx, page_ids)
    n_keys = past_len + 1
    bias_t = _select_bias(scores.reshape(bd, (n_pages + 1) * PAGE_SIZE).T, n_keys, min(IDX_TOPK_MAX, n_keys // 4))
    bias_s = bias_t.T.reshape(bd, n_pages + 1, PAGE_SIZE)
    o_s = _decode_attention(q_s, cache_k, cache_v, page_ids, k_s, v_s, mode="dsa",
                            n_pp=_pages_per_step(n_pages), bias=bias_s)
    leaves_p = (_heads_last(k_pt, DSA_HEADS), _heads_last(v_pt, DSA_HEADS), kd_pt[:, :IDX_DIM, :].transpose(0, 2, 1))
    leaves_s = (_heads_last(k_st, DSA_HEADS).reshape(bd, 1, DSA_HEADS, DSA_HEAD_DIM),
                _heads_last(v_st, DSA_HEADS).reshape(bd, 1, DSA_HEADS, DSA_HEAD_DIM),
                kd_st[0, :IDX_DIM, :].T.reshape(bd, 1, IDX_DIM))
    return (o_p.reshape(b * s, d), o_s) + leaves_p + leaves_s


def _gla_mixer(xp, xs, dims, state, j, w_in, w_gate_up, gate_b, norm_g):
    b, s, bd = dims
    d = D_MODEL
    c_g = 2 * GLA_DK + 2 * GLA_DV
    w_g = jnp.pad(w_in[j][:, c_g:c_g + GLA_GATE_RANK], ((0, 0), (0, LANES - GLA_GATE_RANK)))[None]
    wgu = jnp.pad(w_gate_up[j], ((0, LANES - GLA_GATE_RANK), (0, 0)))
    outs = []
    for x in (xp, xs):
        qk = _project(x, w_in, j, 0, 2 * GLA_DK)
        v = _project(x, w_in, j, 2 * GLA_DK, GLA_DV)
        r = _project(x, w_in, j, 2 * GLA_DK + GLA_DV, GLA_DV)
        glow = _project(x, w_g, 0, 0, LANES, tn=LANES)
        outs.append((qk, v, r, glow))
    qk, v, r, glow = outs[0]
    o_p, st_p = _gla_prompt(qk.reshape(b, s, -1), v.reshape(b, s, -1), r.reshape(b, s, -1), glow.reshape(b, s, -1),
                            wgu, gate_b[j], norm_g[j], c=GLA_HEAD_K, hps=GLA_HEADS)
    qk, v, r, glow = outs[1]
    n_state = state.shape[1]
    o_s, st_s = _gla_sample(qk, v, r, glow, wgu, gate_b[j], norm_g[j],
                            state.reshape((-1,) + state.shape[2:]), j * n_state)
    return o_p.reshape(b * s, d), o_s, st_p, st_s


def kernel(x_prompt, x_sample, cache_diff_k, cache_diff_v, cache_dsa_k, cache_dsa_v, cache_dsa_kidx, state_gla,
           page_table, ln_mix_g, ln_mix_b, ln_ffn_g, ln_ffn_b, ffn_w_gate_up, ffn_w_down, diff_w_in, diff_lambda,
           diff_subln_g, diff_w_out, dsa_w_in, dsa_w_out, gla_w_in, gla_w_gate_up, gla_gate_b, gla_norm_g,
           gla_w_out):
    b, s, d = x_prompt.shape
    bd, s_d, _ = x_sample.shape
    assert s_d == 1 and d == D_MODEL
    dims = (b, s, bd)
    n_pool, page = cache_diff_k.shape[1], cache_diff_k.shape[2]
    past_len = page_table.shape[1] * page
    xp = x_prompt.reshape(b * s, d)
    xs = x_sample.reshape(bd, d)
    pos_p = jnp.arange(s, dtype=jnp.int32)
    pos_s = jnp.full((bd,), past_len, dtype=jnp.int32)
    ropes = (_rope_tables(pos_p), _rope_tables(pos_s), _rope_tables_t(pos_p), _rope_tables_t(pos_s))

    def feature_major(c):
        perm = (0, 1) + tuple(range(3, c.ndim)) + (2,)
        return jnp.transpose(c, perm).reshape(c.shape[0] * c.shape[1], -1, page)

    cdk, csk, csv, csi = (feature_major(c) for c in (cache_diff_k, cache_dsa_k, cache_dsa_v, cache_dsa_kidx))
    cdv = cache_diff_v.reshape(cache_diff_v.shape[0] * n_pool, page * DIFF_HEADS, LANES)
    tm_p = _row_tile(b * s, 512)
    tm_s = _row_tile(bd, 512)
    tf = 256
    dk_stack = dv_stack = None
    dk_s, dv_s = [], []
    sk_p, sv_p, si_p, sk_s, sv_s, si_s = [], [], [], [], [], []
    gs_p, gs_s = [], []
    n_diff = diff_w_in.shape[0]
    for i in range(DEPTH):
        kind, j = i % 3, i // 3
        page_ids = page_table + j * n_pool
        if kind == 0:
            lam_init = 0.8 - 0.6 * math.exp(-0.3 * i)
            o_p, o_s, dk_stack, dv_stack, k_s, v_s = _diff_mixer(
                xp, xs, dims, cdk, cdv, page_ids, diff_w_in, j, diff_lambda[j], diff_subln_g[j], lam_init, ropes,
                dk_stack, dv_stack)
            dk_s.append(k_s)
            dv_s.append(v_s)
            w_out = diff_w_out
        elif kind == 1:
            o_p, o_s, k_p, v_p, ki_p, k_s, v_s, ki_s = _dsa_mixer(xp, xs, dims, csk, csv, csi, page_ids, dsa_w_in, j,
                                                                  ropes, past_len)
            sk_p.append(k_p)
            sv_p.append(v_p)
            si_p.append(ki_p)
            sk_s.append(k_s)
            sv_s.append(v_s)
            si_s.append(ki_s)
            w_out = dsa_w_out
        else:
            o_p, o_s, st_p, st_s = _gla_mixer(xp, xs, dims, state_gla, j, gla_w_in, gla_w_gate_up, gla_gate_b,
                                              gla_norm_g)
            gs_p.append(st_p)
            gs_s.append(st_s)
            w_out = gla_w_out
        xp = _linear_res_ln(o_p, w_out, j, xp, ln_mix_g, ln_mix_b, i, tm=tm_p)
        xs = _linear_res_ln(o_s, w_out, j, xs, ln_mix_g, ln_mix_b, i, tm=tm_s)
        xp = _ffn_ln(xp, ffn_w_gate_up, ffn_w_down, ln_ffn_g, ln_ffn_b, i, tm=_row_tile(b * s, 1024), tf=tf)
        xs = _ffn_ln(xs, ffn_w_gate_up, ffn_w_down, ln_ffn_g, ln_ffn_b, i, tm=tm_s, tf=tf)
    dk_p = dk_stack.reshape(n_diff, b, 2 * DIFF_HEADS, DIFF_HEAD_DIM, s).transpose(0, 1, 4, 2, 3)
    dv_p = dv_stack.reshape(n_diff, b, s, DIFF_HEADS, LANES)
    return (xp.reshape(b, s, d), xs.reshape(bd, 1, d),
            dk_p, dv_p, jnp.stack(dk_s), jnp.stack(dv_s),
            jnp.stack(sk_p), jnp.stack(sv_p), jnp.stack(si_p),
            jnp.stack(sk_s), jnp.stack(sv_s), jnp.stack(si_s),
            jnp.stack(gs_p), jnp.stack(gs_s))
```

```python
import functools
import math

import jax
import jax.numpy as jnp
from jax import lax
from jax.experimental import pallas as pl
from jax.experimental.pallas import tpu as pltpu

F32 = jnp.float32
BF16 = jnp.bfloat16

D_MODEL = 1024
DEPTH = 4
PAGE_SIZE = 128
DIFF_HEADS = 8
DIFF_HEAD_DIM = 64
DSA_HEADS = 16
DSA_HEAD_DIM = 64
IDX_HEADS = 8
IDX_DIM = 64
IDX_TOPK_MAX = 256
GLA_HEADS = 4
GLA_DK = D_MODEL // 2
GLA_DV = D_MODEL
GLA_HEAD_K = GLA_DK // GLA_HEADS
GLA_HEAD_V = GLA_DV // GLA_HEADS
GLA_GATE_RANK = 16
GLA_TAU = 16.0
D_FF = (((8 * D_MODEL + 2) // 3 + 255) // 256) * 256
ROPE_THETA = 10000.0
LN_EPS = 1e-5
DEEPNORM_ALPHA = (2.0 * DEPTH) ** 0.25

LANES = 128
SUBLANES = 8
VMEM_LIMIT = 56 * 1024 * 1024
NEG = -1e30
INT_MIN = -2 ** 31

N_GROUPS = D_MODEL // LANES


def _cparams(sem):
    return pltpu.CompilerParams(dimension_semantics=sem, vmem_limit_bytes=VMEM_LIMIT)


def _bdot(a, b):
    return jnp.dot(a.astype(BF16), b.astype(BF16), preferred_element_type=F32)


def _bdot_nt(a, b):
    return lax.dot_general(a.astype(BF16), b.astype(BF16), (((1,), (1,)), ((), ())),
                           preferred_element_type=F32)


def _bdot_tn(a, b):
    return lax.dot_general(a.astype(BF16), b.astype(BF16), (((0,), (0,)), ((), ())),
                           preferred_element_type=F32)


def _layer_norm_rows(z, g, b):
    mu = jnp.mean(z, axis=-1, keepdims=True)
    zc = z - mu
    var = jnp.mean(zc * zc, axis=-1, keepdims=True)
    return zc * lax.rsqrt(var + LN_EPS) * g + b


def _sigmoid(x):
    return 1.0 / (1.0 + jnp.exp(-x))


def _rope_tables(pos):
    d = DIFF_HEAD_DIM
    inv = ROPE_THETA ** (-jnp.arange(0, d, 2, dtype=F32) / d)
    ang = pos.astype(F32)[:, None] * inv[None, :]
    cos = jnp.cos(ang)
    sin = jnp.sin(ang)
    return (jnp.concatenate([cos, cos, cos, cos], axis=-1),
            jnp.concatenate([-sin, sin, -sin, sin], axis=-1))


def _rope_apply(y, cos, sin, first_half):
    partner = jnp.where(first_half, pltpu.roll(y, 96, 1), pltpu.roll(y, 32, 1))
    return y * cos + partner * sin


def _linear_kernel(x_ref, w_ref, *rest, n_rope, single_tile):
    if n_rope:
        cos_ref, sin_ref, o_ref = rest
    else:
        (o_ref,) = rest
    acc = _bdot(x_ref[...], w_ref[...])
    tm, tn = acc.shape
    if not n_rope:
        o_ref[...] = acc
        return

    def roped(n_chunks):
        cos = cos_ref[...]
        sin = sin_ref[...]
        lane = lax.broadcasted_iota(jnp.int32, (tm, LANES), 1)
        first_half = (lane % DIFF_HEAD_DIM) < (DIFF_HEAD_DIM // 2)
        for c in range(tn // LANES):
            sl = slice(c * LANES, (c + 1) * LANES)
            o_ref[:, sl] = _rope_apply(acc[:, sl], cos, sin, first_half) if c < n_chunks else acc[:, sl]

    if single_tile:
        roped(n_rope)
        return
    j = pl.program_id(1)
    pl.when(j < n_rope)(lambda: roped(tn // LANES))

    @pl.when(j >= n_rope)
    def _():
        o_ref[...] = acc


def _linear(x, w, layer, col0, width, *, tm, tn, rope=None, n_rope=None):
    n, k = x.shape
    assert n % tm == 0 and width % tn == 0 and col0 % tn == 0
    nj = width // tn
    if rope is None:
        n_rope = 0
    elif n_rope is None:
        n_rope = nj if nj > 1 else tn // LANES
    in_specs = [pl.BlockSpec((tm, k), lambda i, j: (i, 0)),
                pl.BlockSpec((None, k, tn), lambda i, j: (layer, 0, col0 // tn + j))]
    args = [x, w]
    if n_rope:
        p_blocks = rope[0].shape[0] // tm
        assert rope[0].shape[0] % tm == 0
        spec = pl.BlockSpec((tm, LANES), lambda i, j: (i % p_blocks, 0))
        in_specs += [spec, spec]
        args += [rope[0], rope[1]]
    return pl.pallas_call(
        functools.partial(_linear_kernel, n_rope=n_rope, single_tile=nj == 1),
        out_shape=jax.ShapeDtypeStruct((n, width), F32),
        grid=(n // tm, nj),
        in_specs=in_specs,
        out_specs=pl.BlockSpec((tm, tn), lambda i, j: (i, j)),
        compiler_params=_cparams(("parallel", "arbitrary")),
        name="linear",
    )(*args)


def _rope_tables_t(pos):
    cos, sin = _rope_tables(pos)
    return cos.T, sin.T


def _linear_t_kernel(w_ref, x_ref, *rest, rope, with_bf16, extra, aliased):
    rest = list(rest)
    we_ref = rest.pop(0) if extra else None
    cos_ref, sin_ref = (rest.pop(0), rest.pop(0)) if rope else (None, None)
    if aliased:
        rest.pop(0)
    o_ref = rest.pop(0)
    o16_ref = rest.pop(0) if with_bf16 else None
    oe_ref = rest.pop(0) if extra else None
    tt = x_ref.shape[0]
    xb = x_ref[...].astype(BF16)

    def roped(y):
        row = lax.broadcasted_iota(jnp.int32, (LANES, tt), 0)
        first_half = (row % DIFF_HEAD_DIM) < (DIFF_HEAD_DIM // 2)
        partner = jnp.where(first_half, pltpu.roll(y, LANES - DIFF_HEAD_DIM // 2, 0),
                            pltpu.roll(y, DIFF_HEAD_DIM // 2, 0))
        return y * cos_ref[...] + partner * sin_ref[...]

    acc = _bdot_nt(w_ref[...], xb)
    for c in range(acc.shape[0] // LANES):
        sl = slice(c * LANES, (c + 1) * LANES)
        y = roped(acc[sl, :]) if rope else acc[sl, :]
        o_ref[sl, :] = y
        if with_bf16:
            o16_ref[sl, :] = y.astype(BF16)

    if extra:
        @pl.when(pl.program_id(2) == 0)
        def _():
            ye = _bdot_nt(we_ref[...], xb)
            oe_ref[...] = roped(ye) if rope else ye


def _linear_t(x, w_t, nb, *, tc, tt, rope=None, with_bf16=False, w_extra_t=None, stack=None):
    n, k = x.shape
    c = w_t.shape[0]
    s = n // nb
    assert n == nb * s and s % tt == 0 and c % tc == 0 and tc % LANES == 0
    nt = s // tt
    extra = w_extra_t is not None
    n_layers, layer, prev = stack if stack is not None else (1, 0, None)
    in_specs = [pl.BlockSpec((tc, k), lambda b, ti, j: (j, 0)),
                pl.BlockSpec((tt, k), lambda b, ti, j: (b * nt + ti, 0))]
    args = [w_t, x]
    if extra:
        assert w_extra_t.shape == (LANES, k)
        in_specs.append(pl.BlockSpec((LANES, k), lambda b, ti, j: (0, 0)))
        args.append(w_extra_t)
    if rope is not None:
        p_blocks = rope[0].shape[1] // tt
        spec = pl.BlockSpec((LANES, tt), lambda b, ti, j: (0, ti % p_blocks))
        in_specs += [spec, spec]
        args += [rope[0], rope[1]]
    aliases = {}
    if prev is not None:
        assert prev.shape == (n_layers * nb, c, s)
        aliases = {len(args): 0}
        in_specs.append(pl.BlockSpec(memory_space=pl.ANY))
        args.append(prev)
    out_spec = pl.BlockSpec((None, tc, tt), lambda b, ti, j: (b, j, ti))
    out_shape = [jax.ShapeDtypeStruct((n_layers * nb, c, s), F32)]
    out_specs = [pl.BlockSpec((None, tc, tt), lambda b, ti, j: (layer * nb + b, j, ti))]
    if with_bf16:
        out_shape.append(jax.ShapeDtypeStruct((nb, c, s), BF16))
        out_specs.append(out_spec)
    if extra:
        out_shape.append(jax.ShapeDtypeStruct((nb, LANES, s), F32))
        out_specs.append(pl.BlockSpec((None, LANES, tt), lambda b, ti, j: (b, 0, ti)))
    outs = pl.pallas_call(
        functools.partial(_linear_t_kernel, rope=rope is not None, with_bf16=with_bf16, extra=extra,
                          aliased=prev is not None),
        out_shape=out_shape,
        grid=(nb, nt, c // tc),
        in_specs=in_specs,
        out_specs=out_specs,
        input_output_aliases=aliases,
        compiler_params=_cparams(("parallel", "parallel", "arbitrary")),
        name="linear_t",
    )(*args)
    return tuple(outs) if len(outs) > 1 else outs[0]


def _linear_heads_kernel(x_ref, w_ref, *o_refs, leaf, groups, aliased):
    xb = x_ref[...].astype(BF16)
    o_refs = list(o_refs)
    if aliased:
        o_refs.pop(0)
    leaf_ref = o_refs.pop(0) if leaf else None
    grp_ref = o_refs.pop(0) if groups else None
    for h in range(w_ref.shape[1] // LANES):
        y = jnp.dot(xb, w_ref[:, h * LANES:(h + 1) * LANES].astype(BF16), preferred_element_type=F32)
        if leaf:
            leaf_ref[:, h, :] = y
        if groups:
            grp_ref[h] = y.astype(BF16)


def _linear_heads(x, w, layer, col0, n_heads, *, tm, leaf=True, groups=False, stack=None):
    n, k = x.shape
    width = n_heads * LANES
    assert n % tm == 0 and col0 % width == 0 and (leaf or groups)
    n_slabs, slab, prev = stack if stack is not None else (1, 0, None)
    in_specs = [pl.BlockSpec((tm, k), lambda i: (i, 0)),
                pl.BlockSpec((None, k, width), lambda i: (layer, 0, col0 // width))]
    args = [x, w]
    aliases = {}
    if prev is not None:
        assert leaf and prev.shape == (n_slabs * n, n_heads, LANES)
        aliases = {len(args): 0}
        in_specs.append(pl.BlockSpec(memory_space=pl.ANY))
        args.append(prev)
    out_shape, out_specs = [], []
    if leaf:
        out_shape.append(jax.ShapeDtypeStruct((n_slabs * n, n_heads, LANES), F32))
        out_specs.append(pl.BlockSpec((tm, n_heads, LANES), lambda i: (slab * (n // tm) + i, 0, 0)))
    if groups:
        out_shape.append(jax.ShapeDtypeStruct((n_heads, n, LANES), BF16))
        out_specs.append(pl.BlockSpec((n_heads, tm, LANES), lambda i: (0, i, 0)))
    outs = pl.pallas_call(
        functools.partial(_linear_heads_kernel, leaf=leaf, groups=groups, aliased=prev is not None),
        out_shape=out_shape,
        grid=(n // tm,),
        in_specs=in_specs,
        out_specs=out_specs,
        input_output_aliases=aliases,
        compiler_params=_cparams(("parallel",)),
        name="linear_heads",
    )(*args)
    return outs[0] if len(outs) == 1 else tuple(outs)


LOG2E = 1.4426950408889634


def _q_pairs_kernel(x_ref, w_ref, cos_ref, sin_ref, o_ref, *, scale):
    acc = _bdot(x_ref[...], w_ref[...])
    tm = acc.shape[0]
    cos = cos_ref[...]
    sin = sin_ref[...]
    lane = lax.broadcasted_iota(jnp.int32, (tm, LANES), 1)
    first_half = (lane % DIFF_HEAD_DIM) < (DIFF_HEAD_DIM // 2)
    lo = lane < DIFF_HEAD_DIM
    for g in range(N_GROUPS):
        y = _rope_apply(acc[:, g * LANES:(g + 1) * LANES], cos, sin, first_half) * scale
        o_ref[g, 0] = jnp.where(lo, y, 0.0).astype(BF16)
        o_ref[g, 1] = jnp.where(lo, 0.0, y).astype(BF16)


def _q_pairs(x, w, layer, rope, *, tm):
    n, k = x.shape
    d = D_MODEL
    assert n % tm == 0 and rope[0].shape[0] % tm == 0
    p_blocks = rope[0].shape[0] // tm
    rspec = pl.BlockSpec((tm, LANES), lambda i: (i % p_blocks, 0))
    return pl.pallas_call(
        functools.partial(_q_pairs_kernel, scale=DIFF_HEAD_DIM ** -0.5 * LOG2E),
        out_shape=jax.ShapeDtypeStruct((N_GROUPS, 2, n, LANES), BF16),
        grid=(n // tm,),
        in_specs=[pl.BlockSpec((tm, k), lambda i: (i, 0)),
                  pl.BlockSpec((None, k, d), lambda i: (layer, 0, 0)), rspec, rspec],
        out_specs=pl.BlockSpec((N_GROUPS, 2, tm, LANES), lambda i: (0, 0, i, 0)),
        compiler_params=_cparams(("parallel",)),
        name="q_pairs",
    )(x, w, rope[0], rope[1])


def _res_ln_kernel(x_ref, w_ref, r_ref, g_ref, b_ref, o_ref):
    y = _bdot(x_ref[...], w_ref[...])
    z = DEEPNORM_ALPHA * r_ref[...] + y
    o_ref[...] = _layer_norm_rows(z, g_ref[...], b_ref[...])


def _linear_res_ln(x, w, layer, resid, g, b, ln_layer, *, tm):
    n, k = x.shape
    d = w.shape[2]
    assert n % tm == 0
    row = lambda i: (i, 0)
    ln_row = lambda i: (ln_layer, 0, 0)
    return pl.pallas_call(
        _res_ln_kernel,
        out_shape=jax.ShapeDtypeStruct((n, d), F32),
        grid=(n // tm,),
        in_specs=[pl.BlockSpec((tm, k), row), pl.BlockSpec((None, k, d), lambda i: (layer, 0, 0)),
                  pl.BlockSpec((tm, d), row),
                  pl.BlockSpec((None, 1, d), ln_row), pl.BlockSpec((None, 1, d), ln_row)],
        out_specs=pl.BlockSpec((tm, d), row),
        compiler_params=_cparams(("parallel",)),
        name="out_proj_ln",
    )(x, w, resid, g.reshape(-1, 1, d), b.reshape(-1, 1, d))


def _ffn_kernel(x_ref, wg_ref, wu_ref, wd_ref, g_ref, b_ref, o_ref, acc_ref, xb_ref):
    f = pl.program_id(1)

    @pl.when(f == 0)
    def _():
        acc_ref[...] = jnp.zeros_like(acc_ref)
        xb_ref[...] = x_ref[...].astype(BF16)

    xb = xb_ref[...]
    gate = jnp.dot(xb, wg_ref[...].astype(BF16), preferred_element_type=F32)
    up = jnp.dot(xb, wu_ref[...].astype(BF16), preferred_element_type=F32)
    h = gate * _sigmoid(gate) * up
    acc_ref[...] += _bdot(h, wd_ref[...])

    @pl.when(f == pl.num_programs(1) - 1)
    def _():
        z = DEEPNORM_ALPHA * x_ref[...] + acc_ref[...]
        o_ref[...] = _layer_norm_rows(z, g_ref[...], b_ref[...])


def _ffn_ln(x, w_gate_up, w_down, g, b, layer, *, tm, tf):
    n, d = x.shape
    dff = w_down.shape[1]
    assert n % tm == 0 and dff % tf == 0
    nf = dff // tf
    ln_row = lambda i, f: (layer, 0, 0)
    return pl.pallas_call(
        _ffn_kernel,
        out_shape=jax.ShapeDtypeStruct((n, d), F32),
        grid=(n // tm, nf),
        in_specs=[pl.BlockSpec((tm, d), lambda i, f: (i, 0)),
                  pl.BlockSpec((None, d, tf), lambda i, f: (layer, 0, f)),
                  pl.BlockSpec((None, d, tf), lambda i, f: (layer, 0, nf + f)),
                  pl.BlockSpec((None, tf, d), lambda i, f: (layer, f, 0)),
                  pl.BlockSpec((None, 1, d), ln_row), pl.BlockSpec((None, 1, d), ln_row)],
        out_specs=pl.BlockSpec((tm, d), lambda i, f: (i, 0)),
        scratch_shapes=[pltpu.VMEM((tm, d), F32), pltpu.VMEM((tm, d), BF16)],
        compiler_params=_cparams(("parallel", "arbitrary")),
        name="ffn_ln",
    )(x, w_gate_up, w_gate_up, w_down, g.reshape(-1, 1, d), b.reshape(-1, 1, d))


def _diff_lambda(lam_ref, lam_init):
    l = lam_ref[...]
    a = jnp.sum(l[0:1] * l[1:2], axis=-1, keepdims=True)
    c = jnp.sum(l[2:3] * l[3:4], axis=-1, keepdims=True)
    return jnp.exp(a) - jnp.exp(c) + lam_init


def _rms_rows(o, g):
    return o * lax.rsqrt(jnp.mean(o * o, axis=-1, keepdims=True) + LN_EPS) * g


def _flash_kernel(qi_ref, ki_ref, *refs, mode, tq, tk, lam_init):
    if mode == "diff":
        q_ref, k_ref, v_ref, lam_ref, g_ref, o_ref, m_sc, l_sc, acc_sc = refs
    else:
        q_ref, k_ref, v_ref, bias_ref, o_ref, m_sc, l_sc, acc_sc = refs
    step = pl.program_id(1)
    qi = qi_ref[step]
    ki = ki_ref[step]
    k_last = (qi * tq + (tq - 1)) // tk

    @pl.when(ki == 0)
    def _():
        m_sc[...] = jnp.full_like(m_sc, NEG)
        l_sc[...] = jnp.zeros_like(l_sc)
        acc_sc[...] = jnp.zeros_like(acc_sc)

    def block_update(causal):
        if causal:
            row = lax.broadcasted_iota(jnp.int32, (2 * tq, tk), 0)
            col = lax.broadcasted_iota(jnp.int32, (2 * tq, tk), 1)
            allowed = col <= jnp.where(row >= tq, row - tq, row) + (qi * tq) % tk

        def group(g, carry):
            q2 = q_ref[g].reshape(2 * tq, LANES)
            s = jnp.dot(q2, k_ref[g], preferred_element_type=F32)
            if mode == "dsa":
                bias = bias_ref[...]
                s = s + jnp.concatenate([bias, bias], axis=0)
            if causal:
                s = jnp.where(allowed, s, NEG)
            m_prev = m_sc[g]
            m_next = jnp.maximum(m_prev, jnp.max(s, axis=-1, keepdims=True))
            p = jnp.exp2(s - jnp.concatenate([m_next] * (tk // LANES), axis=-1))
            alpha = jnp.exp2(m_prev - m_next)
            l_sc[g] = alpha * l_sc[g] + jnp.sum(p, axis=-1, keepdims=True)
            acc_sc[g] = alpha * acc_sc[g] + jnp.dot(p.astype(BF16), v_ref[g], preferred_element_type=F32)
            m_sc[g] = m_next
            return carry

        lax.fori_loop(0, N_GROUPS, group, 0, unroll=4)

    if mode == "diff":
        pl.when(ki < k_last)(lambda: block_update(False))
        pl.when(ki == k_last)(lambda: block_update(True))
    else:
        block_update(False)

    @pl.when(ki == k_last)
    def _():
        lane = lax.broadcasted_iota(jnp.int32, (1, LANES), 1)
        lo = lane < DIFF_HEAD_DIM
        if mode == "diff":
            lam = _diff_lambda(lam_ref, lam_init)
            gain = g_ref[...] * (1.0 - lam_init)
        for g in range(N_GROUPS):
            o_lo = acc_sc[g, :tq, :] / l_sc[g, :tq, :]
            o_hi = acc_sc[g, tq:, :] / l_sc[g, tq:, :]
            if mode == "diff":
                o_ref[:, g * LANES:(g + 1) * LANES] = _rms_rows(o_lo - lam * o_hi, gain)
            else:
                o_ref[:, g * LANES:(g + 1) * LANES] = jnp.where(lo, o_lo, o_hi)


def _flash_attention(q2, k_t, v_g, b, *, mode, tq, tk, lam_params=None, subln_g=None, lam_init=0.0, bias=None):
    n = q2.shape[2]
    s = n // b
    d = D_MODEL
    assert s % tq == 0 and s % tk == 0 and tk % tq == 0 and n == b * s
    nq, nk = s // tq, s // tk
    k4 = k_t.reshape(b, N_GROUPS, LANES, s)
    pairs = [(qb, kb) for qb in range(nq) for kb in range((qb * tq + tq - 1) // tk + 1)]
    qi_of = jnp.asarray([p[0] for p in pairs], dtype=jnp.int32)
    ki_of = jnp.asarray([p[1] for p in pairs], dtype=jnp.int32)
    qspec = pl.BlockSpec((N_GROUPS, 2, tq, LANES), lambda bi, st, qo, ko: (0, 0, bi * nq + qo[st], 0))
    kspec = pl.BlockSpec((None, N_GROUPS, LANES, tk), lambda bi, st, qo, ko: (bi, 0, 0, ko[st]))
    vspec = pl.BlockSpec((N_GROUPS, tk, LANES), lambda bi, st, qo, ko: (0, bi * nk + ko[st], 0))
    in_specs = [qspec, kspec, vspec]
    args = [q2, k4, v_g]
    if mode == "diff":
        in_specs += [pl.BlockSpec((4, DIFF_HEAD_DIM), lambda bi, st, qo, ko: (0, 0)),
                     pl.BlockSpec((1, LANES), lambda bi, st, qo, ko: (0, 0))]
        args += [lam_params, subln_g.reshape(1, LANES)]
    else:
        assert bias.shape == (b, nk, s, tk)
        in_specs += [pl.BlockSpec((None, None, tq, tk), lambda bi, st, qo, ko: (bi, ko[st], qo[st], 0))]
        args += [bias]
    stat = pltpu.VMEM((N_GROUPS, 2 * tq, LANES), F32)
    return pl.pallas_call(
        functools.partial(_flash_kernel, mode=mode, tq=tq, tk=tk, lam_init=lam_init),
        out_shape=jax.ShapeDtypeStruct((n, d), F32),
        grid_spec=pltpu.PrefetchScalarGridSpec(
            num_scalar_prefetch=2,
            grid=(b, len(pairs)),
            in_specs=in_specs,
            out_specs=pl.BlockSpec((tq, d), lambda bi, st, qo, ko: (bi * nq + qo[st], 0)),
            scratch_shapes=[stat, stat, stat]),
        compiler_params=_cparams(("parallel", "arbitrary")),
        name="flash_" + mode,
    )(qi_of, ki_of, *args)


def _decode_kernel(pt_ref, q_ref, *refs, mode, n_pp, n_pages, scale, lam_init):
    del pt_ref
    k_refs = refs[:n_pp]
    v_refs = refs[n_pp:2 * n_pp]
    knew_ref, vnew_ref = refs[2 * n_pp:2 * n_pp + 2]
    rest = refs[2 * n_pp + 2:]
    if mode == "diff":
        lam_ref, g_ref, o_ref, qm_sc, m_sc, l_sc, acc_sc, e_sc = rest
    else:
        bias_ref, o_ref, qm_sc, m_sc, l_sc, acc_sc = rest
    step = pl.program_id(1)
    n_sub = 2 * N_GROUPS
    row = lax.broadcasted_iota(jnp.int32, (n_sub, D_MODEL), 0)
    lane = lax.broadcasted_iota(jnp.int32, (n_sub, D_MODEL), 1)

    @pl.when(step == 0)
    def _():
        qm_sc[...] = jnp.where(lane // DIFF_HEAD_DIM == row, q_ref[...] * scale, 0.0)
        m_sc[...] = jnp.full_like(m_sc, NEG)
        l_sc[...] = jnp.zeros_like(l_sc)
        acc_sc[...] = jnp.zeros_like(acc_sc)
        if mode == "diff":
            tok = lax.broadcasted_iota(jnp.int32, (PAGE_SIZE, D_MODEL), 0)
            col = lax.broadcasted_iota(jnp.int32, (PAGE_SIZE, D_MODEL), 1)
            e_sc[...] = jnp.where(col // DIFF_HEADS == tok, 1.0, 0.0).astype(BF16)

    qm = qm_sc[...]
    s_parts = []
    for i in range(n_pp):
        s = _bdot(qm, k_refs[i][...])
        if mode == "dsa":
            s = s + bias_ref[pl.ds(step * n_pp + i, 1), :]
        s_parts.append(s)
    s_all = jnp.concatenate(s_parts, axis=-1)
    m_prev = m_sc[...]
    m_new = jnp.maximum(m_prev, jnp.max(s_all, axis=-1, keepdims=True))
    alpha = jnp.exp(m_prev - m_new)
    p_all = jnp.exp(s_all - m_new)
    l_sc[...] = alpha * l_sc[...] + jnp.sum(p_all, axis=-1, keepdims=True)
    pv = None
    for i in range(n_pp):
        p_i = p_all[:, i * PAGE_SIZE:(i + 1) * PAGE_SIZE]
        if mode == "diff":
            p_rows = jnp.where(lane % DIFF_HEADS == row // 2,
                               jnp.dot(p_i.astype(BF16), e_sc[...], preferred_element_type=F32), 0.0)
            part = _bdot(p_rows, v_refs[i][...])
        else:
            part = _bdot_nt(p_i, v_refs[i][...])
        pv = part if pv is None else pv + part
    acc_sc[...] = alpha * acc_sc[...] + pv
    m_sc[...] = m_new

    @pl.when(step == pl.num_programs(1) - 1)
    def _():
        s_new = jnp.sum(qm * knew_ref[...], axis=-1, keepdims=True)
        if mode == "dsa":
            s_new = s_new + bias_ref[n_pages:n_pages + 1, 0:1]
        m_prev = m_sc[...]
        m_fin = jnp.maximum(m_prev, s_new)
        alpha = jnp.exp(m_prev - m_fin)
        p_new = jnp.exp(s_new - m_fin)
        l_fin = alpha * l_sc[...] + p_new
        if mode == "diff":
            row_h = lax.broadcasted_iota(jnp.int32, (n_sub, LANES), 0) // 2
            v_rows = jnp.zeros((n_sub, LANES), F32)
            for h in range(DIFF_HEADS):
                v_rows = jnp.where(row_h == h, vnew_ref[:, h * LANES:(h + 1) * LANES], v_rows)
            o16 = (alpha * acc_sc[...] + p_new * v_rows) / l_fin
            lam = _diff_lambda(lam_ref, lam_init)
            gain = g_ref[...] * (1.0 - lam_init)
            for h in range(DIFF_HEADS):
                o_h = o16[2 * h:2 * h + 1, :] - lam * o16[2 * h + 1:2 * h + 2, :]
                o_ref[:, h * LANES:(h + 1) * LANES] = _rms_rows(o_h, gain)
        else:
            o16 = (alpha * acc_sc[...] + p_new * vnew_ref[...]) / l_fin
            o_ref[...] = jnp.sum(jnp.where(lane // DSA_HEAD_DIM == row, o16, 0.0), axis=0, keepdims=True)


def _decode_attention(q, cache_k, cache_v, page_ids, k_new, v_new, *, mode, n_pp,
                      lam_params=None, subln_g=None, lam_init=0.0, bias=None):
    bd, d = q.shape
    n_pages = page_ids.shape[1]
    assert n_pages % n_pp == 0 and cache_k.shape[1:] == (d, PAGE_SIZE) and cache_v.shape[1:] == (d, PAGE_SIZE)
    n_steps = n_pages // n_pp
    row_spec = pl.BlockSpec((None, 1, d), lambda b, s, pt: (b, 0, 0))

    def page_spec(i):
        return pl.BlockSpec((None, d, PAGE_SIZE), lambda b, s, pt: (pt[b * n_pages + s * n_pp + i], 0, 0))

    in_specs = [row_spec] + [page_spec(i) for i in range(n_pp)] * 2 + [row_spec, row_spec]
    args = [q.reshape(bd, 1, d)] + [cache_k] * n_pp + [cache_v] * n_pp + [k_new.reshape(bd, 1, d), v_new.reshape(bd, 1, d)]
    n_sub = 2 * N_GROUPS
    scratch = [pltpu.VMEM((n_sub, d), F32), pltpu.VMEM((n_sub, 1), F32), pltpu.VMEM((n_sub, 1), F32)]
    if mode == "diff":
        in_specs += [pl.BlockSpec((4, DIFF_HEAD_DIM), lambda b, s, pt: (0, 0)),
                     pl.BlockSpec((1, LANES), lambda b, s, pt: (0, 0))]
        args += [lam_params, subln_g.reshape(1, LANES)]
        scratch += [pltpu.VMEM((n_sub, LANES), F32), pltpu.VMEM((PAGE_SIZE, d), BF16)]
    else:
        in_specs += [pl.BlockSpec((None, n_pages + 1, PAGE_SIZE), lambda b, s, pt: (b, 0, 0))]
        args += [bias]
        scratch += [pltpu.VMEM((n_sub, d), F32)]
    out = pl.pallas_call(
        functools.partial(_decode_kernel, mode=mode, n_pp=n_pp, n_pages=n_pages,
                          scale=DIFF_HEAD_DIM ** -0.5, lam_init=lam_init),
        out_shape=jax.ShapeDtypeStruct((bd, 1, d), F32),
        grid_spec=pltpu.PrefetchScalarGridSpec(
            num_scalar_prefetch=1,
            grid=(bd, n_steps),
            in_specs=in_specs,
            out_specs=row_spec,
            scratch_shapes=scratch),
        compiler_params=_cparams(("parallel", "arbitrary")),
        name="decode_" + mode,
    )(page_ids.reshape(-1), *args)
    return out.reshape(bd, d)


def _sortable_keys(score):
    bits = lax.bitcast_convert_type(score, jnp.int32)
    key = jnp.where(bits < 0, bits ^ jnp.int32(0x7FFFFFFF), bits)
    return jnp.where(score == 0.0, 0, key)


def _kth_threshold(key_ref, nch, ch, ksel):
    def count_ge(cand):
        def body(c, acc):
            blk = key_ref[pl.ds(pl.multiple_of(c * ch, ch), ch), :]
            hit = jnp.where(blk >= cand, 1, 0).astype(jnp.int32)
            return acc + jnp.sum(hit.reshape(ch // SUBLANES, SUBLANES, LANES), axis=0)

        acc = lax.fori_loop(0, nch, body, jnp.zeros((SUBLANES, LANES), jnp.int32))
        return jnp.sum(acc, axis=0, keepdims=True)

    def bit_body(i, t):
        cand = t ^ lax.shift_left(jnp.int32(1), jnp.asarray(31 - i, dtype=jnp.int32))
        return jnp.where(count_ge(cand) >= ksel, cand, t)

    t = lax.fori_loop(0, 32, bit_body, jnp.full((1, LANES), INT_MIN, jnp.int32))
    n_above = count_ge(t + 1)
    need = ksel - n_above
    surplus = jnp.logical_and(t > INT_MIN, count_ge(t) - n_above > need)
    return t, need.astype(F32), surplus


def _tri_ones(n):
    row = lax.broadcasted_iota(jnp.int32, (n, n), 0)
    col = lax.broadcasted_iota(jnp.int32, (n, n), 1)
    return jnp.where(col <= row, 1.0, 0.0).astype(BF16)


def _select_chunk(key, t, need, carry, tri):
    eq = key == t
    pref = jnp.dot(tri, jnp.where(eq, 1.0, 0.0).astype(BF16), preferred_element_type=F32) + carry
    sel = jnp.logical_or(key > t, jnp.logical_and(eq, pref <= need))
    sel = jnp.logical_and(sel, key > INT_MIN)
    return sel, pref[key.shape[0] - 1:, :]


def _dsa_index_prompt_kernel(idx_ref, kd_ref, o_ref, key_sc, *, tq, ch, ksel):
    qi = pl.program_id(1)
    n_ch_total = o_ref.shape[0]
    nch = (qi * tq + tq + ch - 1) // ch
    qblk = idx_ref[...]
    lane = lax.broadcasted_iota(jnp.int32, (1, LANES), 1)
    lo = lane < IDX_DIM
    parts = []
    for h in range(IDX_HEADS):
        grp = qblk[:, (h // 2) * LANES:(h // 2 + 1) * LANES]
        parts.append(jnp.where(lo if h % 2 == 0 else jnp.logical_not(lo), grp, 0.0).astype(BF16))
    qstack = jnp.concatenate(parts, axis=0)
    w = qblk[:, 5 * LANES:6 * LANES] * (IDX_HEADS ** -0.5 * IDX_DIM ** -0.5)
    w_cols = [w[:, h:h + 1] for h in range(IDX_HEADS)]
    qpos = qi * tq + lane

    def score_body(c, _):
        base = pl.multiple_of(c * ch, ch)
        r = _bdot(qstack, kd_ref[:, pl.ds(base, ch)])
        score = jnp.zeros((tq, ch), F32)
        for h in range(IDX_HEADS):
            score = score + w_cols[h] * jnp.maximum(r[h * tq:(h + 1) * tq, :], 0.0)
        kpos = base + lax.broadcasted_iota(jnp.int32, (ch, tq), 0)
        key_sc[pl.ds(base, ch), :] = jnp.where(kpos <= qpos, _sortable_keys(jnp.transpose(score)), INT_MIN)
        return 0

    lax.fori_loop(0, nch, score_body, 0)
    t, need, surplus = _kth_threshold(key_sc, nch, ch, ksel)
    any_surplus = jnp.max(jnp.where(surplus, 1, 0)) > 0

    @pl.when(any_surplus)
    def _():
        tri = _tri_ones(ch)

        def out_body(c, carry):
            base = pl.multiple_of(c * ch, ch)
            sel, carry = _select_chunk(key_sc[pl.ds(base, ch), :], t, need, carry, tri)
            o_ref[c] = jnp.transpose(jnp.where(sel, 0.0, NEG))
            return carry

        lax.fori_loop(0, nch, out_body, jnp.zeros((1, LANES), F32))

    @pl.when(jnp.logical_not(any_surplus))
    def _():
        def out_body(c, _):
            key = key_sc[pl.ds(pl.multiple_of(c * ch, ch), ch), :]
            sel = jnp.logical_and(key >= t, key > INT_MIN)
            o_ref[c] = jnp.transpose(jnp.where(sel, 0.0, NEG))
            return 0

        lax.fori_loop(0, nch, out_body, 0)

    def fill_body(c, _):
        o_ref[c] = jnp.full((tq, ch), NEG, F32)
        return 0

    lax.fori_loop(nch, n_ch_total, fill_body, 0)


def _dsa_index_prompt(idx, kd_t, *, ch):
    b, s, w = idx.shape
    tq = LANES
    assert s % ch == 0 and ch % tq == 0
    ksel = min(IDX_TOPK_MAX, s // 4)
    return pl.pallas_call(
        functools.partial(_dsa_index_prompt_kernel, tq=tq, ch=ch, ksel=ksel),
        out_shape=jax.ShapeDtypeStruct((b, s // ch, s, ch), F32),
        grid=(b, s // tq),
        in_specs=[pl.BlockSpec((None, tq, w), lambda bi, qi: (bi, qi, 0)),
                  pl.BlockSpec((None, LANES, s), lambda bi, qi: (bi, 0, 0))],
        out_specs=pl.BlockSpec((None, s // ch, tq, ch), lambda bi, qi: (bi, 0, qi, 0)),
        scratch_shapes=[pltpu.VMEM((s, LANES), jnp.int32)],
        compiler_params=_cparams(("parallel", "arbitrary")),
        name="dsa_index_prompt",
    )(idx, kd_t)


def _dsa_score_sample_kernel(pt_ref, q_ref, w_ref, knew_ref, *refs, n_pages):
    del pt_ref
    page_refs = refs[:n_pages]
    o_ref = refs[n_pages]
    q = q_ref[...]
    w = w_ref[...] * (IDX_HEADS ** -0.5 * IDX_DIM ** -0.5)
    for p in range(n_pages):
        s = _bdot(q, page_refs[p][...])
        o_ref[p:p + 1, :] = jnp.sum(w * jnp.maximum(s, 0.0), axis=0, keepdims=True)
    s_new = jnp.sum(q * knew_ref[...], axis=-1, keepdims=True)
    sc_new = jnp.sum(w * jnp.maximum(s_new, 0.0), axis=0, keepdims=True)
    lane = lax.broadcasted_iota(jnp.int32, (1, PAGE_SIZE), 1)
    o_ref[n_pages:n_pages + 1, :] = jnp.where(lane == 0, sc_new, NEG)


def _dsa_score_sample(qi, wi, ki_new, cache_kidx, page_ids):
    bd = qi.shape[0]
    n_pages = page_ids.shape[1]
    assert cache_kidx.shape[1:] == (IDX_DIM, PAGE_SIZE)

    def page_spec(p):
        return pl.BlockSpec((None, IDX_DIM, PAGE_SIZE), lambda b, pt: (pt[b * n_pages + p], 0, 0))

    return pl.pallas_call(
        functools.partial(_dsa_score_sample_kernel, n_pages=n_pages),
        out_shape=jax.ShapeDtypeStruct((bd, n_pages + 1, PAGE_SIZE), F32),
        grid_spec=pltpu.PrefetchScalarGridSpec(
            num_scalar_prefetch=1,
            grid=(bd,),
            in_specs=[pl.BlockSpec((None, IDX_HEADS, IDX_DIM), lambda b, pt: (b, 0, 0)),
                      pl.BlockSpec((None, IDX_HEADS, 1), lambda b, pt: (b, 0, 0)),
                      pl.BlockSpec((None, 1, IDX_DIM), lambda b, pt: (b, 0, 0))]
                     + [page_spec(p) for p in range(n_pages)],
            out_specs=pl.BlockSpec((None, n_pages + 1, PAGE_SIZE), lambda b, pt: (b, 0, 0))),
        compiler_params=_cparams(("parallel",)),
        name="dsa_score_sample",
    )(page_ids.reshape(-1), qi, wi, ki_new, *([cache_kidx] * n_pages))


def _select_bias_kernel(s_ref, o_ref, key_sc, *, n_keys, ch, ksel):
    n_rows = s_ref.shape[0]
    nch = n_rows // ch
    kpos = lax.broadcasted_iota(jnp.int32, (n_rows, LANES), 0)
    key_sc[...] = jnp.where(kpos < n_keys, _sortable_keys(s_ref[...]), INT_MIN)
    t, need, _ = _kth_threshold(key_sc, nch, ch, ksel)
    tri = _tri_ones(ch)
    carry = jnp.zeros((1, LANES), F32)
    for c in range(nch):
        sel, carry = _select_chunk(key_sc[c * ch:(c + 1) * ch, :], t, need, carry, tri)
        o_ref[c * ch:(c + 1) * ch, :] = jnp.where(sel, 0.0, NEG)


def _select_bias(score_t, n_keys, ksel):
    n_rows, n_q = score_t.shape
    assert n_q == LANES and n_rows % LANES == 0
    return pl.pallas_call(
        functools.partial(_select_bias_kernel, n_keys=n_keys, ch=LANES, ksel=ksel),
        out_shape=jax.ShapeDtypeStruct((n_rows, LANES), F32),
        scratch_shapes=[pltpu.VMEM((n_rows, LANES), jnp.int32)],
        compiler_params=pltpu.CompilerParams(vmem_limit_bytes=VMEM_LIMIT),
        name="dsa_select_sample",
    )(score_t)


def _log_sigmoid(x):
    return -(jnp.maximum(-x, 0.0) + jnp.log1p(jnp.exp(-jnp.abs(x))))


def _gla_gate(glow, wgu, gate_b):
    return _log_sigmoid(_bdot(glow, wgu) + gate_b) / GLA_TAU


def _column(row_vec, eye):
    n = eye.shape[0]
    return jnp.sum(jnp.where(eye, jnp.broadcast_to(row_vec, (n, n)), 0.0), axis=-1, keepdims=True)


def _level_reference(b_ref, half, c):
    sub = lax.broadcasted_iota(jnp.int32, (SUBLANES, LANES), 0)
    slabs = []
    for t0 in range(0, c, SUBLANES):
        cur = None
        for t in range(t0, t0 + SUBLANES, min(SUBLANES, 2 * half)):
            r = (t // (2 * half)) * (2 * half) + half - 1
            bc = jnp.broadcast_to(b_ref[r:r + 1, :], (SUBLANES, LANES))
            cur = bc if cur is None else jnp.where(sub >= (t - t0), bc, cur)
        slabs.append(cur)
    return jnp.concatenate(slabs, axis=0)


def _gla_prompt_kernel(q_ref, k_ref, v_ref, r_ref, gl_ref, wgu_ref, gb_ref, ng_ref, o_ref, st_ref,
                       s_sc, b_sc, *, c, hps):
    ci = pl.program_id(2)
    hk, hv = GLA_HEAD_K, GLA_HEAD_V

    @pl.when(ci == 0)
    def _():
        s_sc[...] = jnp.zeros_like(s_sc)

    g_all = _gla_gate(gl_ref[...], wgu_ref[...], gb_ref[...])
    row = lax.broadcasted_iota(jnp.int32, (c, c), 0)
    col = lax.broadcasted_iota(jnp.int32, (c, c), 1)
    tri = jnp.where(col <= row, 1.0, 0.0).astype(BF16)
    eye = row == col
    halves = [c >> (lvl + 1) for lvl in range(c.bit_length() - 1)]
    valids = [jnp.logical_and(row // (2 * h) == col // (2 * h),
                              jnp.logical_and(row % (2 * h) >= h, col % (2 * h) < h)) for h in halves]
    for hh in range(hps):
        ksl = slice(hh * hk, (hh + 1) * hk)
        vsl = slice(hh * hv, (hh + 1) * hv)
        g = g_all[:, ksl]
        g1 = g.astype(BF16)
        rem = g - g1.astype(F32)
        g2 = rem.astype(BF16)
        g3 = (rem - g2.astype(F32)).astype(BF16)
        bcum = (jnp.dot(tri, g1, preferred_element_type=F32) + jnp.dot(tri, g2, preferred_element_type=F32)
                + jnp.dot(tri, g3, preferred_element_type=F32))
        b_ref = b_sc.at[hh]
        b_ref[...] = bcum
        q = q_ref[:, ksl] * (GLA_HEAD_K ** -0.5)
        k = k_ref[:, ksl]
        v = v_ref[:, vsl]
        att = jnp.where(eye, jnp.sum(q * k, axis=-1, keepdims=True), 0.0)
        for half, valid in zip(halves, valids):
            ref = _level_reference(b_ref, half, c)
            qt = q * jnp.exp(jnp.minimum(bcum - ref, 0.0))
            kt = k * jnp.exp(jnp.minimum(ref - bcum, 0.0))
            att = jnp.where(valid, _bdot_nt(qt, kt), att)
        s0 = s_sc[hh]
        o = _bdot(q * jnp.exp(bcum), s0) + _bdot(att, v)
        o = _rms_rows(o, ng_ref[...])
        rr = r_ref[:, vsl]
        o_ref[:, vsl] = o * (rr * _sigmoid(rr))
        b_last = b_ref[c - 1:c, :]
        khat = k * jnp.exp(b_last - bcum)
        s_new = _column(jnp.exp(b_last), eye) * s0 + _bdot_tn(khat, v)
        s_sc[hh] = s_new

    @pl.when(ci == pl.num_programs(2) - 1)
    def _():
        st_ref[...] = s_sc[...]


def _gla_prompt(qk, v, r, glow, wgu, gate_b, norm_g, *, c, hps):
    b, s, _ = qk.shape
    assert s % c == 0 and c == GLA_HEAD_K and GLA_HEADS % hps == 0
    hk, hv = GLA_HEAD_K, GLA_HEAD_V
    nhb = GLA_HEADS // hps
    return pl.pallas_call(
        functools.partial(_gla_prompt_kernel, c=c, hps=hps),
        out_shape=(jax.ShapeDtypeStruct((b, s, GLA_DV), F32),
                   jax.ShapeDtypeStruct((b, GLA_HEADS, hk, hv), F32)),
        grid=(b, nhb, s // c),
        in_specs=[pl.BlockSpec((None, c, hps * hk), lambda bi, h, ci: (bi, ci, h)),
                  pl.BlockSpec((None, c, hps * hk), lambda bi, h, ci: (bi, ci, nhb + h)),
                  pl.BlockSpec((None, c, hps * hv), lambda bi, h, ci: (bi, ci, h)),
                  pl.BlockSpec((None, c, hps * hv), lambda bi, h, ci: (bi, ci, h)),
                  pl.BlockSpec((None, c, LANES), lambda bi, h, ci: (bi, ci, 0)),
                  pl.BlockSpec((LANES, hps * hk), lambda bi, h, ci: (0, h)),
                  pl.BlockSpec((1, hps * hk), lambda bi, h, ci: (0, h)),
                  pl.BlockSpec((1, hv), lambda bi, h, ci: (0, 0))],
        out_specs=(pl.BlockSpec((None, c, hps * hv), lambda bi, h, ci: (bi, ci, h)),
                   pl.BlockSpec((None, hps, hk, hv), lambda bi, h, ci: (bi, h, 0, 0))),
        scratch_shapes=[pltpu.VMEM((hps, hk, hv), F32), pltpu.VMEM((hps, c, hk), F32)],
        compiler_params=_cparams(("parallel", "parallel", "arbitrary")),
        name="gla_prompt",
    )(qk, qk, v, r, glow, wgu, gate_b.reshape(1, GLA_DK), norm_g.reshape(1, hv))


def _gla_sample_kernel(qk_ref, v_ref, r_ref, gl_ref, wgu_ref, gb_ref, ng_ref, st_ref, o_ref, nst_ref):
    hk, hv = GLA_HEAD_K, GLA_HEAD_V
    glow = jnp.broadcast_to(gl_ref[...], (SUBLANES, LANES))
    g = _gla_gate(glow, wgu_ref[...], gb_ref[...])[0:1, :]
    eye = (lax.broadcasted_iota(jnp.int32, (hk, hk), 0) == lax.broadcasted_iota(jnp.int32, (hk, hk), 1))
    for h in range(GLA_HEADS):
        ksl = slice(h * hk, (h + 1) * hk)
        vsl = slice(h * hv, (h + 1) * hv)
        q_col = _column(qk_ref[:, ksl] * (GLA_HEAD_K ** -0.5), eye)
        k_col = _column(qk_ref[:, GLA_DK + h * hk:GLA_DK + (h + 1) * hk], eye)
        a_col = _column(jnp.exp(g[:, ksl]), eye)
        s_new = a_col * st_ref[h] + k_col * v_ref[:, vsl]
        nst_ref[h] = s_new
        o = jnp.sum(q_col * s_new, axis=0, keepdims=True)
        rr = r_ref[:, vsl]
        o_ref[:, vsl] = _rms_rows(o, ng_ref[...]) * (rr * _sigmoid(rr))


def _gla_sample(qk, v, r, glow, wgu, gate_b, norm_g, state, state_row0):
    bd = qk.shape[0]
    hk, hv = GLA_HEAD_K, GLA_HEAD_V
    rowspec = lambda w: pl.BlockSpec((None, 1, w), lambda b: (b, 0, 0))
    fixed2 = lambda b: (0, 0)
    o, nst = pl.pallas_call(
        _gla_sample_kernel,
        out_shape=(jax.ShapeDtypeStruct((bd, 1, GLA_DV), F32),
                   jax.ShapeDtypeStruct((bd, GLA_HEADS, hk, hv), F32)),
        grid=(bd,),
        in_specs=[rowspec(2 * GLA_DK), rowspec(GLA_DV), rowspec(GLA_DV), rowspec(LANES),
                  pl.BlockSpec((LANES, GLA_DK), fixed2), pl.BlockSpec((1, GLA_DK), fixed2),
                  pl.BlockSpec((1, hv), fixed2),
                  pl.BlockSpec((None, GLA_HEADS, hk, hv), lambda b: (state_row0 + b, 0, 0, 0))],
        out_specs=(rowspec(GLA_DV), pl.BlockSpec((None, GLA_HEADS, hk, hv), lambda b: (b, 0, 0, 0))),
        compiler_params=_cparams(("parallel",)),
        name="gla_sample",
    )(qk.reshape(bd, 1, -1), v.reshape(bd, 1, -1), r.reshape(bd, 1, -1), glow.reshape(bd, 1, -1),
      wgu, gate_b.reshape(1, GLA_DK), norm_g.reshape(1, hv), state)
    return o.reshape(bd, GLA_DV), nst


FLASH_TQ = 512
FLASH_TK = 512


def _row_tile(n, cap):
    t = cap
    while n % t:
        t //= 2
    return t


def _project(x, w, layer, col0, width, rope=None, n_rope=None, tn=None):
    tm = _row_tile(x.shape[0], 1024)
    if tn is None:
        tn = width if width <= D_MODEL else (512 if width % 512 == 0 else LANES)
    return _linear(x, w, layer, col0, width, tm=tm, tn=tn, rope=rope, n_rope=n_rope)


def _pages_per_step(n_pages):
    for n in (16, 4, 2):
        if n_pages % n == 0:
            return n
    return 1


def _project_t(x, w_t, nb, rope=None, with_bf16=False, w_extra_t=None, stack=None):
    s = x.shape[0] // nb
    return _linear_t(x, w_t, nb, tc=min(D_MODEL, w_t.shape[0]), tt=_row_tile(s, 512), rope=rope, with_bf16=with_bf16,
                     w_extra_t=w_extra_t, stack=stack)


def _heads_last(x_t, n_heads):
    b, c, s = x_t.shape
    return x_t.reshape(b, n_heads, c // n_heads, s).transpose(0, 3, 1, 2)


def _diff_mixer(xp, xs, dims, cache_k, cache_v, page_ids, w_in, j, lam_params, subln_g, lam_init, ropes,
                k_stack, v_stack):
    b, s, bd = dims
    d = D_MODEL
    n_diff = w_in.shape[0]
    rope_p, rope_s, rope_pt, rope_st = ropes
    w_k_t = w_in[j][:, d:2 * d].T
    tm = _row_tile(b * s, 512)
    q2_p = _q_pairs(xp, w_in, j, rope_p, tm=_row_tile(s, 512))
    k_stack, k_pt16 = _project_t(xp, w_k_t, b, rope_pt, with_bf16=True,
                                 stack=(n_diff, j, k_stack))
    v_stack, v_pg = _linear_heads(xp, w_in, j, 2 * d, DIFF_HEADS, tm=tm, groups=True,
                                  stack=(n_diff, j, v_stack))
    o_p = _flash_attention(q2_p, k_pt16, v_pg, b, mode="diff", tq=_row_tile(s, FLASH_TQ), tk=_row_tile(s, FLASH_TK),
                           lam_params=lam_params, subln_g=subln_g, lam_init=lam_init)
    q_s = _project(xs, w_in, j, 0, d, rope_s)
    k_s = _project(xs, w_in, j, d, d, rope_s)
    v_s = _project(xs, w_in, j, 2 * d, d)
    k_st = _project_t(xs, w_k_t, 1, rope_st)
    v_sh = _linear_heads(xs, w_in, j, 2 * d, DIFF_HEADS, tm=_row_tile(bd, 512))
    o_s = _decode_attention(q_s, cache_k, cache_v, page_ids, k_s, v_s, mode="diff",
                            n_pp=_pages_per_step(page_ids.shape[1]),
                            lam_params=lam_params, subln_g=subln_g, lam_init=lam_init)
    k_leaf_s = _heads_last(k_st, 2 * DIFF_HEADS).reshape(bd, 1, 2 * DIFF_HEADS, DIFF_HEAD_DIM)
    v_leaf_s = v_sh.reshape(bd, 1, DIFF_HEADS, LANES)
    return o_p.reshape(b * s, d), o_s, k_stack, v_stack, k_leaf_s, v_leaf_s


def _dsa_mixer(xp, xs, dims, cache_k, cache_v, cache_kidx, page_ids, w_in, j, ropes, past_len):
    b, s, bd = dims
    d = D_MODEL
    rope_p, rope_s, rope_pt, rope_st = ropes
    w = w_in[j]
    c_qi, c_ki, c_wi = 3 * d, 3 * d + IDX_HEADS * IDX_DIM, 3 * d + IDX_HEADS * IDX_DIM + IDX_DIM
    w_idx = jnp.concatenate([w[:, c_qi:c_ki], w[:, c_ki:c_wi], w[:, c_ki:c_wi], w[:, c_wi:c_wi + IDX_HEADS],
                             jnp.zeros((d, LANES - IDX_HEADS), F32)], axis=1)[None]
    idx_w = w_idx.shape[2]
    w_k_t = w[:, d:2 * d].T
    w_v_t = w[:, 2 * d:3 * d].T
    w_ki2_t = jnp.concatenate([w[:, c_ki:c_wi], w[:, c_ki:c_wi]], axis=1).T

    q2_p = _q_pairs(xp, w_in, j, rope_p, tm=_row_tile(s, 512))
    k_pt, k_pt16, kd_pt = _project_t(xp, w_k_t, b, rope_pt, with_bf16=True, w_extra_t=w_ki2_t)
    v_pt = _project_t(xp, w_v_t, b)
    v_pg = _linear_heads(xp, w_in, j, 2 * d, N_GROUPS, tm=_row_tile(b * s, 512), leaf=False, groups=True)
    idx_p = _project(xp, w_idx, 0, 0, idx_w, rope_p, n_rope=5, tn=idx_w)
    t = _row_tile(s, FLASH_TK)
    bias_p =_dsa_index_prompt(idx_p.reshape(b, s, idx_w), kd_pt, ch=t)
    o_p = _flash_attention(q2_p, k_pt16, v_pg, b, mode="dsa", tq=_row_tile(s, FLASH_TQ), tk=t, bias=bias_p)

    q_s = _project(xs, w_in, j, 0, d, rope_s)
    k_s = _project(xs, w_in, j, d, d, rope_s)
    v_s = _project(xs, w_in, j, 2 * d, d)
    k_st, kd_st = _project_t(xs, w_k_t, 1, rope_st, w_extra_t=w_ki2_t)
    v_st = _project_t(xs, w_v_t, 1)
    idx_s = _project(xs, w_idx, 0, 0, idx_w, rope_s, n_rope=5, tn=idx_w)
    qi_s = idx_s[:, :IDX_HEADS * IDX_DIM].reshape(bd, IDX_HEADS, IDX_DIM)
    ki_s = idx_s[:, IDX_HEADS * IDX_DIM:IDX_HEADS * IDX_DIM + IDX_DIM]
    wi_s = idx_s[:, 5 * LANES:5 * LANES + IDX_HEADS].reshape(bd, IDX_HEADS, 1)
    n_pages = page_ids.shape[1]
    scores = _dsa_score_sample(qi_s, wi_s, ki_s.reshape(bd, 1, IDX_DIM), cache_kidx, page_ids)
    n_keys = past_len + 1
    bias_t = _select_bias(scores.reshape(bd, (n_pages + 1) * PAGE_SIZE).T, n_keys, min(IDX_TOPK_MAX, n_keys // 4))
    bias_s = bias_t.T.reshape(bd, n_pages + 1, PAGE_SIZE)
    o_s = _decode_attention(q_s, cache_k, cache_v, page_ids, k_s, v_s, mode="dsa",
                            n_pp=_pages_per_step(n_pages), bias=bias_s)
    leaves_p = (_heads_last(k_pt, DSA_HEADS), _heads_last(v_pt, DSA_HEADS), kd_pt[:, :IDX_DIM, :].transpose(0, 2, 1))
    leaves_s = (_heads_last(k_st, DSA_HEADS).reshape(bd, 1, DSA_HEADS, DSA_HEAD_DIM),
                _heads_last(v_st, DSA_HEADS).reshape(bd, 1, DSA_HEADS, DSA_HEAD_DIM),
                kd_st[0, :IDX_DIM, :].T.reshape(bd, 1, IDX_DIM))
    return (o_p.reshape(b * s, d), o_s) + leaves_p + leaves_s


def _gla_mixer(xp, xs, dims, state, j, w_in, w_gate_up, gate_b, norm_g):
    b, s, bd = dims
    d = D_MODEL
    c_g = 2 * GLA_DK + 2 * GLA_DV
    w_g = jnp.pad(w_in[j][:, c_g:c_g + GLA_GATE_RANK], ((0, 0), (0, LANES - GLA_GATE_RANK)))[None]
    wgu = jnp.pad(w_gate_up[j], ((0, LANES - GLA_GATE_RANK), (0, 0)))
    outs = []
    for x in (xp, xs):
        qk = _project(x, w_in, j, 0, 2 * GLA_DK)
        v = _project(x, w_in, j, 2 * GLA_DK, GLA_DV)
        r = _project(x, w_in, j, 2 * GLA_DK + GLA_DV, GLA_DV)
        glow = _project(x, w_g, 0, 0, LANES, tn=LANES)
        outs.append((qk, v, r, glow))
    qk, v, r, glow = outs[0]
    o_p, st_p = _gla_prompt(qk.reshape(b, s, -1), v.reshape(b, s, -1), r.reshape(b, s, -1), glow.reshape(b, s, -1),
                            wgu, gate_b[j], norm_g[j], c=GLA_HEAD_K, hps=GLA_HEADS)
    qk, v, r, glow = outs[1]
    n_state = state.shape[1]
    o_s, st_s = _gla_sample(qk, v, r, glow, wgu, gate_b[j], norm_g[j],
                            state.reshape((-1,) + state.shape[2:]), j * n_state)
    return o_p.reshape(b * s, d), o_s, st_p, st_s


def kernel(x_prompt, x_sample, cache_diff_k, cache_diff_v, cache_dsa_k, cache_dsa_v, cache_dsa_kidx, state_gla,
           page_table, ln_mix_g, ln_mix_b, ln_ffn_g, ln_ffn_b, ffn_w_gate_up, ffn_w_down, diff_w_in, diff_lambda,
           diff_subln_g, diff_w_out, dsa_w_in, dsa_w_out, gla_w_in, gla_w_gate_up, gla_gate_b, gla_norm_g,
           gla_w_out):
    b, s, d = x_prompt.shape
    bd, s_d, _ = x_sample.shape
    assert s_d == 1 and d == D_MODEL
    dims = (b, s, bd)
    n_pool, page = cache_diff_k.shape[1], cache_diff_k.shape[2]
    past_len = page_table.shape[1] * page
    xp = x_prompt.reshape(b * s, d)
    xs = x_sample.reshape(bd, d)
    pos_p = jnp.arange(s, dtype=jnp.int32)
    pos_s = jnp.full((bd,), past_len, dtype=jnp.int32)
    ropes = (_rope_tables(pos_p), _rope_tables(pos_s), _rope_tables_t(pos_p), _rope_tables_t(pos_s))

    def feature_major(c):
        perm = (0, 1) + tuple(range(3, c.ndim)) + (2,)
        return jnp.transpose(c, perm).reshape(c.shape[0] * c.shape[1], -1, page)

    cdk, csk, csv, csi = (feature_major(c) for c in (cache_diff_k, cache_dsa_k, cache_dsa_v, cache_dsa_kidx))
    cdv = cache_diff_v.reshape(cache_diff_v.shape[0] * n_pool, page * DIFF_HEADS, LANES)
    tm_p = _row_tile(b * s, 1024)
    tm_s = _row_tile(bd, 512)
    tf = 256
    dk_stack = dv_stack = None
    dk_s, dv_s = [], []
    sk_p, sv_p, si_p, sk_s, sv_s, si_s = [], [], [], [], [], []
    gs_p, gs_s = [], []
    n_diff = diff_w_in.shape[0]
    for i in range(DEPTH):
        kind, j = i % 3, i // 3
        page_ids = page_table + j * n_pool
        if kind == 0:
            lam_init = 0.8 - 0.6 * math.exp(-0.3 * i)
            o_p, o_s, dk_stack, dv_stack, k_s, v_s = _diff_mixer(
                xp, xs, dims, cdk, cdv, page_ids, diff_w_in, j, diff_lambda[j], diff_subln_g[j], lam_init, ropes,
                dk_stack, dv_stack)
            dk_s.append(k_s)
            dv_s.append(v_s)
            w_out = diff_w_out
        elif kind == 1:
            o_p, o_s, k_p, v_p, ki_p, k_s, v_s, ki_s = _dsa_mixer(xp, xs, dims, csk, csv, csi, page_ids, dsa_w_in, j,
                                                                  ropes, past_len)
            sk_p.append(k_p)
            sv_p.append(v_p)
            si_p.append(ki_p)
            sk_s.append(k_s)
            sv_s.append(v_s)
            si_s.append(ki_s)
            w_out = dsa_w_out
        else:
            o_p, o_s, st_p, st_s = _gla_mixer(xp, xs, dims, state_gla, j, gla_w_in, gla_w_gate_up, gla_gate_b,
                                              gla_norm_g)
            gs_p.append(st_p)
            gs_s.append(st_s)
            w_out = gla_w_out
        xp = _linear_res_ln(o_p, w_out, j, xp, ln_mix_g, ln_mix_b, i, tm=tm_p)
        xs = _linear_res_ln(o_s, w_out, j, xs, ln_mix_g, ln_mix_b, i, tm=tm_s)
        xp = _ffn_ln(xp, ffn_w_gate_up, ffn_w_down, ln_ffn_g, ln_ffn_b, i, tm=_row_tile(b * s, 1024), tf=tf)
        xs = _ffn_ln(xs, ffn_w_gate_up, ffn_w_down, ln_ffn_g, ln_ffn_b, i, tm=tm_s, tf=tf)
    dk_p = dk_stack.reshape(n_diff, b, 2 * DIFF_HEADS, DIFF_HEAD_DIM, s).transpose(0, 1, 4, 2, 3)
    dv_p = dv_stack.reshape(n_diff, b, s, DIFF_HEADS, LANES)
    return (xp.reshape(b, s, d), xs.reshape(bd, 1, d),
            dk_p, dv_p, jnp.stack(dk_s), jnp.stack(dv_s),
            jnp.stack(sk_p), jnp.stack(sv_p), jnp.stack(si_p),
            jnp.stack(sk_s), jnp.stack(sv_s), jnp.stack(si_s),
            jnp.stack(gs_p), jnp.stack(gs_s))
```
